```python
import jax, jax.numpy as jnp
from jax import lax
import numpy as np

D_MODEL = 1024
BATCH = 8
SEQ = 4096
DEPTH = 2

MIX_W = 512
N_BRANCH = 4
POOL_WINDOWS = (2, 4, 8, 16)
POOL_GROUPS = 4
POOL_GW = MIX_W // POOL_GROUPS
ATT_HEADS = 8
ATT_KV_HEADS = 2
HEAD_DIM = 64
ROPE_DIM = HEAD_DIM // 4
ROPE_THETA = 500000.0
IDX_HEADS = 8
IDX_DIM = 32
IDX_ROPE_DIM = IDX_DIM // 4
TOPK_MAX = 256
Q_BLOCK = 128
CONV_KERNEL = 31
SSM_HEADS = 8
SSM_HEAD_DIM = 64
SSM_GROUPS = 2
SSM_STATE = 64
SSM_CONV = 4
SSM_XBC = MIX_W + 2 * SSM_GROUPS * SSM_STATE
SSM_CHUNK = 128
D_FF = 2816
N_EXPERTS = 8
TOP_K = 2
D_FF_EXPERT = 3584
N_DENSE = (DEPTH + 1) // 2
N_MOE = DEPTH // 2
NORM_EPS = 1e-6
SPLIT_SIZES = (MIX_W, ATT_HEADS * HEAD_DIM, 2 * ATT_KV_HEADS * HEAD_DIM, IDX_HEADS * IDX_DIM, IDX_DIM, IDX_HEADS,
               2 * MIX_W, MIX_W, SSM_XBC, SSM_HEADS, N_BRANCH * D_MODEL)
IN_W = sum(SPLIT_SIZES)

kernel_name = 'hybrid_pool_dsa_conformer_ssd_moe'


def rms_norm(x, g):
    xf = x.astype(jnp.float32)
    y = xf * lax.rsqrt(jnp.mean(xf * xf, axis=-1, keepdims=True) + NORM_EPS)
    return (y * g.astype(jnp.float32)).astype(x.dtype)


def layer_norm(x, g, b):
    xf = x.astype(jnp.float32)
    mu = jnp.mean(xf, axis=-1, keepdims=True)
    xc = xf - mu
    y = xc * lax.rsqrt(jnp.mean(xc * xc, axis=-1, keepdims=True) + NORM_EPS)
    return (y * g.astype(jnp.float32) + b.astype(jnp.float32)).astype(x.dtype)


def rope_partial(x, rot_dim):
    seq = x.shape[1]
    half = rot_dim // 2
    inv_freq = jnp.power(jnp.float32(ROPE_THETA), -jnp.arange(half, dtype=jnp.float32) * (2.0 / rot_dim))
    ang = jnp.arange(seq, dtype=jnp.float32)[:, None] * inv_freq[None, :]
    cos = jnp.cos(ang)[None, :, None, :]
    sin = jnp.sin(ang)[None, :, None, :]
    xf = x.astype(jnp.float32)
    x1 = xf[..., :half]
    x2 = xf[..., half:rot_dim]
    out = jnp.concatenate([x1 * cos - x2 * sin, x2 * cos + x1 * sin, xf[..., rot_dim:]], axis=-1)
    return out.astype(x.dtype)


def causal_depthwise_conv(x, w, b):
    width, ch = w.shape
    y = lax.conv_general_dilated(x, w[:, None, :].astype(x.dtype), window_strides=(1,),
                                 padding=[(width - 1, 0)], dimension_numbers=('NWC', 'WIO', 'NWC'),
                                 feature_group_count=ch)
    return y + b.astype(x.dtype)


def pool_mixer(u, w_grp, scale):
    bsz, seq, _ = u.shape
    uf = u.astype(jnp.float32)
    csum = jnp.cumsum(uf, axis=1)
    t = jnp.arange(seq)
    means = []
    for g, win in enumerate(POOL_WINDOWS):
        cs = csum[..., g * POOL_GW:(g + 1) * POOL_GW]
        cs_prev = jnp.pad(cs, ((0, 0), (win, 0), (0, 0)))[:, :seq]
        cnt = jnp.minimum(t + 1, win).astype(jnp.float32)[None, :, None]
        means.append((cs - cs_prev) / cnt)
    p = (jnp.concatenate(means, axis=-1) - uf).reshape(bsz, seq, POOL_GROUPS, POOL_GW)
    y = jnp.einsum('bsgc,gcd->bsgd', p, w_grp.astype(jnp.float32)).reshape(bsz, seq, MIX_W)
    return (y * scale.astype(jnp.float32)).astype(u.dtype)


def dsa_attention(q, k, v, q_idx, k_idx, w_idx):
    bsz, seq, n_heads, hd = q.shape
    n_kv = k.shape[2]
    top_k = min(TOPK_MAX, seq // 4)
    n_blocks = seq // Q_BLOCK
    key_pos = jnp.arange(seq)
    scale = hd ** -0.5
    gather = jax.vmap(lambda a, i: a[i])

    def to_blocks(a):
        return jnp.moveaxis(a.reshape((bsz, n_blocks, Q_BLOCK) + a.shape[2:]), 1, 0)

    def one_block(args):
        qb, qib, wb, t0 = args
        pos_q = t0 + jnp.arange(Q_BLOCK)
        s_h = jax.nn.relu(jnp.einsum('bqhd,bsd->bqhs', qib, k_idx))
        score = jnp.einsum('bqhs,bqh->bqs', s_h, wb).astype(jnp.float32)
        causal = key_pos[None, :] <= pos_q[:, None]
        score = jnp.where(causal[None], score, -jnp.inf)
        _, idx = lax.top_k(score, top_k)
        valid = idx <= pos_q[None, :, None]
        ks = gather(k, idx)
        vs = gather(v, idx)
        qg = qb.reshape(bsz, Q_BLOCK, n_kv, n_heads // n_kv, hd)
        logits = jnp.einsum('bqgrd,bqkgd->bqgrk', qg, ks).astype(jnp.float32) * scale
        logits = jnp.where(valid[:, :, None, None, :], logits, -jnp.inf)
        p = jax.nn.softmax(logits, axis=-1).astype(v.dtype)
        o = jnp.einsum('bqgrk,bqkgd->bqgrd', p, vs)
        return o.reshape(bsz, Q_BLOCK, n_heads * hd)

    out = lax.map(one_block, (to_blocks(q), to_blocks(q_idx), to_blocks(w_idx),
                              jnp.arange(n_blocks) * Q_BLOCK))
    return jnp.moveaxis(out, 0, 1).reshape(bsz, seq, n_heads * hd)


def conformer_conv(u, dw, db, ln_g, ln_b):
    a, gt = jnp.split(u, 2, axis=-1)
    y = a * jax.nn.sigmoid(gt)
    y = causal_depthwise_conv(y, dw, db)
    return jax.nn.silu(layer_norm(y, ln_g, ln_b))


def ssd_scan(x, dt, a, bm, cm):
    bsz, seq, nh, hp = x.shape
    ng, ns = bm.shape[2], bm.shape[3]
    rep = nh // ng
    nc = seq // SSM_CHUNK
    xd = (x.astype(jnp.float32) * dt[..., None]).reshape(bsz, nc, SSM_CHUNK, nh, hp)
    da = (dt * a).reshape(bsz, nc, SSM_CHUNK, nh).transpose(0, 3, 1, 2)
    bc = bm.astype(jnp.float32).reshape(bsz, nc, SSM_CHUNK, ng, ns)
    cc = cm.astype(jnp.float32).reshape(bsz, nc, SSM_CHUNK, ng, ns)
    a_cum = jnp.cumsum(da, axis=-1)
    tri = jnp.tril(jnp.ones((SSM_CHUNK, SSM_CHUNK), dtype=bool))
    seg = a_cum[..., :, None] - a_cum[..., None, :]
    lmat = jnp.exp(jnp.where(tri, seg, -jnp.inf)).transpose(0, 2, 1, 3, 4)
    cb = jnp.repeat(jnp.einsum('bclgn,bcsgn->bcgls', cc, bc), rep, axis=2)
    y_diag = jnp.einsum('bchls,bcshp->bclhp', cb * lmat, xd)
    bh = jnp.repeat(bc, rep, axis=3)
    ch = jnp.repeat(cc, rep, axis=3)
    decay_states = jnp.exp(a_cum[..., -1:] - a_cum)
    states = jnp.einsum('bclhn,bhcl,bclhp->bchpn', bh, decay_states, xd)
    chunk_decay = jnp.moveaxis(jnp.exp(a_cum[..., -1]), 2, 0)

    def step(carry, inp):
        dec, st = inp
        return carry * dec[..., None, None] + st, carry

    init = jnp.zeros((bsz, nh, hp, ns), jnp.float32)
    _, prev = lax.scan(step, init, (chunk_decay, jnp.moveaxis(states, 1, 0)))
    prev = jnp.moveaxis(prev, 0, 1)
    y_off = jnp.einsum('bclhn,bchpn,bhcl->bclhp', ch, prev, jnp.exp(a_cum))
    return (y_diag + y_off).reshape(bsz, seq, nh, hp)


def mamba2_mixer(z, xbc, dt_raw, conv_w, conv_b, dt_bias, a_log, d_skip, norm_g):
    bsz, seq, _ = xbc.shape
    xbc = jax.nn.silu(causal_depthwise_conv(xbc, conv_w, conv_b))
    xs, bm, cm = jnp.split(xbc, [MIX_W, MIX_W + SSM_GROUPS * SSM_STATE], axis=-1)
    xs = xs.reshape(bsz, seq, SSM_HEADS, SSM_HEAD_DIM)
    bm = bm.reshape(bsz, seq, SSM_GROUPS, SSM_STATE)
    cm = cm.reshape(bsz, seq, SSM_GROUPS, SSM_STATE)
    dt = jax.nn.softplus(dt_raw.astype(jnp.float32) + dt_bias.astype(jnp.float32))
    a = -jnp.exp(a_log.astype(jnp.float32))
    y = ssd_scan(xs, dt, a, bm, cm) + d_skip.astype(jnp.float32)[:, None] * xs.astype(jnp.float32)
    y = y.reshape(bsz, seq, MIX_W) * jax.nn.silu(z.astype(jnp.float32))
    return rms_norm(y, norm_g)


def hybrid_mixer(h, w_in, pool_w, pool_scale, conv_dw, conv_b, conv_ln_g, conv_ln_b,
                 ssm_conv_w, ssm_conv_b, ssm_dt_bias, ssm_a_log, ssm_d, ssm_norm, w_br, w_out):
    bsz, seq, _ = h.shape
    offsets = np.cumsum(SPLIT_SIZES)[:-1].tolist()
    (u_pool, q, kv, qi, ki, wi, u_conv, z, xbc, dt_raw, gate_logits) = jnp.split(h @ w_in, offsets, axis=-1)
    y_a = pool_mixer(u_pool, pool_w, pool_scale)
    q = rope_partial(q.reshape(bsz, seq, ATT_HEADS, HEAD_DIM), ROPE_DIM)
    k, v = jnp.split(kv, 2, axis=-1)
    k = rope_partial(k.reshape(bsz, seq, ATT_KV_HEADS, HEAD_DIM), ROPE_DIM)
    v = v.reshape(bsz, seq, ATT_KV_HEADS, HEAD_DIM)
    qi = rope_partial(qi.reshape(bsz, seq, IDX_HEADS, IDX_DIM), IDX_ROPE_DIM)
    ki = rope_partial(ki.reshape(bsz, seq, 1, IDX_DIM), IDX_ROPE_DIM)[:, :, 0, :]
    y_b = dsa_attention(q, k, v, qi, ki, wi)
    y_c = conformer_conv(u_conv, conv_dw, conv_b, conv_ln_g, conv_ln_b)
    y_d = mamba2_mixer(z, xbc, dt_raw, ssm_conv_w, ssm_conv_b, ssm_dt_bias, ssm_a_log, ssm_d, ssm_norm)
    branches = jnp.stack([y_a.astype(h.dtype), y_b.astype(h.dtype), y_c.astype(h.dtype), y_d.astype(h.dtype)], axis=2)
    u = jnp.einsum('bsnc,ncd->bsnd', branches, w_br)
    gates = jax.nn.sigmoid(gate_logits.reshape(bsz, seq, N_BRANCH, D_MODEL))
    merged = jnp.einsum('bsnd,bsnd->bsd', gates, u)
    return merged @ w_out


def swiglu(h, wg, wu, wd):
    return (jax.nn.silu(h @ wg) * (h @ wu)) @ wd


def moe_swiglu(h, router, wg, wu, wd):
    logits = (h @ router).astype(jnp.float32)
    top_val, top_idx = lax.top_k(logits, TOP_K)
    probs = jax.nn.softmax(top_val, axis=-1)
    combine = jnp.einsum('bske,bsk->bse', jax.nn.one_hot(top_idx, N_EXPERTS, dtype=jnp.float32), probs)
    out = jnp.zeros(h.shape, jnp.float32)
    for e in range(N_EXPERTS):
        out = out + combine[..., e:e + 1] * swiglu(h, wg[e], wu[e], wd[e]).astype(jnp.float32)
    return out.astype(h.dtype)


def setup_inputs(seed: int = 0) -> dict:
    key = jax.random.key(seed)
    ks = iter(jax.random.split(key, 32))
    f32 = jnp.float32

    def nrm(shape, s):
        return jax.random.normal(next(ks), shape, f32) * s

    L, D = DEPTH, D_MODEL
    x = nrm((BATCH, SEQ, D), 1.0)
    norm_mix = 1.0 + nrm((L, D), 0.02)
    w_in = nrm((L, D, IN_W), D ** -0.5)
    pool_w = nrm((L, POOL_GROUPS, POOL_GW, POOL_GW), POOL_GW ** -0.5)
    pool_scale = 1.0 + nrm((L, MIX_W), 0.1)
    conv_dw = nrm((L, CONV_KERNEL, MIX_W), CONV_KERNEL ** -0.5)
    conv_b = nrm((L, MIX_W), 0.02)
    conv_ln_g = 1.0 + nrm((L, MIX_W), 0.02)
    conv_ln_b = nrm((L, MIX_W), 0.02)
    ssm_conv_w = nrm((L, SSM_CONV, SSM_XBC), SSM_CONV ** -0.5)
    ssm_conv_b = nrm((L, SSM_XBC), 0.02)
    dt0 = jnp.exp(jax.random.uniform(next(ks), (L, SSM_HEADS), f32, jnp.log(1e-3), jnp.log(1e-1)))
    ssm_dt_bias = dt0 + jnp.log(-jnp.expm1(-dt0))
    ssm_a_log = jnp.log(jax.random.uniform(next(ks), (L, SSM_HEADS), f32, 1.0, 16.0))
    ssm_d = 1.0 + nrm((L, SSM_HEADS), 0.1)
    ssm_norm = 1.0 + nrm((L, MIX_W), 0.02)
    w_br = nrm((L, N_BRANCH, MIX_W, D), MIX_W ** -0.5)
    w_out = nrm((L, D, D), D ** -0.5)
    norm_ffn = 1.0 + nrm((L, D), 0.02)
    ffn_w_gate = nrm((N_DENSE, D, D_FF), D ** -0.5)
    ffn_w_up = nrm((N_DENSE, D, D_FF), D ** -0.5)
    ffn_w_down = nrm((N_DENSE, D_FF, D), D_FF ** -0.5)
    moe_router = nrm((N_MOE, D, N_EXPERTS), D ** -0.5)
    moe_w_gate = nrm((N_MOE, N_EXPERTS, D, D_FF_EXPERT), D ** -0.5)
    moe_w_up = nrm((N_MOE, N_EXPERTS, D, D_FF_EXPERT), D ** -0.5)
    moe_w_down = nrm((N_MOE, N_EXPERTS, D_FF_EXPERT, D), D_FF_EXPERT ** -0.5)
    final_norm = 1.0 + nrm((D,), 0.02)
    return {'x': x, 'norm_mix': norm_mix, 'w_in': w_in, 'pool_w': pool_w, 'pool_scale': pool_scale,
            'conv_dw': conv_dw, 'conv_b': conv_b, 'conv_ln_g': conv_ln_g, 'conv_ln_b': conv_ln_b,
            'ssm_conv_w': ssm_conv_w, 'ssm_conv_b': ssm_conv_b, 'ssm_dt_bias': ssm_dt_bias,
            'ssm_a_log': ssm_a_log, 'ssm_d': ssm_d, 'ssm_norm': ssm_norm, 'w_br': w_br, 'w_out': w_out,
            'norm_ffn': norm_ffn, 'ffn_w_gate': ffn_w_gate, 'ffn_w_up': ffn_w_up, 'ffn_w_down': ffn_w_down,
            'moe_router': moe_router, 'moe_w_gate': moe_w_gate, 'moe_w_up': moe_w_up, 'moe_w_down': moe_w_down,
            'final_norm': final_norm}


def reference(x, norm_mix, w_in, pool_w, pool_scale, conv_dw, conv_b, conv_ln_g, conv_ln_b,
              ssm_conv_w, ssm_conv_b, ssm_dt_bias, ssm_a_log, ssm_d, ssm_norm, w_br, w_out,
              norm_ffn, ffn_w_gate, ffn_w_up, ffn_w_down, moe_router, moe_w_gate, moe_w_up, moe_w_down,
              final_norm):
    for layer in range(DEPTH):
        h = rms_norm(x, norm_mix[layer])
        x = x + hybrid_mixer(h, w_in[layer], pool_w[layer], pool_scale[layer], conv_dw[layer], conv_b[layer],
                             conv_ln_g[layer], conv_ln_b[layer], ssm_conv_w[layer], ssm_conv_b[layer],
                             ssm_dt_bias[layer], ssm_a_log[layer], ssm_d[layer], ssm_norm[layer],
                             w_br[layer], w_out[layer])
        h = rms_norm(x, norm_ffn[layer])
        j = layer // 2
        if layer % 2 == 0:
            x = x + swiglu(h, ffn_w_gate[j], ffn_w_up[j], ffn_w_down[j])
        else:
            x = x + moe_swiglu(h, moe_router[j], moe_w_gate[j], moe_w_up[j], moe_w_down[j])
    return rms_norm(x, final_norm)
```

```python
import functools

import numpy as np
import jax
import jax.numpy as jnp
from jax import lax
from jax.experimental import pallas as pl
from jax.experimental.pallas import tpu as pltpu

POOL_WINDOWS = (2, 4, 8, 16)
ATT_HEADS = 8
ATT_KV_HEADS = 2
HEAD_DIM = 64
ROPE_DIM = HEAD_DIM // 4
ROPE_THETA = 500000.0
IDX_HEADS = 8
IDX_DIM = 32
IDX_ROPE_DIM = IDX_DIM // 4
TOPK_MAX = 256
SSM_HEADS = 8
SSM_HEAD_DIM = 64
SSM_GROUPS = 2
SSM_STATE = 64
SSM_CHUNK = 128
TOP_K = 2
NORM_EPS = 1e-6

LANES = 128
SUBLANES = 8
VMEM_LIMIT_BYTES = 56 * 1024 * 1024

F32 = jnp.float32
BF16 = jnp.bfloat16
INT_MIN = -(2 ** 31)
NEG_BIG = -1e30


def _cparams(n_axes):
    return pltpu.CompilerParams(dimension_semantics=("arbitrary",) * n_axes,
                                vmem_limit_bytes=VMEM_LIMIT_BYTES)


def _sigmoid(x):
    return 1.0 / (1.0 + jnp.exp(-x))


def _rms_rows(x, g):
    return x * lax.rsqrt(jnp.mean(x * x, axis=-1, keepdims=True) + NORM_EPS) * g


def _norm_matmul_kernel(x_ref, g_ref, w_ref, o_ref, h_ref):
    @pl.when(pl.program_id(1) == 0)
    def _():
        h_ref[...] = _rms_rows(x_ref[...], g_ref[...]).astype(h_ref.dtype)

    o_ref[...] = jnp.dot(h_ref[...], w_ref[...], preferred_element_type=F32)


def _norm_matmul(x, g, w, tm, tn):
    t, d = x.shape
    n = w.shape[1]
    return pl.pallas_call(
        _norm_matmul_kernel,
        grid=(t // tm, n // tn),
        in_specs=[pl.BlockSpec((tm, d), lambda i, j: (i, 0)),
                  pl.BlockSpec((1, d), lambda i, j: (0, 0)),
                  pl.BlockSpec((d, tn), lambda i, j: (0, j))],
        out_specs=pl.BlockSpec((tm, tn), lambda i, j: (i, j)),
        out_shape=jax.ShapeDtypeStruct((t, n), F32),
        scratch_shapes=[pltpu.VMEM((tm, d), BF16)],
        compiler_params=_cparams(2),
        name="norm_in_proj",
    )(x, g, w)


POOL_HIST = 16


def _pool_kernel(u_ref, w_ref, sc_ref, o_ref, buf_ref):
    s = pl.program_id(1)
    ts = u_ref.shape[0]
    gw = w_ref.shape[1]

    @pl.when(s == 0)
    def _():
        buf_ref[0:POOL_HIST, :] = jnp.zeros((POOL_HIST, buf_ref.shape[1]), F32)

    @pl.when(s > 0)
    def _():
        buf_ref[0:POOL_HIST, :] = buf_ref[ts:ts + POOL_HIST, :]

    buf_ref[POOL_HIST:POOL_HIST + ts, :] = u_ref[...]
    pos = s * ts + lax.broadcasted_iota(jnp.int32, (ts, 1), 0)
    for g, win in enumerate(POOL_WINDOWS):
        cols = slice(g * gw, (g + 1) * gw)
        acc = buf_ref[POOL_HIST:POOL_HIST + ts, cols]
        cur = acc
        for k in range(1, win):
            acc = acc + buf_ref[POOL_HIST - k:POOL_HIST - k + ts, cols]
        cnt = jnp.minimum(pos + 1, win).astype(F32)
        p = acc / cnt - cur
        y = jnp.dot(p.astype(BF16), w_ref[g], preferred_element_type=F32)
        o_ref[:, cols] = y * sc_ref[:, cols]


def _pool_mixer(proj, col_block, w, scale, bsz, seq, ts):
    mixw = scale.shape[-1]
    nblk = seq // ts
    return pl.pallas_call(
        _pool_kernel,
        grid=(bsz, nblk),
        in_specs=[pl.BlockSpec((ts, mixw), lambda b, s: (b * nblk + s, col_block)),
                  pl.BlockSpec(w.shape, lambda b, s: (0, 0, 0)),
                  pl.BlockSpec((1, mixw), lambda b, s: (0, 0))],
        out_specs=pl.BlockSpec((ts, mixw), lambda b, s: (b * nblk + s, 0)),
        out_shape=jax.ShapeDtypeStruct((bsz * seq, mixw), F32),
        scratch_shapes=[pltpu.VMEM((POOL_HIST + ts, mixw), F32)],
        compiler_params=_cparams(2),
        name="pool_mixer",
    )(proj, w, scale)


CONV_HIST = 32


def _conformer_kernel(u_ref, dw_ref, db_ref, lg_ref, lb_ref, o_ref, buf_ref):
    s = pl.program_id(1)
    ts = u_ref.shape[0]
    c = o_ref.shape[1]
    width = dw_ref.shape[0]

    @pl.when(s == 0)
    def _():
        buf_ref[0:CONV_HIST, :] = jnp.zeros((CONV_HIST, c), F32)

    @pl.when(s > 0)
    def _():
        buf_ref[0:CONV_HIST, :] = buf_ref[ts:ts + CONV_HIST, :]

    a = u_ref[:, 0:c]
    gt = u_ref[:, c:2 * c]
    buf_ref[CONV_HIST:CONV_HIST + ts, :] = a * _sigmoid(gt)
    rc = min(ts, LANES)
    for r0 in range(0, ts, rc):
        for c0 in range(0, c, LANES):
            acc = jnp.zeros((rc, LANES), F32) + db_ref[:, c0:c0 + LANES]
            for k in range(width):
                off = CONV_HIST - (width - 1) + k + r0
                acc = acc + buf_ref[off:off + rc, c0:c0 + LANES] * dw_ref[k:k + 1, c0:c0 + LANES]
            o_ref[r0:r0 + rc, c0:c0 + LANES] = acc
    acc = o_ref[...]
    mu = jnp.mean(acc, axis=-1, keepdims=True)
    xc = acc - mu
    y = xc * lax.rsqrt(jnp.mean(xc * xc, axis=-1, keepdims=True) + NORM_EPS)
    y = y * lg_ref[...] + lb_ref[...]
    o_ref[...] = y * _sigmoid(y)


def _conformer(proj, col_block, dw, db, lg, lb, bsz, seq, ts):
    c = dw.shape[1]
    nblk = seq // ts
    return pl.pallas_call(
        _conformer_kernel,
        grid=(bsz, nblk),
        in_specs=[pl.BlockSpec((ts, 2 * c), lambda b, s: (b * nblk + s, col_block)),
                  pl.BlockSpec(dw.shape, lambda b, s: (0, 0)),
                  pl.BlockSpec((1, c), lambda b, s: (0, 0)),
                  pl.BlockSpec((1, c), lambda b, s: (0, 0)),
                  pl.BlockSpec((1, c), lambda b, s: (0, 0))],
        out_specs=pl.BlockSpec((ts, c), lambda b, s: (b * nblk + s, 0)),
        out_shape=jax.ShapeDtypeStruct((bsz * seq, c), F32),
        scratch_shapes=[pltpu.VMEM((CONV_HIST + ts, c), F32)],
        compiler_params=_cparams(2),
        name="conformer_conv",
    )(proj, dw, db, lg, lb)


SSM_HIST = 8


def _ssd_kernel(z_ref, xs_ref, bc_ref, dt_ref, cw_ref, cb_ref, dtb_ref, a_ref, dsk_ref, ng_ref,
                o_ref, xbuf_ref, bcbuf_ref, state_ref):
    c_idx = pl.program_id(1)
    L = xs_ref.shape[0]
    mixw = xs_ref.shape[1]
    bcw = bc_ref.shape[1]
    width = cw_ref.shape[0]
    hp = SSM_HEAD_DIM
    ns = SSM_STATE
    rep = SSM_HEADS // SSM_GROUPS

    @pl.when(c_idx == 0)
    def _():
        xbuf_ref[0:SSM_HIST, :] = jnp.zeros((SSM_HIST, mixw), F32)
        bcbuf_ref[0:SSM_HIST, :] = jnp.zeros((SSM_HIST, bcw), F32)
        state_ref[...] = jnp.zeros(state_ref.shape, F32)

    @pl.when(c_idx > 0)
    def _():
        xbuf_ref[0:SSM_HIST, :] = xbuf_ref[L:L + SSM_HIST, :]
        bcbuf_ref[0:SSM_HIST, :] = bcbuf_ref[L:L + SSM_HIST, :]

    xbuf_ref[SSM_HIST:SSM_HIST + L, :] = xs_ref[...]
    bcbuf_ref[SSM_HIST:SSM_HIST + L, :] = bc_ref[...]

    xc = jnp.zeros((L, mixw), F32) + cb_ref[:, 0:mixw]
    bcc = jnp.zeros((L, bcw), F32) + cb_ref[:, mixw:mixw + bcw]
    for k in range(width):
        off = SSM_HIST - (width - 1) + k
        xc = xc + xbuf_ref[off:off + L, :] * cw_ref[k:k + 1, 0:mixw]
        bcc = bcc + bcbuf_ref[off:off + L, :] * cw_ref[k:k + 1, mixw:mixw + bcw]
    xc = xc * _sigmoid(xc)
    bcc = bcc * _sigmoid(bcc)

    dtr = dt_ref[...] + dtb_ref[...]
    dt = jnp.maximum(dtr, 0.0) + jnp.log(1.0 + jnp.exp(-jnp.abs(dtr)))
    da = dt * a_ref[...]
    row_i = lax.broadcasted_iota(jnp.int32, (L, L), 0)
    col_i = lax.broadcasted_iota(jnp.int32, (L, L), 1)
    tri = col_i <= row_i
    a_cum = jnp.dot(tri.astype(F32), da, preferred_element_type=F32,
                    precision=lax.Precision.HIGHEST)
    a_cum_t = a_cum.T
    a_last = a_cum[L - 1:L, :]
    dec_last = jnp.exp(a_last)
    dec_out = jnp.exp(a_cum)
    dec_st = jnp.exp(a_last - a_cum)

    cbs = []
    for g in range(SSM_GROUPS):
        bg = bcc[:, g * ns:(g + 1) * ns].astype(BF16)
        cg = bcc[:, SSM_GROUPS * ns + g * ns:SSM_GROUPS * ns + (g + 1) * ns].astype(BF16)
        cbs.append(lax.dot_general(cg, bg, (((1,), (1,)), ((), ())), preferred_element_type=F32))

    for h in range(SSM_HEADS):
        g = h // rep
        x_h = xc[:, h * hp:(h + 1) * hp]
        b_g = bcc[:, g * ns:(g + 1) * ns]
        c_g = bcc[:, SSM_GROUPS * ns + g * ns:SSM_GROUPS * ns + (g + 1) * ns]
        xd = x_h * dt[:, h:h + 1]
        seg = a_cum[:, h:h + 1] - a_cum_t[h:h + 1, :]
        lmat = jnp.exp(jnp.where(tri, seg, -jnp.inf))
        y_diag = jnp.dot((cbs[g] * lmat).astype(BF16), xd.astype(BF16), preferred_element_type=F32)
        st = state_ref[h]
        y_off = jnp.dot(c_g.astype(BF16), st.astype(BF16), preferred_element_type=F32) * dec_out[:, h:h + 1]
        bd = (b_g * dec_st[:, h:h + 1]).astype(BF16)
        st_new = lax.dot_general(bd, xd.astype(BF16), (((0,), (0,)), ((), ())),
                                 preferred_element_type=F32)
        state_ref[h] = st * dec_last[:, h:h + 1] + st_new
        y_h = y_diag + y_off + dsk_ref[:, h:h + 1] * x_h
        zz = z_ref[:, h * hp:(h + 1) * hp]
        o_ref[:, h * hp:(h + 1) * hp] = y_h * (zz * _sigmoid(zz))

    y = o_ref[...]
    o_ref[...] = _rms_rows(y, ng_ref[...])


def _ssd(proj, z_block, xs_block, bc_block, dt_block, cw, cb, dtb, a_neg, dsk, ng, bsz, seq):
    mixw = ng.shape[-1]
    bcw = cw.shape[1] - mixw
    L = SSM_CHUNK
    nblk = seq // L
    row = lambda b, s: b * nblk + s
    return pl.pallas_call(
        _ssd_kernel,
        grid=(bsz, nblk),
        in_specs=[pl.BlockSpec((L, mixw), lambda b, s: (row(b, s), z_block)),
                  pl.BlockSpec((L, mixw), lambda b, s: (row(b, s), xs_block)),
                  pl.BlockSpec((L, bcw), lambda b, s: (row(b, s), bc_block)),
                  pl.BlockSpec((L, LANES), lambda b, s: (row(b, s), dt_block)),
                  pl.BlockSpec(cw.shape, lambda b, s: (0, 0)),
                  pl.BlockSpec(cb.shape, lambda b, s: (0, 0)),
                  pl.BlockSpec((1, LANES), lambda b, s: (0, 0)),
                  pl.BlockSpec((1, LANES), lambda b, s: (0, 0)),
                  pl.BlockSpec((1, LANES), lambda b, s: (0, 0)),
                  pl.BlockSpec((1, mixw), lambda b, s: (0, 0))],
        out_specs=pl.BlockSpec((L, mixw), lambda b, s: (row(b, s), 0)),
        out_shape=jax.ShapeDtypeStruct((bsz * seq, mixw), F32),
        scratch_shapes=[pltpu.VMEM((SSM_HIST + L, mixw), F32),
                        pltpu.VMEM((SSM_HIST + L, bcw), F32),
                        pltpu.VMEM((SSM_HEADS, SSM_STATE, SSM_HEAD_DIM), F32)],
        compiler_params=_cparams(2),
        name="ssd_mixer",
    )(proj, proj, proj, proj, cw, cb, dtb, a_neg, dsk, ng)


def _rope_cols(x, c, s1, s2, half):
    return (x * c + pltpu.roll(x, LANES - half, axis=1) * s1 + pltpu.roll(x, half, axis=1) * s2)


def _rope_prep_kernel(q_ref, kv_ref, idx_ref, ta_ref, ti_ref, tt_ref,
                      qt_ref, k_ref, vt_ref, qit_ref, ki_ref, wt_ref):
    hd = HEAD_DIM
    scale = hd ** -0.5
    ca, s1a, s2a = ta_ref[0], ta_ref[1], ta_ref[2]
    for c in range(q_ref.shape[1] // LANES):
        qc = _rope_cols(q_ref[:, c * LANES:(c + 1) * LANES], ca, s1a, s2a, ROPE_DIM // 2) * scale
        qt_ref[c * LANES:(c + 1) * LANES, :] = qc.T.astype(qt_ref.dtype)
    kw = ATT_KV_HEADS * hd
    kc = _rope_cols(kv_ref[:, 0:kw], ca, s1a, s2a, ROPE_DIM // 2)
    for g in range(ATT_KV_HEADS):
        k_ref[g] = kc[:, g * hd:(g + 1) * hd].astype(k_ref.dtype)
    vt_ref[...] = kv_ref[:, kw:2 * kw].T.astype(vt_ref.dtype)
    ci, s1i, s2i = ti_ref[0], ti_ref[1], ti_ref[2]
    nq = IDX_HEADS * IDX_DIM
    for c in range(nq // LANES):
        qc = _rope_cols(idx_ref[:, c * LANES:(c + 1) * LANES], ci, s1i, s2i, IDX_ROPE_DIM // 2)
        qit_ref[c * LANES:(c + 1) * LANES, :] = qc.T.astype(qit_ref.dtype)
    tail = _rope_cols(idx_ref[:, nq:nq + LANES], tt_ref[0], tt_ref[1], tt_ref[2], IDX_ROPE_DIM // 2)
    ki_ref[...] = tail[:, 0:IDX_DIM].astype(ki_ref.dtype)
    wt_ref[...] = tail.T[IDX_DIM:IDX_DIM + IDX_HEADS, :]


def _rope_prep(proj, q_block, kv_block, idx_block, tab_att, tab_idx, tab_tail, bsz, seq, ts):
    nblk = seq // ts
    qw = ATT_HEADS * HEAD_DIM
    kw = ATT_KV_HEADS * HEAD_DIM
    idxw = 4 * LANES
    row = lambda b, s: b * nblk + s
    tab_spec = pl.BlockSpec((3, ts, LANES), lambda b, s: (0, s, 0))
    return pl.pallas_call(
        _rope_prep_kernel,
        grid=(bsz, nblk),
        in_specs=[pl.BlockSpec((ts, qw), lambda b, s: (row(b, s), q_block)),
                  pl.BlockSpec((ts, 2 * kw), lambda b, s: (row(b, s), kv_block)),
                  pl.BlockSpec((ts, idxw), lambda b, s: (row(b, s), idx_block)),
                  tab_spec, tab_spec, tab_spec],
        out_specs=[pl.BlockSpec((None, qw, ts), lambda b, s: (b, 0, s)),
                   pl.BlockSpec((None, ATT_KV_HEADS, ts, HEAD_DIM), lambda b, s: (b, 0, s, 0)),
                   pl.BlockSpec((None, kw, ts), lambda b, s: (b, 0, s)),
                   pl.BlockSpec((None, IDX_HEADS * IDX_DIM, ts), lambda b, s: (b, 0, s)),
                   pl.BlockSpec((None, ts, IDX_DIM), lambda b, s: (b, s, 0)),
                   pl.BlockSpec((None, IDX_HEADS, ts), lambda b, s: (b, 0, s))],
        out_shape=[jax.ShapeDtypeStruct((bsz, qw, seq), BF16),
                   jax.ShapeDtypeStruct((bsz, ATT_KV_HEADS, seq, HEAD_DIM), BF16),
                   jax.ShapeDtypeStruct((bsz, kw, seq), BF16),
                   jax.ShapeDtypeStruct((bsz, IDX_HEADS * IDX_DIM, seq), BF16),
                   jax.ShapeDtypeStruct((bsz, seq, IDX_DIM), BF16),
                   jax.ShapeDtypeStruct((bsz, IDX_HEADS, seq), F32)],
        compiler_params=_cparams(2),
        name="rope_prep",
    )(proj, proj, proj, tab_att, tab_idx, tab_tail)


def _rope_tables(seq, head_dim, rot_dim, n_rot_heads):
    half = rot_dim // 2
    inv_freq = jnp.power(jnp.float32(ROPE_THETA), -jnp.arange(half, dtype=F32) * (2.0 / rot_dim))
    ang = jnp.arange(seq, dtype=F32)[:, None] * inv_freq[None, :]
    cos, sin = jnp.cos(ang), jnp.sin(ang)
    lane = np.arange(LANES)
    j = lane % head_dim
    rot_head = lane < n_rot_heads * head_dim
    first = rot_head & (j < half)
    second = rot_head & (j >= half) & (j < rot_dim)
    fi = np.where(first, j, 0)
    si = np.where(second, j - half, 0)
    c = jnp.where(first[None, :], cos[:, fi], jnp.where(second[None, :], cos[:, si], 1.0))
    s1 = jnp.where(first[None, :], -sin[:, fi], 0.0)
    s2 = jnp.where(second[None, :], sin[:, si], 0.0)
    return jnp.stack([c, s1, s2]).astype(F32)


def _sortable_key(score):
    score = jnp.where(score == 0.0, 0.0, score)
    bits = pltpu.bitcast(score, jnp.int32)
    return bits ^ (lax.shift_right_arithmetic(bits, 31) & jnp.int32(0x7FFFFFFF))


def _dsa_kernel(qt_ref, qit_ref, wt_ref, k_ref, ki_ref, vt_ref, o_ref, keys_ref, ot_ref, *, top_k, seq_bits):
    j = pl.program_id(1)
    qb = LANES
    hd = HEAD_DIM
    rep = ATT_HEADS // ATT_KV_HEADS
    sub_i = lax.broadcasted_iota(jnp.int32, (qb, qb), 0)
    lane_i = lax.broadcasted_iota(jnp.int32, (qb, qb), 1)

    def chunk_keys(c):
        start = pl.multiple_of(c * qb, qb)
        kic = ki_ref[pl.ds(start, qb), :]
        score = jnp.zeros((qb, qb), F32)
        for h in range(IDX_HEADS):
            s_h = jnp.dot(kic, qit_ref[h * IDX_DIM:(h + 1) * IDX_DIM, :], preferred_element_type=F32)
            score = score + jnp.maximum(s_h, 0.0) * wt_ref[h:h + 1, :]
        return start, _sortable_key(score)

    def score_body(c, carry):
        start, key = chunk_keys(c)
        keys_ref[pl.ds(start, qb), :] = key
        return carry

    lax.fori_loop(0, j, score_body, 0)
    start_d, key_d = chunk_keys(j)
    keys_ref[pl.ds(start_d, qb), :] = jnp.where(sub_i <= lane_i, key_d, jnp.int32(INT_MIN))
    n_chunks = j + 1

    def count_where(pred_fn):
        def body(c, acc):
            start = pl.multiple_of(c * qb, qb)
            hit = pred_fn(keys_ref[pl.ds(start, qb), :], start)
            return acc + jnp.where(hit, 1.0, 0.0)
        acc = lax.fori_loop(0, n_chunks, body, jnp.zeros((qb, qb), F32))
        return jnp.sum(acc, axis=0, keepdims=True)

    def bit_body(i, tau):
        cand = tau ^ lax.shift_left(jnp.int32(1), 31 - i)
        cnt = count_where(lambda kk, _: kk >= cand)
        return jnp.where(cnt >= top_k, cand, tau)

    tau = lax.fori_loop(0, 32, bit_body, jnp.full((1, qb), INT_MIN, jnp.int32))
    tau = jnp.maximum(tau, jnp.int32(INT_MIN + 1))

    n_gt = count_where(lambda kk, _: kk > tau)
    n_eq = count_where(lambda kk, _: kk == tau)
    need = top_k - n_gt
    has_excess = jnp.max(jnp.where(n_eq > need, 1.0, 0.0)) > 0.5

    @pl.when(has_excess)
    def _():
        def pos_body(i, xcut):
            cand = xcut | lax.shift_left(jnp.int32(1), seq_bits - 1 - i)
            cnt = count_where(lambda kk, st: (kk == tau) & ((st + sub_i) < cand))
            return jnp.where(cnt < need, cand, xcut)

        xcut = lax.fori_loop(0, seq_bits, pos_body, jnp.zeros((1, qb), jnp.int32))
        def fix_body(c, carry):
            start = pl.multiple_of(c * qb, qb)
            kk = keys_ref[pl.ds(start, qb), :]
            drop = (kk == tau) & (((start + sub_i) > xcut) | (need < 1))
            keys_ref[pl.ds(start, qb), :] = jnp.where(drop, jnp.int32(INT_MIN), kk)
            return carry

        lax.fori_loop(0, n_chunks, fix_body, 0)

    for h in range(ATT_HEADS):
        g = h // rep
        q_h = qt_ref[h * hd:(h + 1) * hd, :]

        def att_body(c, carry):
            m, ssum, acc = carry
            start = pl.multiple_of(c * qb, qb)
            sel = keys_ref[pl.ds(start, qb), :] >= tau
            logit = jnp.dot(k_ref[g, pl.ds(start, qb), :], q_h, preferred_element_type=F32)
            logit = jnp.where(sel, logit, NEG_BIG)
            m_new = jnp.maximum(m, jnp.max(logit, axis=0, keepdims=True))
            p = jnp.where(sel, jnp.exp(logit - m_new), 0.0)
            alpha = jnp.exp(m - m_new)
            ssum = ssum * alpha + jnp.sum(p, axis=0, keepdims=True)
            pv = jnp.dot(vt_ref[g * hd:(g + 1) * hd, pl.ds(start, qb)], p.astype(BF16),
                         preferred_element_type=F32)
            return m_new, ssum, acc * alpha + pv

        init = (jnp.full((1, qb), NEG_BIG, F32), jnp.zeros((1, qb), F32), jnp.zeros((hd, qb), F32))
        _, ssum, acc = lax.fori_loop(0, n_chunks, att_body, init)
        ot_ref[h * hd:(h + 1) * hd, :] = acc / ssum
    for c0 in range(0, ATT_HEADS * hd, LANES):
        o_ref[:, c0:c0 + LANES] = ot_ref[c0:c0 + LANES, :].T


def _dsa_attention(qt, qit, wt, k, ki, vt, bsz, seq):
    qb = LANES
    nblk = seq // qb
    qw = ATT_HEADS * HEAD_DIM
    kw = ATT_KV_HEADS * HEAD_DIM
    top_k = min(TOPK_MAX, seq // 4)
    seq_bits = int(np.log2(seq))
    assert 2 ** seq_bits == seq
    kern = functools.partial(_dsa_kernel, top_k=top_k, seq_bits=seq_bits)
    return pl.pallas_call(
        kern,
        grid=(bsz, nblk),
        in_specs=[pl.BlockSpec((None, qw, qb), lambda b, j: (b, 0, j)),
                  pl.BlockSpec((None, IDX_HEADS * IDX_DIM, qb), lambda b, j: (b, 0, j)),
                  pl.BlockSpec((None, IDX_HEADS, qb), lambda b, j: (b, 0, j)),
                  pl.BlockSpec((None, ATT_KV_HEADS, seq, HEAD_DIM), lambda b, j: (b, 0, 0, 0)),
                  pl.BlockSpec((None, seq, IDX_DIM), lambda b, j: (b, 0, 0)),
                  pl.BlockSpec((None, kw, seq), lambda b, j: (b, 0, 0))],
        out_specs=pl.BlockSpec((qb, qw), lambda b, j: (b * nblk + j, 0)),
        out_shape=jax.ShapeDtypeStruct((bsz * seq, qw), F32),
        scratch_shapes=[pltpu.VMEM((seq, qb), jnp.int32), pltpu.VMEM((qw, qb), F32)],
        compiler_params=_cparams(2),
        name="dsa_attention",
    )(qt, qit, wt, k, ki, vt)


def _merge_kernel(x_ref, ya_ref, yb_ref, yc_ref, yd_ref, gl_ref, wbr_ref, wout_ref, o_ref):
    d = x_ref.shape[1]
    merged = None
    for n, y_ref in enumerate((ya_ref, yb_ref, yc_ref, yd_ref)):
        u = jnp.dot(y_ref[...].astype(BF16), wbr_ref[n], preferred_element_type=F32)
        term = _sigmoid(gl_ref[:, n * d:(n + 1) * d]) * u
        merged = term if merged is None else merged + term
    o_ref[...] = x_ref[...] + jnp.dot(merged.astype(BF16), wout_ref[...], preferred_element_type=F32)


def _merge(x, ya, yb, yc, yd, proj, gate_block, wbr, wout, tm):
    t, d = x.shape
    mixw = ya.shape[1]
    nbr = wbr.shape[0]
    y_spec = pl.BlockSpec((tm, mixw), lambda i: (i, 0))
    return pl.pallas_call(
        _merge_kernel,
        grid=(t // tm,),
        in_specs=[pl.BlockSpec((tm, d), lambda i: (i, 0)), y_spec, y_spec, y_spec, y_spec,
                  pl.BlockSpec((tm, nbr * d), lambda i: (i, gate_block)),
                  pl.BlockSpec(wbr.shape, lambda i: (0, 0, 0)),
                  pl.BlockSpec(wout.shape, lambda i: (0, 0))],
        out_specs=pl.BlockSpec((tm, d), lambda i: (i, 0)),
        out_shape=jax.ShapeDtypeStruct((t, d), F32),
        compiler_params=_cparams(1),
        name="merge_out_proj",
    )(x, ya, yb, yc, yd, proj, wbr, wout)


def _ffn_kernel(x_ref, g_ref, wg_ref, wu_ref, wd_ref, fin_ref, o_ref, h_ref, acc_ref, *, final_norm):
    f = pl.program_id(1)

    @pl.when(f == 0)
    def _():
        h_ref[...] = _rms_rows(x_ref[...], g_ref[...]).astype(h_ref.dtype)
        acc_ref[...] = jnp.zeros(acc_ref.shape, F32)

    h = h_ref[...]
    a = jnp.dot(h, wg_ref[...], preferred_element_type=F32)
    b = jnp.dot(h, wu_ref[...], preferred_element_type=F32)
    t = (a * _sigmoid(a)) * b
    acc_ref[...] += jnp.dot(t.astype(BF16), wd_ref[...], preferred_element_type=F32)

    @pl.when(f == pl.num_programs(1) - 1)
    def _():
        y = x_ref[...] + acc_ref[...]
        o_ref[...] = _rms_rows(y, fin_ref[...]) if final_norm else y


def _ffn(x, g, wg, wu, wd, fin, final_norm, tm, tf):
    t, d = x.shape
    ff = wg.shape[1]
    return pl.pallas_call(
        functools.partial(_ffn_kernel, final_norm=final_norm),
        grid=(t // tm, ff // tf),
        in_specs=[pl.BlockSpec((tm, d), lambda i, f: (i, 0)),
                  pl.BlockSpec((1, d), lambda i, f: (0, 0)),
                  pl.BlockSpec((d, tf), lambda i, f: (0, f)),
                  pl.BlockSpec((d, tf), lambda i, f: (0, f)),
                  pl.BlockSpec((tf, d), lambda i, f: (f, 0)),
                  pl.BlockSpec((1, d), lambda i, f: (0, 0))],
        out_specs=pl.BlockSpec((tm, d), lambda i, f: (i, 0)),
        out_shape=jax.ShapeDtypeStruct((t, d), F32),
        scratch_shapes=[pltpu.VMEM((tm, d), BF16), pltpu.VMEM((tm, d), F32)],
        compiler_params=_cparams(2),
        name="dense_swiglu",
    )(x, g, wg, wu, wd, fin)


def _moe_kernel(x_ref, g_ref, r_ref, wg_ref, wu_ref, wd_ref, fin_ref, o_ref, h_ref, comb_ref, acc_ref,
                *, n_experts, final_norm):
    e = pl.program_id(1)
    f = pl.program_id(2)
    tm = x_ref.shape[0]
    lane = lax.broadcasted_iota(jnp.int32, (tm, LANES), 1)

    @pl.when((e == 0) & (f == 0))
    def _():
        hf = _rms_rows(x_ref[...], g_ref[...])
        h_ref[...] = hf.astype(h_ref.dtype)
        acc_ref[...] = jnp.zeros(acc_ref.shape, F32)
        logits = jnp.dot(hf, r_ref[...], preferred_element_type=F32, precision=lax.Precision.HIGHEST)
        logits = jnp.where(lane < n_experts, logits, -jnp.inf)
        m1 = jnp.max(logits, axis=-1, keepdims=True)
        i1 = jnp.min(jnp.where(logits == m1, lane, LANES), axis=-1, keepdims=True)
        rest = jnp.where(lane == i1, -jnp.inf, logits)
        m2 = jnp.max(rest, axis=-1, keepdims=True)
        i2 = jnp.min(jnp.where(rest == m2, lane, LANES), axis=-1, keepdims=True)
        e2 = jnp.exp(m2 - m1)
        p1 = 1.0 / (1.0 + e2)
        p2 = e2 / (1.0 + e2)
        comb_ref[...] = jnp.where(lane == i1, p1, 0.0) + jnp.where(lane == i2, p2, 0.0)

    c_e = jnp.sum(jnp.where(lane == e, comb_ref[...], 0.0), axis=-1, keepdims=True)
    h = h_ref[...]
    a = jnp.dot(h, wg_ref[...], preferred_element_type=F32)
    b = jnp.dot(h, wu_ref[...], preferred_element_type=F32)
    t = (a * _sigmoid(a)) * b
    acc_ref[...] += c_e * jnp.dot(t.astype(BF16), wd_ref[...], preferred_element_type=F32)

    @pl.when((e == pl.num_programs(1) - 1) & (f == pl.num_programs(2) - 1))
    def _():
        y = x_ref[...] + acc_ref[...]
        o_ref[...] = _rms_rows(y, fin_ref[...]) if final_norm else y


def _moe(x, g, router, wg, wu, wd, fin, final_norm, tm, tf):
    t, d = x.shape
    n_experts, _, ff = wg.shape
    return pl.pallas_call(
        functools.partial(_moe_kernel, n_experts=n_experts, final_norm=final_norm),
        grid=(t // tm, n_experts, ff // tf),
        in_specs=[pl.BlockSpec((tm, d), lambda i, e, f: (i, 0)),
                  pl.BlockSpec((1, d), lambda i, e, f: (0, 0)),
                  pl.BlockSpec((d, LANES), lambda i, e, f: (0, 0)),
                  pl.BlockSpec((None, d, tf), lambda i, e, f: (e, 0, f)),
                  pl.BlockSpec((None, d, tf), lambda i, e, f: (e, 0, f)),
                  pl.BlockSpec((None, tf, d), lambda i, e, f: (e, f, 0)),
                  pl.BlockSpec((1, d), lambda i, e, f: (0, 0))],
        out_specs=pl.BlockSpec((tm, d), lambda i, e, f: (i, 0)),
        out_shape=jax.ShapeDtypeStruct((t, d), F32),
        scratch_shapes=[pltpu.VMEM((tm, d), BF16), pltpu.VMEM((tm, LANES), F32), pltpu.VMEM((tm, d), F32)],
        compiler_params=_cparams(3),
        name="moe_swiglu",
    )(x, g, router, wg, wu, wd, fin)


def _pack_w_in(w_in, d_model, mixw):
    qw = ATT_HEADS * HEAD_DIM
    kvw = 2 * ATT_KV_HEADS * HEAD_DIM
    qiw = IDX_HEADS * IDX_DIM
    bcw = 2 * SSM_GROUPS * SSM_STATE
    sizes = (mixw, qw, kvw, qiw, IDX_DIM, IDX_HEADS, 2 * mixw, mixw, mixw + bcw, SSM_HEADS, 4 * d_model)
    offs = np.concatenate([[0], np.cumsum(sizes)])
    assert offs[-1] == w_in.shape[1]
    seg = lambda i: w_in[:, offs[i]:offs[i + 1]]
    u_pool, q, kv, qi, ki, wi, u_conv, z, xbc, dt, gates = (seg(i) for i in range(len(sizes)))
    d = w_in.shape[0]
    zeros = lambda n: jnp.zeros((d, n), w_in.dtype)
    idx_blk = jnp.concatenate([qi, ki, wi, zeros(3 * LANES - qiw - IDX_DIM - IDX_HEADS),
                               dt, zeros(LANES - SSM_HEADS)], axis=1)
    packed = jnp.concatenate([gates, u_conv, u_pool, q, z, xbc[:, :mixw], idx_blk, kv, xbc[:, mixw:]], axis=1)
    widths = dict(gates=4 * d_model, conv=2 * mixw, pool=mixw, q=qw, z=mixw, xs=mixw, idx=4 * LANES,
                  kv=kvw, bc=bcw)
    blocks, off = {}, 0
    for name in ("gates", "conv", "pool", "q", "z", "xs", "idx", "kv", "bc"):
        assert off % widths[name] == 0
        blocks[name] = off // widths[name]
        off += widths[name]
    blocks["dt"] = (blocks["idx"] * 4 * LANES + 3 * LANES) // LANES
    return packed.astype(BF16), blocks


def _pad_lanes(v):
    return jnp.zeros((1, LANES), F32).at[0, :v.shape[0]].set(v.astype(F32))


def _pick_tile(n, target):
    t = min(n, target)
    while n % t:
        t //= 2
    return t


def _pick_lane_tile(n, target):
    best = None
    for m in range(LANES, min(n, target) + 1, LANES):
        if n % m == 0:
            best = m
    return n if best is None else best


def kernel(x, norm_mix, w_in, pool_w, pool_scale, conv_dw, conv_b, conv_ln_g, conv_ln_b, ssm_conv_w, ssm_conv_b, ssm_dt_bias, ssm_a_log, ssm_d, ssm_norm, w_br, w_out, norm_ffn, ffn_w_gate, ffn_w_up, ffn_w_down, moe_router, moe_w_gate, moe_w_up, moe_w_down, final_norm):
    bsz, seq, d = x.shape
    depth = norm_mix.shape[0]
    mixw = pool_scale.shape[-1]
    t = bsz * seq
    row = lambda v: v.reshape(1, -1).astype(F32)

    tab_att = _rope_tables(seq, HEAD_DIM, ROPE_DIM, LANES // HEAD_DIM)
    tab_idx = _rope_tables(seq, IDX_DIM, IDX_ROPE_DIM, LANES // IDX_DIM)
    tab_tail = _rope_tables(seq, IDX_DIM, IDX_ROPE_DIM, 1)

    tm = _pick_tile(t, 1024)
    ts_seq = _pick_tile(seq, 512)
    fin = row(final_norm)

    xf = x.reshape(t, d)
    for layer in range(depth):
        w_packed, blk = _pack_w_in(w_in[layer], d, mixw)
        proj = _norm_matmul(xf, row(norm_mix[layer]), w_packed, tm, _pick_tile(w_packed.shape[1], 1024))

        y_a = _pool_mixer(proj, blk["pool"], pool_w[layer].astype(BF16), row(pool_scale[layer]), bsz, seq, ts_seq)
        qt, k_r, vt, qit, ki_r, wt = _rope_prep(proj, blk["q"], blk["kv"], blk["idx"], tab_att, tab_idx,
                                                tab_tail, bsz, seq, ts_seq)
        y_b = _dsa_attention(qt, qit, wt, k_r, ki_r, vt, bsz, seq)
        y_c = _conformer(proj, blk["conv"], conv_dw[layer], row(conv_b[layer]), row(conv_ln_g[layer]),
                         row(conv_ln_b[layer]), bsz, seq, _pick_tile(seq, 256))
        y_d = _ssd(proj, blk["z"], blk["xs"], blk["bc"], blk["dt"], ssm_conv_w[layer], row(ssm_conv_b[layer]),
                   _pad_lanes(ssm_dt_bias[layer]), _pad_lanes(-jnp.exp(ssm_a_log[layer].astype(F32))),
                   _pad_lanes(ssm_d[layer]), row(ssm_norm[layer]), bsz, seq)
        xf = _merge(xf, y_a, y_b, y_c, y_d, proj, blk["gates"], w_br[layer].astype(BF16),
                    w_out[layer].astype(BF16), _pick_tile(t, 512))

        last = layer == depth - 1
        jj = layer // 2
        if layer % 2 == 0:
            ff = ffn_w_gate.shape[-1]
            xf = _ffn(xf, row(norm_ffn[layer]), ffn_w_gate[jj].astype(BF16), ffn_w_up[jj].astype(BF16),
                      ffn_w_down[jj].astype(BF16), fin, last, _pick_tile(t, 512), _pick_lane_tile(ff, 1408))
        else:
            ff = moe_w_gate.shape[-1]
            n_exp = moe_router.shape[-1]
            router = jnp.zeros((d, LANES), F32).at[:, :n_exp].set(moe_router[jj].astype(F32))
            xf = _moe(xf, row(norm_ffn[layer]), router, moe_w_gate[jj].astype(BF16), moe_w_up[jj].astype(BF16),
                      moe_w_down[jj].astype(BF16), fin, last, tm, _pick_lane_tile(ff, 896))
    if depth == 0:
        raise ValueError("depth must be positive")
    return xf.reshape(bsz, seq, d)
```

```python
import functools

import numpy as np
import jax
import jax.numpy as jnp
from jax import lax
from jax.experimental import pallas as pl
from jax.experimental.pallas import tpu as pltpu

POOL_WINDOWS = (2, 4, 8, 16)
ATT_HEADS = 8
ATT_KV_HEADS = 2
HEAD_DIM = 64
ROPE_DIM = HEAD_DIM // 4
ROPE_THETA = 500000.0
IDX_HEADS = 8
IDX_DIM = 32
IDX_ROPE_DIM = IDX_DIM // 4
TOPK_MAX = 256
SSM_HEADS = 8
SSM_HEAD_DIM = 64
SSM_GROUPS = 2
SSM_STATE = 64
SSM_CHUNK = 128
TOP_K = 2
NORM_EPS = 1e-6

LANES = 128
SUBLANES = 8
VMEM_LIMIT_BYTES = 56 * 1024 * 1024

F32 = jnp.float32
BF16 = jnp.bfloat16
INT_MIN = -(2 ** 31)
NEG_BIG = -1e30


def _cparams(n_axes):
    return pltpu.CompilerParams(dimension_semantics=("arbitrary",) * n_axes,
                                vmem_limit_bytes=VMEM_LIMIT_BYTES)


def _sigmoid(x):
    return 1.0 / (1.0 + jnp.exp(-x))


def _rms_rows(x, g):
    return x * lax.rsqrt(jnp.mean(x * x, axis=-1, keepdims=True) + NORM_EPS) * g


def _norm_matmul_kernel(x_ref, g_ref, w_ref, o_ref, h_ref):
    @pl.when(pl.program_id(1) == 0)
    def _():
        h_ref[...] = _rms_rows(x_ref[...], g_ref[...]).astype(h_ref.dtype)

    o_ref[...] = jnp.dot(h_ref[...], w_ref[...], preferred_element_type=F32)


def _norm_matmul(x, g, w, tm, tn):
    t, d = x.shape
    n = w.shape[1]
    return pl.pallas_call(
        _norm_matmul_kernel,
        grid=(t // tm, n // tn),
        in_specs=[pl.BlockSpec((tm, d), lambda i, j: (i, 0)),
                  pl.BlockSpec((1, d), lambda i, j: (0, 0)),
                  pl.BlockSpec((d, tn), lambda i, j: (0, j))],
        out_specs=pl.BlockSpec((tm, tn), lambda i, j: (i, j)),
        out_shape=jax.ShapeDtypeStruct((t, n), F32),
        scratch_shapes=[pltpu.VMEM((tm, d), BF16)],
        compiler_params=_cparams(2),
        name="norm_in_proj",
    )(x, g, w)


POOL_HIST = 16


def _pool_kernel(u_ref, w_ref, sc_ref, o_ref, buf_ref):
    s = pl.program_id(1)
    ts = u_ref.shape[0]
    gw = w_ref.shape[1]

    @pl.when(s == 0)
    def _():
        buf_ref[0:POOL_HIST, :] = jnp.zeros((POOL_HIST, buf_ref.shape[1]), F32)

    @pl.when(s > 0)
    def _():
        buf_ref[0:POOL_HIST, :] = buf_ref[ts:ts + POOL_HIST, :]

    buf_ref[POOL_HIST:POOL_HIST + ts, :] = u_ref[...]
    pos = s * ts + lax.broadcasted_iota(jnp.int32, (ts, 1), 0)
    for g, win in enumerate(POOL_WINDOWS):
        cols = slice(g * gw, (g + 1) * gw)
        acc = buf_ref[POOL_HIST:POOL_HIST + ts, cols]
        cur = acc
        for k in range(1, win):
            acc = acc + buf_ref[POOL_HIST - k:POOL_HIST - k + ts, cols]
        cnt = jnp.minimum(pos + 1, win).astype(F32)
        p = acc / cnt - cur
        y = jnp.dot(p.astype(BF16), w_ref[g], preferred_element_type=F32)
        o_ref[:, cols] = y * sc_ref[:, cols]


def _pool_mixer(proj, col_block, w, scale, bsz, seq, ts):
    mixw = scale.shape[-1]
    nblk = seq // ts
    return pl.pallas_call(
        _pool_kernel,
        grid=(bsz, nblk),
        in_specs=[pl.BlockSpec((ts, mixw), lambda b, s: (b * nblk + s, col_block)),
                  pl.BlockSpec(w.shape, lambda b, s: (0, 0, 0)),
                  pl.BlockSpec((1, mixw), lambda b, s: (0, 0))],
        out_specs=pl.BlockSpec((ts, mixw), lambda b, s: (b * nblk + s, 0)),
        out_shape=jax.ShapeDtypeStruct((bsz * seq, mixw), F32),
        scratch_shapes=[pltpu.VMEM((POOL_HIST + ts, mixw), F32)],
        compiler_params=_cparams(2),
        name="pool_mixer",
    )(proj, w, scale)


CONV_HIST = 32


def _conformer_kernel(u_ref, dw_ref, db_ref, lg_ref, lb_ref, o_ref, buf_ref):
    s = pl.program_id(1)
    ts = u_ref.shape[0]
    c = o_ref.shape[1]
    width = dw_ref.shape[0]

    @pl.when(s == 0)
    def _():
        buf_ref[0:CONV_HIST, :] = jnp.zeros((CONV_HIST, c), F32)

    @pl.when(s > 0)
    def _():
        buf_ref[0:CONV_HIST, :] = buf_ref[ts:ts + CONV_HIST, :]

    a = u_ref[:, 0:c]
    gt = u_ref[:, c:2 * c]
    buf_ref[CONV_HIST:CONV_HIST + ts, :] = a * _sigmoid(gt)
    rc = min(ts, LANES)
    for r0 in range(0, ts, rc):
        for c0 in range(0, c, LANES):
            acc = jnp.zeros((rc, LANES), F32) + db_ref[:, c0:c0 + LANES]
            for k in range(width):
                off = CONV_HIST - (width - 1) + k + r0
                acc = acc + buf_ref[off:off + rc, c0:c0 + LANES] * dw_ref[k:k + 1, c0:c0 + LANES]
            o_ref[r0:r0 + rc, c0:c0 + LANES] = acc
    acc = o_ref[...]
    mu = jnp.mean(acc, axis=-1, keepdims=True)
    xc = acc - mu
    y = xc * lax.rsqrt(jnp.mean(xc * xc, axis=-1, keepdims=True) + NORM_EPS)
    y = y * lg_ref[...] + lb_ref[...]
    o_ref[...] = y * _sigmoid(y)


def _conformer(proj, col_block, dw, db, lg, lb, bsz, seq, ts):
    c = dw.shape[1]
    nblk = seq // ts
    return pl.pallas_call(
        _conformer_kernel,
        grid=(bsz, nblk),
        in_specs=[pl.BlockSpec((ts, 2 * c), lambda b, s: (b * nblk + s, col_block)),
                  pl.BlockSpec(dw.shape, lambda b, s: (0, 0)),
                  pl.BlockSpec((1, c), lambda b, s: (0, 0)),
                  pl.BlockSpec((1, c), lambda b, s: (0, 0)),
                  pl.BlockSpec((1, c), lambda b, s: (0, 0))],
        out_specs=pl.BlockSpec((ts, c), lambda b, s: (b * nblk + s, 0)),
        out_shape=jax.ShapeDtypeStruct((bsz * seq, c), F32),
        scratch_shapes=[pltpu.VMEM((CONV_HIST + ts, c), F32)],
        compiler_params=_cparams(2),
        name="conformer_conv",
    )(proj, dw, db, lg, lb)


SSM_HIST = 8


def _ssd_kernel(z_ref, xs_ref, bc_ref, dt_ref, cw_ref, cb_ref, dtb_ref, a_ref, dsk_ref, ng_ref,
                o_ref, xbuf_ref, bcbuf_ref, state_ref):
    c_idx = pl.program_id(1)
    L = xs_ref.shape[0]
    mixw = xs_ref.shape[1]
    bcw = bc_ref.shape[1]
    width = cw_ref.shape[0]
    hp = SSM_HEAD_DIM
    ns = SSM_STATE
    rep = SSM_HEADS // SSM_GROUPS

    @pl.when(c_idx == 0)
    def _():
        xbuf_ref[0:SSM_HIST, :] = jnp.zeros((SSM_HIST, mixw), F32)
        bcbuf_ref[0:SSM_HIST, :] = jnp.zeros((SSM_HIST, bcw), F32)
        state_ref[...] = jnp.zeros(state_ref.shape, F32)

    @pl.when(c_idx > 0)
    def _():
        xbuf_ref[0:SSM_HIST, :] = xbuf_ref[L:L + SSM_HIST, :]
        bcbuf_ref[0:SSM_HIST, :] = bcbuf_ref[L:L + SSM_HIST, :]

    xbuf_ref[SSM_HIST:SSM_HIST + L, :] = xs_ref[...]
    bcbuf_ref[SSM_HIST:SSM_HIST + L, :] = bc_ref[...]

    xc = jnp.zeros((L, mixw), F32) + cb_ref[:, 0:mixw]
    bcc = jnp.zeros((L, bcw), F32) + cb_ref[:, mixw:mixw + bcw]
    for k in range(width):
        off = SSM_HIST - (width - 1) + k
        xc = xc + xbuf_ref[off:off + L, :] * cw_ref[k:k + 1, 0:mixw]
        bcc = bcc + bcbuf_ref[off:off + L, :] * cw_ref[k:k + 1, mixw:mixw + bcw]
    xc = xc * _sigmoid(xc)
    bcc = bcc * _sigmoid(bcc)

    dtr = dt_ref[...] + dtb_ref[...]
    dt = jnp.maximum(dtr, 0.0) + jnp.log(1.0 + jnp.exp(-jnp.abs(dtr)))
    da = dt * a_ref[...]
    row_i = lax.broadcasted_iota(jnp.int32, (L, L), 0)
    col_i = lax.broadcasted_iota(jnp.int32, (L, L), 1)
    tri = col_i <= row_i
    a_cum = jnp.dot(tri.astype(F32), da, preferred_element_type=F32,
                    precision=lax.Precision.HIGHEST)
    a_cum_t = a_cum.T
    a_last = a_cum[L - 1:L, :]
    dec_last = jnp.exp(a_last)
    dec_out = jnp.exp(a_cum)
    dec_st = jnp.exp(a_last - a_cum)

    cbs = []
    for g in range(SSM_GROUPS):
        bg = bcc[:, g * ns:(g + 1) * ns].astype(BF16)
        cg = bcc[:, SSM_GROUPS * ns + g * ns:SSM_GROUPS * ns + (g + 1) * ns].astype(BF16)
        cbs.append(lax.dot_general(cg, bg, (((1,), (1,)), ((), ())), preferred_element_type=F32))

    for h in range(SSM_HEADS):
        g = h // rep
        x_h = xc[:, h * hp:(h + 1) * hp]
        b_g = bcc[:, g * ns:(g + 1) * ns]
        c_g = bcc[:, SSM_GROUPS * ns + g * ns:SSM_GROUPS * ns + (g + 1) * ns]
        xd = x_h * dt[:, h:h + 1]
        seg = a_cum[:, h:h + 1] - a_cum_t[h:h + 1, :]
        lmat = jnp.exp(jnp.where(tri, seg, -jnp.inf))
        y_diag = jnp.dot((cbs[g] * lmat).astype(BF16), xd.astype(BF16), preferred_element_type=F32)
        st = state_ref[h]
        y_off = jnp.dot(c_g.astype(BF16), st.astype(BF16), preferred_element_type=F32) * dec_out[:, h:h + 1]
        bd = (b_g * dec_st[:, h:h + 1]).astype(BF16)
        st_new = lax.dot_general(bd, xd.astype(BF16), (((0,), (0,)), ((), ())),
                                 preferred_element_type=F32)
        state_ref[h] = st * dec_last[:, h:h + 1] + st_new
        y_h = y_diag + y_off + dsk_ref[:, h:h + 1] * x_h
        zz = z_ref[:, h * hp:(h + 1) * hp]
        o_ref[:, h * hp:(h + 1) * hp] = y_h * (zz * _sigmoid(zz))

    y = o_ref[...]
    o_ref[...] = _rms_rows(y, ng_ref[...])


def _ssd(proj, z_block, xs_block, bc_block, dt_block, cw, cb, dtb, a_neg, dsk, ng, bsz, seq):
    mixw = ng.shape[-1]
    bcw = cw.shape[1] - mixw
    L = SSM_CHUNK
    nblk = seq // L
    row = lambda b, s: b * nblk + s
    return pl.pallas_call(
        _ssd_kernel,
        grid=(bsz, nblk),
        in_specs=[pl.BlockSpec((L, mixw), lambda b, s: (row(b, s), z_block)),
                  pl.BlockSpec((L, mixw), lambda b, s: (row(b, s), xs_block)),
                  pl.BlockSpec((L, bcw), lambda b, s: (row(b, s), bc_block)),
                  pl.BlockSpec((L, LANES), lambda b, s: (row(b, s), dt_block)),
                  pl.BlockSpec(cw.shape, lambda b, s: (0, 0)),
                  pl.BlockSpec(cb.shape, lambda b, s: (0, 0)),
                  pl.BlockSpec((1, LANES), lambda b, s: (0, 0)),
                  pl.BlockSpec((1, LANES), lambda b, s: (0, 0)),
                  pl.BlockSpec((1, LANES), lambda b, s: (0, 0)),
                  pl.BlockSpec((1, mixw), lambda b, s: (0, 0))],
        out_specs=pl.BlockSpec((L, mixw), lambda b, s: (row(b, s), 0)),
        out_shape=jax.ShapeDtypeStruct((bsz * seq, mixw), F32),
        scratch_shapes=[pltpu.VMEM((SSM_HIST + L, mixw), F32),
                        pltpu.VMEM((SSM_HIST + L, bcw), F32),
                        pltpu.VMEM((SSM_HEADS, SSM_STATE, SSM_HEAD_DIM), F32)],
        compiler_params=_cparams(2),
        name="ssd_mixer",
    )(proj, proj, proj, proj, cw, cb, dtb, a_neg, dsk, ng)


def _rope_cols(x, c, s1, s2, half):
    return (x * c + pltpu.roll(x, LANES - half, axis=1) * s1 + pltpu.roll(x, half, axis=1) * s2)


LOG2E = 1.4426950408889634


def _rope_prep_kernel(q_ref, kv_ref, idx_ref, ta_ref, ti_ref, tt_ref,
                      qg_ref, k_ref, vt_ref, qig_ref, ki_ref, wg_ref):
    hd = HEAD_DIM
    ts = q_ref.shape[0]
    rep = ATT_HEADS // ATT_KV_HEADS
    scale = (hd ** -0.5) * LOG2E
    ca, s1a, s2a = ta_ref[0], ta_ref[1], ta_ref[2]
    heads_per_col = LANES // hd
    for c in range(q_ref.shape[1] // LANES):
        qc = _rope_cols(q_ref[:, c * LANES:(c + 1) * LANES], ca, s1a, s2a, ROPE_DIM // 2) * scale
        qct = qc.T.astype(qg_ref.dtype)
        for r in range(heads_per_col):
            h = c * heads_per_col + r
            g, hh = h // rep, h % rep
            for jb in range(ts // LANES):
                dst = jb * rep * LANES + hh * LANES
                qg_ref[g, :, dst:dst + LANES] = qct[r * hd:(r + 1) * hd, jb * LANES:(jb + 1) * LANES]
    kw = ATT_KV_HEADS * hd
    kc = _rope_cols(kv_ref[:, 0:kw], ca, s1a, s2a, ROPE_DIM // 2)
    for g in range(ATT_KV_HEADS):
        k_ref[g] = kc[:, g * hd:(g + 1) * hd].astype(k_ref.dtype)
    vt_ref[...] = kv_ref[:, kw:2 * kw].T.astype(vt_ref.dtype)
    ci, s1i, s2i = ti_ref[0], ti_ref[1], ti_ref[2]
    nq = IDX_HEADS * IDX_DIM
    iheads_per_col = LANES // IDX_DIM
    for c in range(nq // LANES):
        qc = _rope_cols(idx_ref[:, c * LANES:(c + 1) * LANES], ci, s1i, s2i, IDX_ROPE_DIM // 2)
        qct = qc.T.astype(qig_ref.dtype)
        for r in range(iheads_per_col):
            h = c * iheads_per_col + r
            for jb in range(ts // LANES):
                dst = jb * IDX_HEADS * LANES + h * LANES
                qig_ref[:, dst:dst + LANES] = qct[r * IDX_DIM:(r + 1) * IDX_DIM, jb * LANES:(jb + 1) * LANES]
    tail = _rope_cols(idx_ref[:, nq:nq + LANES], tt_ref[0], tt_ref[1], tt_ref[2], IDX_ROPE_DIM // 2)
    ki_ref[...] = tail[:, 0:IDX_DIM].astype(ki_ref.dtype)
    tail_t = tail.T
    for h in range(IDX_HEADS):
        for jb in range(ts // LANES):
            dst = jb * IDX_HEADS * LANES + h * LANES
            wg_ref[:, dst:dst + LANES] = tail_t[IDX_DIM + h:IDX_DIM + h + 1, jb * LANES:(jb + 1) * LANES]


def _rope_prep(proj, q_block, kv_block, idx_block, tab_att, tab_idx, tab_tail, bsz, seq, ts):
    nblk = seq // ts
    qw = ATT_HEADS * HEAD_DIM
    kw = ATT_KV_HEADS * HEAD_DIM
    idxw = 4 * LANES
    rep = ATT_HEADS // ATT_KV_HEADS
    row = lambda b, s: b * nblk + s
    tab_spec = pl.BlockSpec((3, ts, LANES), lambda b, s: (0, s, 0))
    return pl.pallas_call(
        _rope_prep_kernel,
        grid=(bsz, nblk),
        in_specs=[pl.BlockSpec((ts, qw), lambda b, s: (row(b, s), q_block)),
                  pl.BlockSpec((ts, 2 * kw), lambda b, s: (row(b, s), kv_block)),
                  pl.BlockSpec((ts, idxw), lambda b, s: (row(b, s), idx_block)),
                  tab_spec, tab_spec, tab_spec],
        out_specs=[pl.BlockSpec((None, ATT_KV_HEADS, HEAD_DIM, rep * ts), lambda b, s: (b, 0, 0, s)),
                   pl.BlockSpec((None, ATT_KV_HEADS, ts, HEAD_DIM), lambda b, s: (b, 0, s, 0)),
                   pl.BlockSpec((None, kw, ts), lambda b, s: (b, 0, s)),
                   pl.BlockSpec((None, IDX_DIM, IDX_HEADS * ts), lambda b, s: (b, 0, s)),
                   pl.BlockSpec((None, ts, IDX_DIM), lambda b, s: (b, s, 0)),
                   pl.BlockSpec((None, 1, IDX_HEADS * ts), lambda b, s: (b, 0, s))],
        out_shape=[jax.ShapeDtypeStruct((bsz, ATT_KV_HEADS, HEAD_DIM, rep * seq), BF16),
                   jax.ShapeDtypeStruct((bsz, ATT_KV_HEADS, seq, HEAD_DIM), BF16),
                   jax.ShapeDtypeStruct((bsz, kw, seq), BF16),
                   jax.ShapeDtypeStruct((bsz, IDX_DIM, IDX_HEADS * seq), BF16),
                   jax.ShapeDtypeStruct((bsz, seq, IDX_DIM), BF16),
                   jax.ShapeDtypeStruct((bsz, 1, IDX_HEADS * seq), F32)],
        compiler_params=_cparams(2),
        name="rope_prep",
    )(proj, proj, proj, tab_att, tab_idx, tab_tail)


def _rope_tables(seq, head_dim, rot_dim, n_rot_heads):
    half = rot_dim // 2
    inv_freq = jnp.power(jnp.float32(ROPE_THETA), -jnp.arange(half, dtype=F32) * (2.0 / rot_dim))
    ang = jnp.arange(seq, dtype=F32)[:, None] * inv_freq[None, :]
    cos, sin = jnp.cos(ang), jnp.sin(ang)
    lane = np.arange(LANES)
    j = lane % head_dim
    rot_head = lane < n_rot_heads * head_dim
    first = rot_head & (j < half)
    second = rot_head & (j >= half) & (j < rot_dim)
    fi = np.where(first, j, 0)
    si = np.where(second, j - half, 0)
    c = jnp.where(first[None, :], cos[:, fi], jnp.where(second[None, :], cos[:, si], 1.0))
    s1 = jnp.where(first[None, :], -sin[:, fi], 0.0)
    s2 = jnp.where(second[None, :], sin[:, si], 0.0)
    return jnp.stack([c, s1, s2]).astype(F32)


def _sortable_key(score):
    score = jnp.where(score == 0.0, 0.0, score)
    bits = pltpu.bitcast(score, jnp.int32)
    return bits ^ (lax.shift_right_arithmetic(bits, 31) & jnp.int32(0x7FFFFFFF))


KEY_CHUNK = 256
ATT_CHUNK = 512
COUNT_ROWS = 4 * SUBLANES


def _dsa_kernel(qg_ref, qig_ref, wg_ref, k_ref, ki_ref, vt_ref, o_ref, keys_ref, acc_ref, m_ref, s_ref,
                *, top_k, seq_bits):
    j = pl.program_id(1)
    qb = LANES
    kc = KEY_CHUNK
    hd = HEAD_DIM
    rep = ATT_HEADS // ATT_KV_HEADS
    n_kc = lax.div(j * qb + (qb + kc - 1), kc)
    sub_i = lax.broadcasted_iota(jnp.int32, (kc, qb), 0)
    q_pos = j * qb + lax.broadcasted_iota(jnp.int32, (kc, qb), 1)

    def score_body(c, carry):
        start = pl.multiple_of(c * kc, kc)
        kic = ki_ref[pl.ds(start, kc), :]
        score = jnp.zeros((kc, qb), F32)
        for h0 in range(0, IDX_HEADS, 2):
            s2 = jnp.dot(kic, qig_ref[:, h0 * qb:(h0 + 2) * qb], preferred_element_type=F32)
            for r in range(2):
                h = h0 + r
                score = score + jnp.maximum(s2[:, r * qb:(r + 1) * qb], 0.0) * wg_ref[:, h * qb:(h + 1) * qb]
        causal = (start + sub_i) <= q_pos
        keys_ref[pl.ds(start, kc), :] = jnp.where(causal, _sortable_key(score), jnp.int32(INT_MIN))
        return carry

    lax.fori_loop(0, n_kc, score_body, 0)

    def count_where(pred_fn):
        def body(c, acc):
            start = pl.multiple_of(c * kc, kc)
            ones = jnp.where(pred_fn(keys_ref[pl.ds(start, kc), :], start), 1.0, 0.0)
            return acc + jnp.sum(ones.reshape(kc // COUNT_ROWS, COUNT_ROWS, qb), axis=0)
        acc = lax.fori_loop(0, n_kc, body, jnp.zeros((COUNT_ROWS, qb), F32))
        return jnp.sum(acc, axis=0, keepdims=True)

    def bit_body(i, tau):
        cand = tau ^ lax.shift_left(jnp.int32(1), 31 - i)
        cnt = count_where(lambda kk, _: kk >= cand)
        return jnp.where(cnt >= top_k, cand, tau)

    tau = lax.fori_loop(0, 32, bit_body, jnp.full((1, qb), INT_MIN, jnp.int32))
    tau = jnp.maximum(tau, jnp.int32(INT_MIN + 1))

    n_gt = count_where(lambda kk, _: kk > tau)
    n_eq = count_where(lambda kk, _: kk == tau)
    need = top_k - n_gt
    has_excess = jnp.max(jnp.where(n_eq > need, 1.0, 0.0)) > 0.5

    @pl.when(has_excess)
    def _():
        def pos_body(i, xcut):
            cand = xcut | lax.shift_left(jnp.int32(1), seq_bits - 1 - i)
            cnt = count_where(lambda kk, st: (kk == tau) & ((st + sub_i) < cand))
            return jnp.where(cnt < need, cand, xcut)

        xcut = lax.fori_loop(0, seq_bits, pos_body, jnp.zeros((1, qb), jnp.int32))
        def fix_body(c, carry):
            start = pl.multiple_of(c * kc, kc)
            kk = keys_ref[pl.ds(start, kc), :]
            drop = (kk == tau) & (((start + sub_i) > xcut) | (need < 1))
            keys_ref[pl.ds(start, kc), :] = jnp.where(drop, jnp.int32(INT_MIN), kk)
            return carry

        lax.fori_loop(0, n_kc, fix_body, 0)

    m_ref[...] = jnp.full(m_ref.shape, NEG_BIG, F32)
    s_ref[...] = jnp.zeros(s_ref.shape, F32)
    acc_ref[...] = jnp.zeros(acc_ref.shape, F32)

    ac = ATT_CHUNK
    n_ac = lax.div(j * qb + (qb + ac - 1), ac)

    def fill_body(c, carry):
        keys_ref[pl.ds(pl.multiple_of(c * kc, kc), kc), :] = jnp.full((kc, qb), INT_MIN, jnp.int32)
        return carry

    lax.fori_loop(n_kc, n_ac * (ac // kc), fill_body, 0)

    def att_body(c, carry):
        start = pl.multiple_of(c * ac, ac)
        bias = jnp.where(keys_ref[pl.ds(start, ac), :] >= tau, 0.0, NEG_BIG)
        for g in range(ATT_KV_HEADS):
            lg = jnp.dot(k_ref[g, pl.ds(start, ac), :], qg_ref[g], preferred_element_type=F32)
            ps, alphas = [], []
            for hh in range(rep):
                h = g * rep + hh
                logit = lg[:, hh * qb:(hh + 1) * qb] + bias
                m_old = m_ref[h:h + 1, :]
                m_new = jnp.maximum(m_old, jnp.max(logit, axis=0, keepdims=True))
                p = jnp.exp2(logit - m_new)
                alpha = jnp.exp2(m_old - m_new)
                s_ref[h:h + 1, :] = s_ref[h:h + 1, :] * alpha + jnp.sum(p, axis=0, keepdims=True)
                m_ref[h:h + 1, :] = m_new
                ps.append(p.astype(BF16))
                alphas.append(alpha)
            pv = jnp.dot(vt_ref[g * hd:(g + 1) * hd, pl.ds(start, ac)], jnp.concatenate(ps, axis=1),
                         preferred_element_type=F32)
            for hh in range(rep):
                rows = slice((g * rep + hh) * hd, (g * rep + hh + 1) * hd)
                acc_ref[rows, :] = acc_ref[rows, :] * alphas[hh] + pv[:, hh * qb:(hh + 1) * qb]
        return carry

    lax.fori_loop(0, n_ac, att_body, 0)
    for h in range(ATT_HEADS):
        rows = slice(h * hd, (h + 1) * hd)
        acc_ref[rows, :] = acc_ref[rows, :] / s_ref[h:h + 1, :]
    for c0 in range(0, ATT_HEADS * hd, LANES):
        o_ref[:, c0:c0 + LANES] = acc_ref[c0:c0 + LANES, :].T


def _dsa_attention(qg, qig, wg, k, ki, vt, bsz, seq):
    qb = LANES
    nblk = seq // qb
    qw = ATT_HEADS * HEAD_DIM
    kw = ATT_KV_HEADS * HEAD_DIM
    rep = ATT_HEADS // ATT_KV_HEADS
    top_k = min(TOPK_MAX, seq // 4)
    seq_bits = int(np.log2(seq))
    assert 2 ** seq_bits == seq and seq % ATT_CHUNK == 0 and ATT_CHUNK % KEY_CHUNK == 0
    kern = functools.partial(_dsa_kernel, top_k=top_k, seq_bits=seq_bits)
    return pl.pallas_call(
        kern,
        grid=(bsz, nblk),
        in_specs=[pl.BlockSpec((None, ATT_KV_HEADS, HEAD_DIM, rep * qb), lambda b, j: (b, 0, 0, j)),
                  pl.BlockSpec((None, IDX_DIM, IDX_HEADS * qb), lambda b, j: (b, 0, j)),
                  pl.BlockSpec((None, 1, IDX_HEADS * qb), lambda b, j: (b, 0, j)),
                  pl.BlockSpec((None, ATT_KV_HEADS, seq, HEAD_DIM), lambda b, j: (b, 0, 0, 0)),
                  pl.BlockSpec((None, seq, IDX_DIM), lambda b, j: (b, 0, 0)),
                  pl.BlockSpec((None, kw, seq), lambda b, j: (b, 0, 0))],
        out_specs=pl.BlockSpec((qb, qw), lambda b, j: (b * nblk + j, 0)),
        out_shape=jax.ShapeDtypeStruct((bsz * seq, qw), F32),
        scratch_shapes=[pltpu.VMEM((seq, qb), jnp.int32), pltpu.VMEM((qw, qb), F32),
                        pltpu.VMEM((ATT_HEADS, qb), F32), pltpu.VMEM((ATT_HEADS, qb), F32)],
        compiler_params=_cparams(2),
        name="dsa_attention",
    )(qg, qig, wg, k, ki, vt)


def _merge_kernel(x_ref, ya_ref, yb_ref, yc_ref, yd_ref, gl_ref, wbr_ref, wout_ref, o_ref):
    d = x_ref.shape[1]
    merged = None
    for n, y_ref in enumerate((ya_ref, yb_ref, yc_ref, yd_ref)):
        u = jnp.dot(y_ref[...].astype(BF16), wbr_ref[n], preferred_element_type=F32)
        term = _sigmoid(gl_ref[:, n * d:(n + 1) * d]) * u
        merged = term if merged is None else merged + term
    o_ref[...] = x_ref[...] + jnp.dot(merged.astype(BF16), wout_ref[...], preferred_element_type=F32)


def _merge(x, ya, yb, yc, yd, proj, gate_block, wbr, wout, tm):
    t, d = x.shape
    mixw = ya.shape[1]
    nbr = wbr.shape[0]
    y_spec = pl.BlockSpec((tm, mixw), lambda i: (i, 0))
    return pl.pallas_call(
        _merge_kernel,
        grid=(t // tm,),
        in_specs=[pl.BlockSpec((tm, d), lambda i: (i, 0)), y_spec, y_spec, y_spec, y_spec,
                  pl.BlockSpec((tm, nbr * d), lambda i: (i, gate_block)),
                  pl.BlockSpec(wbr.shape, lambda i: (0, 0, 0)),
                  pl.BlockSpec(wout.shape, lambda i: (0, 0))],
        out_specs=pl.BlockSpec((tm, d), lambda i: (i, 0)),
        out_shape=jax.ShapeDtypeStruct((t, d), F32),
        compiler_params=_cparams(1),
        name="merge_out_proj",
    )(x, ya, yb, yc, yd, proj, wbr, wout)


def _ffn_kernel(x_ref, g_ref, wg_ref, wu_ref, wd_ref, fin_ref, o_ref, h_ref, acc_ref, *, final_norm):
    f = pl.program_id(1)

    @pl.when(f == 0)
    def _():
        h_ref[...] = _rms_rows(x_ref[...], g_ref[...]).astype(h_ref.dtype)
        acc_ref[...] = jnp.zeros(acc_ref.shape, F32)

    h = h_ref[...]
    a = jnp.dot(h, wg_ref[...], preferred_element_type=F32)
    b = jnp.dot(h, wu_ref[...], preferred_element_type=F32)
    t = (a * _sigmoid(a)) * b
    acc_ref[...] += jnp.dot(t.astype(BF16), wd_ref[...], preferred_element_type=F32)

    @pl.when(f == pl.num_programs(1) - 1)
    def _():
        y = x_ref[...] + acc_ref[...]
        o_ref[...] = _rms_rows(y, fin_ref[...]) if final_norm else y


def _ffn(x, g, wg, wu, wd, fin, final_norm, tm, tf):
    t, d = x.shape
    ff = wg.shape[1]
    return pl.pallas_call(
        functools.partial(_ffn_kernel, final_norm=final_norm),
        grid=(t // tm, ff // tf),
        in_specs=[pl.BlockSpec((tm, d), lambda i, f: (i, 0)),
                  pl.BlockSpec((1, d), lambda i, f: (0, 0)),
                  pl.BlockSpec((d, tf), lambda i, f: (0, f)),
                  pl.BlockSpec((d, tf), lambda i, f: (0, f)),
                  pl.BlockSpec((tf, d), lambda i, f: (f, 0)),
                  pl.BlockSpec((1, d), lambda i, f: (0, 0))],
        out_specs=pl.BlockSpec((tm, d), lambda i, f: (i, 0)),
        out_shape=jax.ShapeDtypeStruct((t, d), F32),
        scratch_shapes=[pltpu.VMEM((tm, d), BF16), pltpu.VMEM((tm, d), F32)],
        compiler_params=_cparams(2),
        name="dense_swiglu",
    )(x, g, wg, wu, wd, fin)


def _moe_kernel(x_ref, g_ref, r_ref, wg_ref, wu_ref, wd_ref, fin_ref, o_ref, h_ref, comb_ref, acc_ref,
                *, n_experts, final_norm):
    e = pl.program_id(1)
    f = pl.program_id(2)
    tm = x_ref.shape[0]
    lane = lax.broadcasted_iota(jnp.int32, (tm, LANES), 1)

    @pl.when((e == 0) & (f == 0))
    def _():
        hf = _rms_rows(x_ref[...], g_ref[...])
        h_ref[...] = hf.astype(h_ref.dtype)
        acc_ref[...] = jnp.zeros(acc_ref.shape, F32)
        logits = jnp.dot(hf, r_ref[...], preferred_element_type=F32, precision=lax.Precision.HIGHEST)
        logits = jnp.where(lane < n_experts, logits, -jnp.inf)
        m1 = jnp.max(logits, axis=-1, keepdims=True)
        i1 = jnp.min(jnp.where(logits == m1, lane, LANES), axis=-1, keepdims=True)
        rest = jnp.where(lane == i1, -jnp.inf, logits)
        m2 = jnp.max(rest, axis=-1, keepdims=True)
        i2 = jnp.min(jnp.where(rest == m2, lane, LANES), axis=-1, keepdims=True)
        e2 = jnp.exp(m2 - m1)
        p1 = 1.0 / (1.0 + e2)
        p2 = e2 / (1.0 + e2)
        comb_ref[...] = jnp.where(lane == i1, p1, 0.0) + jnp.where(lane == i2, p2, 0.0)

    c_e = jnp.sum(jnp.where(lane == e, comb_ref[...], 0.0), axis=-1, keepdims=True)
    h = h_ref[...]
    a = jnp.dot(h, wg_ref[...], preferred_element_type=F32)
    b = jnp.dot(h, wu_ref[...], preferred_element_type=F32)
    t = (a * _sigmoid(a)) * b
    acc_ref[...] += c_e * jnp.dot(t.astype(BF16), wd_ref[...], preferred_element_type=F32)

    @pl.when((e == pl.num_programs(1) - 1) & (f == pl.num_programs(2) - 1))
    def _():
        y = x_ref[...] + acc_ref[...]
        o_ref[...] = _rms_rows(y, fin_ref[...]) if final_norm else y


def _moe(x, g, router, wg, wu, wd, fin, final_norm, tm, tf):
    t, d = x.shape
    n_experts, _, ff = wg.shape
    return pl.pallas_call(
        functools.partial(_moe_kernel, n_experts=n_experts, final_norm=final_norm),
        grid=(t // tm, n_experts, ff // tf),
        in_specs=[pl.BlockSpec((tm, d), lambda i, e, f: (i, 0)),
                  pl.BlockSpec((1, d), lambda i, e, f: (0, 0)),
                  pl.BlockSpec((d, LANES), lambda i, e, f: (0, 0)),
                  pl.BlockSpec((None, d, tf), lambda i, e, f: (e, 0, f)),
                  pl.BlockSpec((None, d, tf), lambda i, e, f: (e, 0, f)),
                  pl.BlockSpec((None, tf, d), lambda i, e, f: (e, f, 0)),
                  pl.BlockSpec((1, d), lambda i, e, f: (0, 0))],
        out_specs=pl.BlockSpec((tm, d), lambda i, e, f: (i, 0)),
        out_shape=jax.ShapeDtypeStruct((t, d), F32),
        scratch_shapes=[pltpu.VMEM((tm, d), BF16), pltpu.VMEM((tm, LANES), F32), pltpu.VMEM((tm, d), F32)],
        compiler_params=_cparams(3),
        name="moe_swiglu",
    )(x, g, router, wg, wu, wd, fin)


def _pack_w_in(w_in, d_model, mixw):
    qw = ATT_HEADS * HEAD_DIM
    kvw = 2 * ATT_KV_HEADS * HEAD_DIM
    qiw = IDX_HEADS * IDX_DIM
    bcw = 2 * SSM_GROUPS * SSM_STATE
    sizes = (mixw, qw, kvw, qiw, IDX_DIM, IDX_HEADS, 2 * mixw, mixw, mixw + bcw, SSM_HEADS, 4 * d_model)
    offs = np.concatenate([[0], np.cumsum(sizes)])
    assert offs[-1] == w_in.shape[1]
    seg = lambda i: w_in[:, offs[i]:offs[i + 1]]
    u_pool, q, kv, qi, ki, wi, u_conv, z, xbc, dt, gates = (seg(i) for i in range(len(sizes)))
    d = w_in.shape[0]
    zeros = lambda n: jnp.zeros((d, n), w_in.dtype)
    idx_blk = jnp.concatenate([qi, ki, wi, zeros(3 * LANES - qiw - IDX_DIM - IDX_HEADS),
                               dt, zeros(LANES - SSM_HEADS)], axis=1)
    packed = jnp.concatenate([gates, u_conv, u_pool, q, z, xbc[:, :mixw], idx_blk, kv, xbc[:, mixw:]], axis=1)
    widths = dict(gates=4 * d_model, conv=2 * mixw, pool=mixw, q=qw, z=mixw, xs=mixw, idx=4 * LANES,
                  kv=kvw, bc=bcw)
    blocks, off = {}, 0
    for name in ("gates", "conv", "pool", "q", "z", "xs", "idx", "kv", "bc"):
        assert off % widths[name] == 0
        blocks[name] = off // widths[name]
        off += widths[name]
    blocks["dt"] = (blocks["idx"] * 4 * LANES + 3 * LANES) // LANES
    return packed.astype(BF16), blocks


def _pad_lanes(v):
    return jnp.zeros((1, LANES), F32).at[0, :v.shape[0]].set(v.astype(F32))


def _pick_tile(n, target):
    t = min(n, target)
    while n % t:
        t //= 2
    return t


def _pick_lane_tile(n, target):
    best = None
    for m in range(LANES, min(n, target) + 1, LANES):
        if n % m == 0:
            best = m
    return n if best is None else best


def kernel(x, norm_mix, w_in, pool_w, pool_scale, conv_dw, conv_b, conv_ln_g, conv_ln_b, ssm_conv_w, ssm_conv_b, ssm_dt_bias, ssm_a_log, ssm_d, ssm_norm, w_br, w_out, norm_ffn, ffn_w_gate, ffn_w_up, ffn_w_down, moe_router, moe_w_gate, moe_w_up, moe_w_down, final_norm):
    bsz, seq, d = x.shape
    depth = norm_mix.shape[0]
    mixw = pool_scale.shape[-1]
    t = bsz * seq
    row = lambda v: v.reshape(1, -1).astype(F32)

    tab_att = _rope_tables(seq, HEAD_DIM, ROPE_DIM, LANES // HEAD_DIM)
    tab_idx = _rope_tables(seq, IDX_DIM, IDX_ROPE_DIM, LANES // IDX_DIM)
    tab_tail = _rope_tables(seq, IDX_DIM, IDX_ROPE_DIM, 1)

    tm = _pick_tile(t, 1024)
    ts_seq = _pick_tile(seq, 512)
    fin = row(final_norm)

    xf = x.reshape(t, d)
    for layer in range(depth):
        w_packed, blk = _pack_w_in(w_in[layer], d, mixw)
        proj = _norm_matmul(xf, row(norm_mix[layer]), w_packed, tm, _pick_tile(w_packed.shape[1], 1024))

        y_a = _pool_mixer(proj, blk["pool"], pool_w[layer].astype(BF16), row(pool_scale[layer]), bsz, seq, ts_seq)
        qt, k_r, vt, qit, ki_r, wt = _rope_prep(proj, blk["q"], blk["kv"], blk["idx"], tab_att, tab_idx,
                                                tab_tail, bsz, seq, ts_seq)
        y_b = _dsa_attention(qt, qit, wt, k_r, ki_r, vt, bsz, seq)
        y_c = _conformer(proj, blk["conv"], conv_dw[layer], row(conv_b[layer]), row(conv_ln_g[layer]),
                         row(conv_ln_b[layer]), bsz, seq, _pick_tile(seq, 256))
        y_d = _ssd(proj, blk["z"], blk["xs"], blk["bc"], blk["dt"], ssm_conv_w[layer], row(ssm_conv_b[layer]),
                   _pad_lanes(ssm_dt_bias[layer]), _pad_lanes(-jnp.exp(ssm_a_log[layer].astype(F32))),
                   _pad_lanes(ssm_d[layer]), row(ssm_norm[layer]), bsz, seq)
        xf = _merge(xf, y_a, y_b, y_c, y_d, proj, blk["gates"], w_br[layer].astype(BF16),
                    w_out[layer].astype(BF16), _pick_tile(t, 512))

        last = layer == depth - 1
        jj = layer // 2
        if layer % 2 == 0:
            ff = ffn_w_gate.shape[-1]
            xf = _ffn(xf, row(norm_ffn[layer]), ffn_w_gate[jj].astype(BF16), ffn_w_up[jj].astype(BF16),
                      ffn_w_down[jj].astype(BF16), fin, last, _pick_tile(t, 512), _pick_lane_tile(ff, 1408))
        else:
            ff = moe_w_gate.shape[-1]
            n_exp = moe_router.shape[-1]
            router = jnp.zeros((d, LANES), F32).at[:, :n_exp].set(moe_router[jj].astype(F32))
            xf = _moe(xf, row(norm_ffn[layer]), router, moe_w_gate[jj].astype(BF16), moe_w_up[jj].astype(BF16),
                      moe_w_down[jj].astype(BF16), fin, last, tm, _pick_lane_tile(ff, 896))
    if depth == 0:
        raise ValueError("depth must be positive")
    return xf.reshape(bsz, seq, d)
```

```python
import functools

import numpy as np
import jax
import jax.numpy as jnp
from jax import lax
from jax.experimental import pallas as pl
from jax.experimental.pallas import tpu as pltpu

POOL_WINDOWS = (2, 4, 8, 16)
ATT_HEADS = 8
ATT_KV_HEADS = 2
HEAD_DIM = 64
ROPE_DIM = HEAD_DIM // 4
ROPE_THETA = 500000.0
IDX_HEADS = 8
IDX_DIM = 32
IDX_ROPE_DIM = IDX_DIM // 4
TOPK_MAX = 256
SSM_HEADS = 8
SSM_HEAD_DIM = 64
SSM_GROUPS = 2
SSM_STATE = 64
SSM_CHUNK = 128
TOP_K = 2
NORM_EPS = 1e-6

LANES = 128
SUBLANES = 8
VMEM_LIMIT_BYTES = 56 * 1024 * 1024

F32 = jnp.float32
BF16 = jnp.bfloat16
INT_MIN = -(2 ** 31)
NEG_BIG = -1e30


def _cparams(n_axes):
    return pltpu.CompilerParams(dimension_semantics=("arbitrary",) * n_axes,
                                vmem_limit_bytes=VMEM_LIMIT_BYTES)


def _sigmoid(x):
    return 1.0 / (1.0 + jnp.exp(-x))


def _rms_rows(x, g):
    return x * lax.rsqrt(jnp.mean(x * x, axis=-1, keepdims=True) + NORM_EPS) * g


def _norm_matmul_kernel(x_ref, g_ref, w_ref, o_ref, h_ref):
    @pl.when(pl.program_id(1) == 0)
    def _():
        h_ref[...] = _rms_rows(x_ref[...], g_ref[...]).astype(h_ref.dtype)

    o_ref[...] = jnp.dot(h_ref[...], w_ref[...], preferred_element_type=F32)


def _norm_matmul(x, g, w, tm, tn):
    t, d = x.shape
    n = w.shape[1]
    return pl.pallas_call(
        _norm_matmul_kernel,
        grid=(t // tm, n // tn),
        in_specs=[pl.BlockSpec((tm, d), lambda i, j: (i, 0)),
                  pl.BlockSpec((1, d), lambda i, j: (0, 0)),
                  pl.BlockSpec((d, tn), lambda i, j: (0, j))],
        out_specs=pl.BlockSpec((tm, tn), lambda i, j: (i, j)),
        out_shape=jax.ShapeDtypeStruct((t, n), F32),
        scratch_shapes=[pltpu.VMEM((tm, d), BF16)],
        compiler_params=_cparams(2),
        name="norm_in_proj",
    )(x, g, w)


POOL_HIST = 16


def _pool_kernel(u_ref, w_ref, sc_ref, o_ref, buf_ref):
    s = pl.program_id(1)
    ts = u_ref.shape[0]
    gw = w_ref.shape[1]

    @pl.when(s == 0)
    def _():
        buf_ref[0:POOL_HIST, :] = jnp.zeros((POOL_HIST, buf_ref.shape[1]), F32)

    @pl.when(s > 0)
    def _():
        buf_ref[0:POOL_HIST, :] = buf_ref[ts:ts + POOL_HIST, :]

    buf_ref[POOL_HIST:POOL_HIST + ts, :] = u_ref[...]
    pos = s * ts + lax.broadcasted_iota(jnp.int32, (ts, 1), 0)
    for g, win in enumerate(POOL_WINDOWS):
        cols = slice(g * gw, (g + 1) * gw)
        acc = buf_ref[POOL_HIST:POOL_HIST + ts, cols]
        cur = acc
        for k in range(1, win):
            acc = acc + buf_ref[POOL_HIST - k:POOL_HIST - k + ts, cols]
        cnt = jnp.minimum(pos + 1, win).astype(F32)
        p = acc / cnt - cur
        y = jnp.dot(p.astype(BF16), w_ref[g], preferred_element_type=F32)
        o_ref[:, cols] = y * sc_ref[:, cols]


def _pool_mixer(proj, col_block, w, scale, bsz, seq, ts):
    mixw = scale.shape[-1]
    nblk = seq // ts
    return pl.pallas_call(
        _pool_kernel,
        grid=(bsz, nblk),
        in_specs=[pl.BlockSpec((ts, mixw), lambda b, s: (b * nblk + s, col_block)),
                  pl.BlockSpec(w.shape, lambda b, s: (0, 0, 0)),
                  pl.BlockSpec((1, mixw), lambda b, s: (0, 0))],
        out_specs=pl.BlockSpec((ts, mixw), lambda b, s: (b * nblk + s, 0)),
        out_shape=jax.ShapeDtypeStruct((bsz * seq, mixw), F32),
        scratch_shapes=[pltpu.VMEM((POOL_HIST + ts, mixw), F32)],
        compiler_params=_cparams(2),
        name="pool_mixer",
    )(proj, w, scale)


CONV_HIST = 32


def _conformer_kernel(u_ref, dw_ref, db_ref, lg_ref, lb_ref, o_ref, buf_ref):
    s = pl.program_id(1)
    ts = u_ref.shape[0]
    c = o_ref.shape[1]
    width = dw_ref.shape[0]

    @pl.when(s == 0)
    def _():
        buf_ref[0:CONV_HIST, :] = jnp.zeros((CONV_HIST, c), F32)

    @pl.when(s > 0)
    def _():
        buf_ref[0:CONV_HIST, :] = buf_ref[ts:ts + CONV_HIST, :]

    a = u_ref[:, 0:c]
    gt = u_ref[:, c:2 * c]
    buf_ref[CONV_HIST:CONV_HIST + ts, :] = a * _sigmoid(gt)
    rc = min(ts, LANES)
    for r0 in range(0, ts, rc):
        for c0 in range(0, c, LANES):
            acc = jnp.zeros((rc, LANES), F32) + db_ref[:, c0:c0 + LANES]
            for k in range(width):
                off = CONV_HIST - (width - 1) + k + r0
                acc = acc + buf_ref[off:off + rc, c0:c0 + LANES] * dw_ref[k:k + 1, c0:c0 + LANES]
            o_ref[r0:r0 + rc, c0:c0 + LANES] = acc
    acc = o_ref[...]
    mu = jnp.mean(acc, axis=-1, keepdims=True)
    xc = acc - mu
    y = xc * lax.rsqrt(jnp.mean(xc * xc, axis=-1, keepdims=True) + NORM_EPS)
    y = y * lg_ref[...] + lb_ref[...]
    o_ref[...] = y * _sigmoid(y)


def _conformer(proj, col_block, dw, db, lg, lb, bsz, seq, ts):
    c = dw.shape[1]
    nblk = seq // ts
    return pl.pallas_call(
        _conformer_kernel,
        grid=(bsz, nblk),
        in_specs=[pl.BlockSpec((ts, 2 * c), lambda b, s: (b * nblk + s, col_block)),
                  pl.BlockSpec(dw.shape, lambda b, s: (0, 0)),
                  pl.BlockSpec((1, c), lambda b, s: (0, 0)),
                  pl.BlockSpec((1, c), lambda b, s: (0, 0)),
                  pl.BlockSpec((1, c), lambda b, s: (0, 0))],
        out_specs=pl.BlockSpec((ts, c), lambda b, s: (b * nblk + s, 0)),
        out_shape=jax.ShapeDtypeStruct((bsz * seq, c), F32),
        scratch_shapes=[pltpu.VMEM((CONV_HIST + ts, c), F32)],
        compiler_params=_cparams(2),
        name="conformer_conv",
    )(proj, dw, db, lg, lb)


SSM_HIST = 8


def _ssd_kernel(z_ref, xs_ref, bc_ref, dt_ref, cw_ref, cb_ref, dtb_ref, a_ref, dsk_ref, ng_ref,
                o_ref, xbuf_ref, bcbuf_ref, state_ref):
    c_idx = pl.program_id(1)
    L = xs_ref.shape[0]
    mixw = xs_ref.shape[1]
    bcw = bc_ref.shape[1]
    width = cw_ref.shape[0]
    hp = SSM_HEAD_DIM
    ns = SSM_STATE
    rep = SSM_HEADS // SSM_GROUPS

    @pl.when(c_idx == 0)
    def _():
        xbuf_ref[0:SSM_HIST, :] = jnp.zeros((SSM_HIST, mixw), F32)
        bcbuf_ref[0:SSM_HIST, :] = jnp.zeros((SSM_HIST, bcw), F32)
        state_ref[...] = jnp.zeros(state_ref.shape, F32)

    @pl.when(c_idx > 0)
    def _():
        xbuf_ref[0:SSM_HIST, :] = xbuf_ref[L:L + SSM_HIST, :]
        bcbuf_ref[0:SSM_HIST, :] = bcbuf_ref[L:L + SSM_HIST, :]

    xbuf_ref[SSM_HIST:SSM_HIST + L, :] = xs_ref[...]
    bcbuf_ref[SSM_HIST:SSM_HIST + L, :] = bc_ref[...]

    xc = jnp.zeros((L, mixw), F32) + cb_ref[:, 0:mixw]
    bcc = jnp.zeros((L, bcw), F32) + cb_ref[:, mixw:mixw + bcw]
    for k in range(width):
        off = SSM_HIST - (width - 1) + k
        xc = xc + xbuf_ref[off:off + L, :] * cw_ref[k:k + 1, 0:mixw]
        bcc = bcc + bcbuf_ref[off:off + L, :] * cw_ref[k:k + 1, mixw:mixw + bcw]
    xc = xc * _sigmoid(xc)
    bcc = bcc * _sigmoid(bcc)

    dtr = dt_ref[...] + dtb_ref[...]
    dt = jnp.maximum(dtr, 0.0) + jnp.log(1.0 + jnp.exp(-jnp.abs(dtr)))
    da = dt * a_ref[...]
    row_i = lax.broadcasted_iota(jnp.int32, (L, L), 0)
    col_i = lax.broadcasted_iota(jnp.int32, (L, L), 1)
    tri = col_i <= row_i
    a_cum = jnp.dot(tri.astype(F32), da, preferred_element_type=F32,
                    precision=lax.Precision.HIGHEST)
    a_cum_t = a_cum.T
    a_last = a_cum[L - 1:L, :]
    dec_last = jnp.exp(a_last)
    dec_out = jnp.exp(a_cum)
    dec_st = jnp.exp(a_last - a_cum)

    cbs = []
    for g in range(SSM_GROUPS):
        bg = bcc[:, g * ns:(g + 1) * ns].astype(BF16)
        cg = bcc[:, SSM_GROUPS * ns + g * ns:SSM_GROUPS * ns + (g + 1) * ns].astype(BF16)
        cbs.append(lax.dot_general(cg, bg, (((1,), (1,)), ((), ())), preferred_element_type=F32))

    for h in range(SSM_HEADS):
        g = h // rep
        x_h = xc[:, h * hp:(h + 1) * hp]
        b_g = bcc[:, g * ns:(g + 1) * ns]
        c_g = bcc[:, SSM_GROUPS * ns + g * ns:SSM_GROUPS * ns + (g + 1) * ns]
        xd = x_h * dt[:, h:h + 1]
        seg = a_cum[:, h:h + 1] - a_cum_t[h:h + 1, :]
        lmat = jnp.exp(jnp.where(tri, seg, -jnp.inf))
        y_diag = jnp.dot((cbs[g] * lmat).astype(BF16), xd.astype(BF16), preferred_element_type=F32)
        st = state_ref[h]
        y_off = jnp.dot(c_g.astype(BF16), st.astype(BF16), preferred_element_type=F32) * dec_out[:, h:h + 1]
        bd = (b_g * dec_st[:, h:h + 1]).astype(BF16)
        st_new = lax.dot_general(bd, xd.astype(BF16), (((0,), (0,)), ((), ())),
                                 preferred_element_type=F32)
        state_ref[h] = st * dec_last[:, h:h + 1] + st_new
        y_h = y_diag + y_off + dsk_ref[:, h:h + 1] * x_h
        zz = z_ref[:, h * hp:(h + 1) * hp]
        o_ref[:, h * hp:(h + 1) * hp] = y_h * (zz * _sigmoid(zz))

    y = o_ref[...]
    o_ref[...] = _rms_rows(y, ng_ref[...])


def _ssd(proj, z_block, xs_block, bc_block, dt_block, cw, cb, dtb, a_neg, dsk, ng, bsz, seq):
    mixw = ng.shape[-1]
    bcw = cw.shape[1] - mixw
    L = SSM_CHUNK
    nblk = seq // L
    row = lambda b, s: b * nblk + s
    return pl.pallas_call(
        _ssd_kernel,
        grid=(bsz, nblk),
        in_specs=[pl.BlockSpec((L, mixw), lambda b, s: (row(b, s), z_block)),
                  pl.BlockSpec((L, mixw), lambda b, s: (row(b, s), xs_block)),
                  pl.BlockSpec((L, bcw), lambda b, s: (row(b, s), bc_block)),
                  pl.BlockSpec((L, LANES), lambda b, s: (row(b, s), dt_block)),
                  pl.BlockSpec(cw.shape, lambda b, s: (0, 0)),
                  pl.BlockSpec(cb.shape, lambda b, s: (0, 0)),
                  pl.BlockSpec((1, LANES), lambda b, s: (0, 0)),
                  pl.BlockSpec((1, LANES), lambda b, s: (0, 0)),
                  pl.BlockSpec((1, LANES), lambda b, s: (0, 0)),
                  pl.BlockSpec((1, mixw), lambda b, s: (0, 0))],
        out_specs=pl.BlockSpec((L, mixw), lambda b, s: (row(b, s), 0)),
        out_shape=jax.ShapeDtypeStruct((bsz * seq, mixw), F32),
        scratch_shapes=[pltpu.VMEM((SSM_HIST + L, mixw), F32),
                        pltpu.VMEM((SSM_HIST + L, bcw), F32),
                        pltpu.VMEM((SSM_HEADS, SSM_STATE, SSM_HEAD_DIM), F32)],
        compiler_params=_cparams(2),
        name="ssd_mixer",
    )(proj, proj, proj, proj, cw, cb, dtb, a_neg, dsk, ng)


def _rope_cols(x, c, s1, s2, half):
    return (x * c + pltpu.roll(x, LANES - half, axis=1) * s1 + pltpu.roll(x, half, axis=1) * s2)


LOG2E = 1.4426950408889634


def _rope_prep_kernel(q_ref, kv_ref, idx_ref, ta_ref, ti_ref, tt_ref,
                      qg_ref, k_ref, vt_ref, qig_ref, ki_ref, wg_ref):
    hd = HEAD_DIM
    ts = q_ref.shape[0]
    rep = ATT_HEADS // ATT_KV_HEADS
    scale = (hd ** -0.5) * LOG2E
    ca, s1a, s2a = ta_ref[0], ta_ref[1], ta_ref[2]
    heads_per_col = LANES // hd
    for c in range(q_ref.shape[1] // LANES):
        qc = _rope_cols(q_ref[:, c * LANES:(c + 1) * LANES], ca, s1a, s2a, ROPE_DIM // 2) * scale
        qct = qc.T.astype(qg_ref.dtype)
        for r in range(heads_per_col):
            h = c * heads_per_col + r
            g, hh = h // rep, h % rep
            for jb in range(ts // LANES):
                dst = jb * rep * LANES + hh * LANES
                qg_ref[g, :, dst:dst + LANES] = qct[r * hd:(r + 1) * hd, jb * LANES:(jb + 1) * LANES]
    kw = ATT_KV_HEADS * hd
    kc = _rope_cols(kv_ref[:, 0:kw], ca, s1a, s2a, ROPE_DIM // 2)
    for g in range(ATT_KV_HEADS):
        k_ref[g] = kc[:, g * hd:(g + 1) * hd].astype(k_ref.dtype)
    vt_ref[...] = kv_ref[:, kw:2 * kw].T.astype(vt_ref.dtype)
    ci, s1i, s2i = ti_ref[0], ti_ref[1], ti_ref[2]
    nq = IDX_HEADS * IDX_DIM
    iheads_per_col = LANES // IDX_DIM
    for c in range(nq // LANES):
        qc = _rope_cols(idx_ref[:, c * LANES:(c + 1) * LANES], ci, s1i, s2i, IDX_ROPE_DIM // 2)
        qct = qc.T.astype(qig_ref.dtype)
        for r in range(iheads_per_col):
            h = c * iheads_per_col + r
            for jb in range(ts // LANES):
                dst = jb * IDX_HEADS * LANES + h * LANES
                qig_ref[:, dst:dst + LANES] = qct[r * IDX_DIM:(r + 1) * IDX_DIM, jb * LANES:(jb + 1) * LANES]
    tail = _rope_cols(idx_ref[:, nq:nq + LANES], tt_ref[0], tt_ref[1], tt_ref[2], IDX_ROPE_DIM // 2)
    ki_ref[...] = tail[:, 0:IDX_DIM].astype(ki_ref.dtype)
    tail_t = tail.T
    for h in range(IDX_HEADS):
        for jb in range(ts // LANES):
            dst = jb * IDX_HEADS * LANES + h * LANES
            wg_ref[:, dst:dst + LANES] = tail_t[IDX_DIM + h:IDX_DIM + h + 1, jb * LANES:(jb + 1) * LANES]


def _rope_prep(proj, q_block, kv_block, idx_block, tab_att, tab_idx, tab_tail, bsz, seq, ts):
    nblk = seq // ts
    qw = ATT_HEADS * HEAD_DIM
    kw = ATT_KV_HEADS * HEAD_DIM
    idxw = 4 * LANES
    rep = ATT_HEADS // ATT_KV_HEADS
    row = lambda b, s: b * nblk + s
    tab_spec = pl.BlockSpec((3, ts, LANES), lambda b, s: (0, s, 0))
    return pl.pallas_call(
        _rope_prep_kernel,
        grid=(bsz, nblk),
        in_specs=[pl.BlockSpec((ts, qw), lambda b, s: (row(b, s), q_block)),
                  pl.BlockSpec((ts, 2 * kw), lambda b, s: (row(b, s), kv_block)),
                  pl.BlockSpec((ts, idxw), lambda b, s: (row(b, s), idx_block)),
                  tab_spec, tab_spec, tab_spec],
        out_specs=[pl.BlockSpec((None, ATT_KV_HEADS, HEAD_DIM, rep * ts), lambda b, s: (b, 0, 0, s)),
                   pl.BlockSpec((None, ATT_KV_HEADS, ts, HEAD_DIM), lambda b, s: (b, 0, s, 0)),
                   pl.BlockSpec((None, kw, ts), lambda b, s: (b, 0, s)),
                   pl.BlockSpec((None, IDX_DIM, IDX_HEADS * ts), lambda b, s: (b, 0, s)),
                   pl.BlockSpec((None, ts, IDX_DIM), lambda b, s: (b, s, 0)),
                   pl.BlockSpec((None, 1, IDX_HEADS * ts), lambda b, s: (b, 0, s))],
        out_shape=[jax.ShapeDtypeStruct((bsz, ATT_KV_HEADS, HEAD_DIM, rep * seq), BF16),
                   jax.ShapeDtypeStruct((bsz, ATT_KV_HEADS, seq, HEAD_DIM), BF16),
                   jax.ShapeDtypeStruct((bsz, kw, seq), BF16),
                   jax.ShapeDtypeStruct((bsz, IDX_DIM, IDX_HEADS * seq), BF16),
                   jax.ShapeDtypeStruct((bsz, seq, IDX_DIM), BF16),
                   jax.ShapeDtypeStruct((bsz, 1, IDX_HEADS * seq), F32)],
        compiler_params=_cparams(2),
        name="rope_prep",
    )(proj, proj, proj, tab_att, tab_idx, tab_tail)


def _rope_tables(seq, head_dim, rot_dim, n_rot_heads):
    half = rot_dim // 2
    inv_freq = jnp.power(jnp.float32(ROPE_THETA), -jnp.arange(half, dtype=F32) * (2.0 / rot_dim))
    ang = jnp.arange(seq, dtype=F32)[:, None] * inv_freq[None, :]
    cos, sin = jnp.cos(ang), jnp.sin(ang)
    lane = np.arange(LANES)
    j = lane % head_dim
    rot_head = lane < n_rot_heads * head_dim
    first = rot_head & (j < half)
    second = rot_head & (j >= half) & (j < rot_dim)
    fi = np.where(first, j, 0)
    si = np.where(second, j - half, 0)
    c = jnp.where(first[None, :], cos[:, fi], jnp.where(second[None, :], cos[:, si], 1.0))
    s1 = jnp.where(first[None, :], -sin[:, fi], 0.0)
    s2 = jnp.where(second[None, :], sin[:, si], 0.0)
    return jnp.stack([c, s1, s2]).astype(F32)


def _sortable_key(score):
    score = jnp.where(score == 0.0, 0.0, score)
    bits = pltpu.bitcast(score, jnp.int32)
    return bits ^ (lax.shift_right_arithmetic(bits, 31) & jnp.int32(0x7FFFFFFF))


KEY_CHUNK = 256
ATT_CHUNK = 512
COUNT_ROWS = 4 * SUBLANES


def _dsa_kernel(qg_ref, qig_ref, wg_ref, k_ref, ki_ref, vt_ref, o_ref, keys_ref, acc_ref, m_ref, s_ref,
                *, top_k, seq_bits):
    j = pl.program_id(1)
    qb = LANES
    kc = KEY_CHUNK
    hd = HEAD_DIM
    rep = ATT_HEADS // ATT_KV_HEADS
    n_kc = lax.div(j * qb + (qb + kc - 1), kc)
    sub_i = lax.broadcasted_iota(jnp.int32, (kc, qb), 0)
    q_pos = j * qb + lax.broadcasted_iota(jnp.int32, (kc, qb), 1)

    def score_body(c, carry):
        start = pl.multiple_of(c * kc, kc)
        kic = ki_ref[pl.ds(start, kc), :]
        score = jnp.zeros((kc, qb), F32)
        for h0 in range(0, IDX_HEADS, 2):
            s2 = jnp.dot(kic, qig_ref[:, h0 * qb:(h0 + 2) * qb], preferred_element_type=F32)
            for r in range(2):
                h = h0 + r
                score = score + jnp.maximum(s2[:, r * qb:(r + 1) * qb], 0.0) * wg_ref[:, h * qb:(h + 1) * qb]
        causal = (start + sub_i) <= q_pos
        keys_ref[pl.ds(start, kc), :] = jnp.where(causal, _sortable_key(score), jnp.int32(INT_MIN))
        return carry

    lax.fori_loop(0, n_kc, score_body, 0)

    def count_where(pred_fn):
        def body(c, acc):
            start = pl.multiple_of(c * kc, kc)
            ones = jnp.where(pred_fn(keys_ref[pl.ds(start, kc), :], start), 1.0, 0.0)
            return acc + jnp.sum(ones.reshape(kc // COUNT_ROWS, COUNT_ROWS, qb), axis=0)
        acc = lax.fori_loop(0, n_kc, body, jnp.zeros((COUNT_ROWS, qb), F32))
        return jnp.sum(acc, axis=0, keepdims=True)

    def bit_body(i, tau):
        cand = tau ^ lax.shift_left(jnp.int32(1), 31 - i)
        cnt = count_where(lambda kk, _: kk >= cand)
        return jnp.where(cnt >= top_k, cand, tau)

    tau = lax.fori_loop(0, 32, bit_body, jnp.full((1, qb), INT_MIN, jnp.int32))
    tau = jnp.maximum(tau, jnp.int32(INT_MIN + 1))

    n_gt = count_where(lambda kk, _: kk > tau)
    n_eq = count_where(lambda kk, _: kk == tau)
    need = top_k - n_gt
    has_excess = jnp.max(jnp.where(n_eq > need, 1.0, 0.0)) > 0.5

    @pl.when(has_excess)
    def _():
        def pos_body(i, xcut):
            cand = xcut | lax.shift_left(jnp.int32(1), seq_bits - 1 - i)
            cnt = count_where(lambda kk, st: (kk == tau) & ((st + sub_i) < cand))
            return jnp.where(cnt < need, cand, xcut)

        xcut = lax.fori_loop(0, seq_bits, pos_body, jnp.zeros((1, qb), jnp.int32))
        def fix_body(c, carry):
            start = pl.multiple_of(c * kc, kc)
            kk = keys_ref[pl.ds(start, kc), :]
            drop = (kk == tau) & (((start + sub_i) > xcut) | (need < 1))
            keys_ref[pl.ds(start, kc), :] = jnp.where(drop, jnp.int32(INT_MIN), kk)
            return carry

        lax.fori_loop(0, n_kc, fix_body, 0)

    m_ref[...] = jnp.full(m_ref.shape, NEG_BIG, F32)
    s_ref[...] = jnp.zeros(s_ref.shape, F32)
    acc_ref[...] = jnp.zeros(acc_ref.shape, F32)

    ac = ATT_CHUNK
    n_ac = lax.div(j * qb + (qb + ac - 1), ac)

    def fill_body(c, carry):
        keys_ref[pl.ds(pl.multiple_of(c * kc, kc), kc), :] = jnp.full((kc, qb), INT_MIN, jnp.int32)
        return carry

    lax.fori_loop(n_kc, n_ac * (ac // kc), fill_body, 0)

    def att_body(c, carry):
        start = pl.multiple_of(c * ac, ac)
        bias = jnp.where(keys_ref[pl.ds(start, ac), :] >= tau, 0.0, NEG_BIG)
        for g in range(ATT_KV_HEADS):
            lg = jnp.dot(k_ref[g, pl.ds(start, ac), :], qg_ref[g], preferred_element_type=F32)
            ps, alphas = [], []
            for hh in range(rep):
                h = g * rep + hh
                logit = lg[:, hh * qb:(hh + 1) * qb] + bias
                m_old = m_ref[h:h + 1, :]
                m_new = jnp.maximum(m_old, jnp.max(logit, axis=0, keepdims=True))
                p = jnp.exp2(logit - m_new)
                alpha = jnp.exp2(m_old - m_new)
                s_ref[h:h + 1, :] = s_ref[h:h + 1, :] * alpha + jnp.sum(p, axis=0, keepdims=True)
                m_ref[h:h + 1, :] = m_new
                ps.append(p.astype(BF16))
                alphas.append(alpha)
            pv = jnp.dot(vt_ref[g * hd:(g + 1) * hd, pl.ds(start, ac)], jnp.concatenate(ps, axis=1),
                         preferred_element_type=F32)
            for hh in range(rep):
                rows = slice((g * rep + hh) * hd, (g * rep + hh + 1) * hd)
                acc_ref[rows, :] = acc_ref[rows, :] * alphas[hh] + pv[:, hh * qb:(hh + 1) * qb]
        return carry

    lax.fori_loop(0, n_ac, att_body, 0)
    for h in range(ATT_HEADS):
        rows = slice(h * hd, (h + 1) * hd)
        acc_ref[rows, :] = acc_ref[rows, :] / s_ref[h:h + 1, :]
    for c0 in range(0, ATT_HEADS * hd, LANES):
        o_ref[:, c0:c0 + LANES] = acc_ref[c0:c0 + LANES, :].T


def _dsa_attention(qg, qig, wg, k, ki, vt, bsz, seq):
    qb = LANES
    nblk = seq // qb
    qw = ATT_HEADS * HEAD_DIM
    kw = ATT_KV_HEADS * HEAD_DIM
    rep = ATT_HEADS // ATT_KV_HEADS
    top_k = min(TOPK_MAX, seq // 4)
    seq_bits = int(np.log2(seq))
    assert 2 ** seq_bits == seq and seq % ATT_CHUNK == 0 and ATT_CHUNK % KEY_CHUNK == 0
    kern = functools.partial(_dsa_kernel, top_k=top_k, seq_bits=seq_bits)
    return pl.pallas_call(
        kern,
        grid=(bsz, nblk),
        in_specs=[pl.BlockSpec((None, ATT_KV_HEADS, HEAD_DIM, rep * qb), lambda b, j: (b, 0, 0, j)),
                  pl.BlockSpec((None, IDX_DIM, IDX_HEADS * qb), lambda b, j: (b, 0, j)),
                  pl.BlockSpec((None, 1, IDX_HEADS * qb), lambda b, j: (b, 0, j)),
                  pl.BlockSpec((None, ATT_KV_HEADS, seq, HEAD_DIM), lambda b, j: (b, 0, 0, 0)),
                  pl.BlockSpec((None, seq, IDX_DIM), lambda b, j: (b, 0, 0)),
                  pl.BlockSpec((None, kw, seq), lambda b, j: (b, 0, 0))],
        out_specs=pl.BlockSpec((qb, qw), lambda b, j: (b * nblk + j, 0)),
        out_shape=jax.ShapeDtypeStruct((bsz * seq, qw), F32),
        scratch_shapes=[pltpu.VMEM((seq, qb), jnp.int32), pltpu.VMEM((qw, qb), F32),
                        pltpu.VMEM((ATT_HEADS, qb), F32), pltpu.VMEM((ATT_HEADS, qb), F32)],
        compiler_params=_cparams(2),
        name="dsa_attention",
    )(qg, qig, wg, k, ki, vt)


def _merge_kernel(x_ref, ya_ref, yb_ref, yc_ref, yd_ref, gl_ref, wbr_ref, wout_ref, o_ref):
    d = x_ref.shape[1]
    merged = None
    for n, y_ref in enumerate((ya_ref, yb_ref, yc_ref, yd_ref)):
        u = jnp.dot(y_ref[...].astype(BF16), wbr_ref[n], preferred_element_type=F32)
        term = _sigmoid(gl_ref[:, n * d:(n + 1) * d]) * u
        merged = term if merged is None else merged + term
    o_ref[...] = x_ref[...] + jnp.dot(merged.astype(BF16), wout_ref[...], preferred_element_type=F32)


def _merge(x, ya, yb, yc, yd, proj, gate_block, wbr, wout, tm):
    t, d = x.shape
    mixw = ya.shape[1]
    nbr = wbr.shape[0]
    y_spec = pl.BlockSpec((tm, mixw), lambda i: (i, 0))
    return pl.pallas_call(
        _merge_kernel,
        grid=(t // tm,),
        in_specs=[pl.BlockSpec((tm, d), lambda i: (i, 0)), y_spec, y_spec, y_spec, y_spec,
                  pl.BlockSpec((tm, nbr * d), lambda i: (i, gate_block)),
                  pl.BlockSpec(wbr.shape, lambda i: (0, 0, 0)),
                  pl.BlockSpec(wout.shape, lambda i: (0, 0))],
        out_specs=pl.BlockSpec((tm, d), lambda i: (i, 0)),
        out_shape=jax.ShapeDtypeStruct((t, d), F32),
        compiler_params=_cparams(1),
        name="merge_out_proj",
    )(x, ya, yb, yc, yd, proj, wbr, wout)


def _ffn_kernel(x_ref, g_ref, wg_ref, wu_ref, wd_ref, fin_ref, o_ref, h_ref, acc_ref, *, final_norm):
    f = pl.program_id(1)

    @pl.when(f == 0)
    def _():
        h_ref[...] = _rms_rows(x_ref[...], g_ref[...]).astype(h_ref.dtype)
        acc_ref[...] = jnp.zeros(acc_ref.shape, F32)

    h = h_ref[...]
    a = jnp.dot(h, wg_ref[...], preferred_element_type=F32)
    b = jnp.dot(h, wu_ref[...], preferred_element_type=F32)
    t = (a * _sigmoid(a)) * b
    acc_ref[...] += jnp.dot(t.astype(BF16), wd_ref[...], preferred_element_type=F32)

    @pl.when(f == pl.num_programs(1) - 1)
    def _():
        y = x_ref[...] + acc_ref[...]
        o_ref[...] = _rms_rows(y, fin_ref[...]) if final_norm else y


def _ffn(x, g, wg, wu, wd, fin, final_norm, tm, tf):
    t, d = x.shape
    ff = wg.shape[1]
    return pl.pallas_call(
        functools.partial(_ffn_kernel, final_norm=final_norm),
        grid=(t // tm, ff // tf),
        in_specs=[pl.BlockSpec((tm, d), lambda i, f: (i, 0)),
                  pl.BlockSpec((1, d), lambda i, f: (0, 0)),
                  pl.BlockSpec((d, tf), lambda i, f: (0, f)),
                  pl.BlockSpec((d, tf), lambda i, f: (0, f)),
                  pl.BlockSpec((tf, d), lambda i, f: (f, 0)),
                  pl.BlockSpec((1, d), lambda i, f: (0, 0))],
        out_specs=pl.BlockSpec((tm, d), lambda i, f: (i, 0)),
        out_shape=jax.ShapeDtypeStruct((t, d), F32),
        scratch_shapes=[pltpu.VMEM((tm, d), BF16), pltpu.VMEM((tm, d), F32)],
        compiler_params=_cparams(2),
        name="dense_swiglu",
    )(x, g, wg, wu, wd, fin)


def _router_kernel(x_ref, g_ref, r_ref, h_ref, mi_ref, mp_ref, cnt_ref, tri_ref, carry_ref, *, n_experts):
    i = pl.program_id(0)
    tm = x_ref.shape[0]
    lane = lax.broadcasted_iota(jnp.int32, (tm, LANES), 1)

    @pl.when(i == 0)
    def _():
        carry_ref[...] = jnp.zeros(carry_ref.shape, F32)
        r_i = lax.broadcasted_iota(jnp.int32, (tm, tm), 0)
        c_i = lax.broadcasted_iota(jnp.int32, (tm, tm), 1)
        tri_ref[...] = jnp.where(c_i < r_i, 1.0, 0.0).astype(tri_ref.dtype)

    hf = _rms_rows(x_ref[...], g_ref[...])
    h_ref[...] = hf
    logits = jnp.dot(hf, r_ref[...], preferred_element_type=F32, precision=lax.Precision.HIGHEST)
    logits = jnp.where(lane < n_experts, logits, -jnp.inf)
    m1 = jnp.max(logits, axis=-1, keepdims=True)
    i1 = jnp.min(jnp.where(logits == m1, lane, LANES), axis=-1, keepdims=True)
    rest = jnp.where(lane == i1, -jnp.inf, logits)
    m2 = jnp.max(rest, axis=-1, keepdims=True)
    i2 = jnp.min(jnp.where(rest == m2, lane, LANES), axis=-1, keepdims=True)
    e2 = jnp.exp(m2 - m1)
    p1 = 1.0 / (1.0 + e2)
    p2 = e2 / (1.0 + e2)
    oh1 = lane == i1
    oh2 = lane == i2
    ohs = jnp.where(oh1, 1.0, jnp.where(oh2, 1.0, 0.0))
    pref = jnp.dot(tri_ref[...], ohs.astype(tri_ref.dtype), preferred_element_type=F32) + carry_ref[...]
    rank1 = jnp.sum(jnp.where(oh1, pref, 0.0), axis=-1, keepdims=True).astype(jnp.int32)
    rank2 = jnp.sum(jnp.where(oh2, pref, 0.0), axis=-1, keepdims=True).astype(jnp.int32)
    carry_ref[...] = carry_ref[...] + jnp.sum(ohs, axis=0, keepdims=True)
    cnt_ref[...] = carry_ref[...]
    mi_ref[...] = jnp.where(lane == 0, i1, jnp.where(lane == 1, i2, jnp.where(lane == 2, rank1,
                                                                          jnp.where(lane == 3, rank2, 0))))
    mp_ref[...] = jnp.where(lane == 0, p1, jnp.where(lane == 1, p2, 0.0))


def _router(x, g, router, tm):
    t, d = x.shape
    n_experts = router.shape[1]
    rpad = jnp.zeros((d, LANES), F32).at[:, :n_experts].set(router.astype(F32))
    return pl.pallas_call(
        functools.partial(_router_kernel, n_experts=n_experts),
        grid=(t // tm,),
        in_specs=[pl.BlockSpec((tm, d), lambda i: (i, 0)),
                  pl.BlockSpec((1, d), lambda i: (0, 0)),
                  pl.BlockSpec((d, LANES), lambda i: (0, 0))],
        out_specs=[pl.BlockSpec((tm, d), lambda i: (i, 0)),
                   pl.BlockSpec((tm, LANES), lambda i: (i, 0)),
                   pl.BlockSpec((tm, LANES), lambda i: (i, 0)),
                   pl.BlockSpec((1, LANES), lambda i: (0, 0))],
        out_shape=[jax.ShapeDtypeStruct((t, d), F32),
                   jax.ShapeDtypeStruct((t, LANES), jnp.int32),
                   jax.ShapeDtypeStruct((t, LANES), F32),
                   jax.ShapeDtypeStruct((1, LANES), F32)],
        scratch_shapes=[pltpu.VMEM((tm, tm), BF16), pltpu.VMEM((1, LANES), F32)],
        compiler_params=_cparams(1),
        name="moe_router",
    )(x, g, rpad)


def _row_copy(src, dst, sem):
    return pltpu.make_async_copy(src, dst, sem)


def _dispatch_kernel(dest_ref, h_ref, xs_in_ref, xs_ref, sem, *, n_tokens):
    del xs_in_ref
    td = h_ref.shape[0]
    base = pl.program_id(0) * td

    def issue(r, carry):
        for k in range(TOP_K):
            d = dest_ref[k * n_tokens + base + r]
            _row_copy(h_ref.at[pl.ds(r, 1), :], xs_ref.at[pl.ds(d, 1), :], sem).start()
        return carry

    def drain(r, carry):
        for k in range(TOP_K):
            _row_copy(h_ref.at[pl.ds(0, 1), :], xs_ref.at[pl.ds(0, 1), :], sem).wait()
        return carry

    lax.fori_loop(0, td, issue, 0)
    lax.fori_loop(0, td, drain, 0)


def _dispatch(dest, h, n_rows, td):
    t, d = h.shape
    xs0 = jnp.zeros((n_rows, d), F32)
    return pl.pallas_call(
        functools.partial(_dispatch_kernel, n_tokens=t),
        grid_spec=pltpu.PrefetchScalarGridSpec(
            num_scalar_prefetch=1,
            grid=(t // td,),
            in_specs=[pl.BlockSpec((td, d), lambda i, dest: (i, 0)),
                      pl.BlockSpec(memory_space=pl.ANY)],
            out_specs=pl.BlockSpec(memory_space=pl.ANY),
            scratch_shapes=[pltpu.SemaphoreType.DMA(())]),
        out_shape=jax.ShapeDtypeStruct((n_rows, d), F32),
        input_output_aliases={2: 0},
        compiler_params=_cparams(1),
        name="moe_dispatch",
    )(dest, h, xs0)


def _expert_kernel(te_ref, nu_ref, xs_ref, wg_ref, wu_ref, wd_ref, ys_ref, h_ref, acc_ref):
    i = pl.program_id(0)
    f = pl.program_id(1)
    last_f = pl.num_programs(1) - 1
    valid = i < nu_ref[0]

    @pl.when(valid & (f == 0))
    def _():
        h_ref[...] = xs_ref[...].astype(h_ref.dtype)
        acc_ref[...] = jnp.zeros(acc_ref.shape, F32)

    @pl.when(valid)
    def _():
        h = h_ref[...]
        a = jnp.dot(h, wg_ref[...], preferred_element_type=F32)
        b = jnp.dot(h, wu_ref[...], preferred_element_type=F32)
        t = (a * _sigmoid(a)) * b
        acc_ref[...] += jnp.dot(t.astype(BF16), wd_ref[...], preferred_element_type=F32)

    @pl.when(valid & (f == last_f))
    def _():
        ys_ref[...] = acc_ref[...]

    @pl.when(jnp.logical_not(valid) & (f == last_f))
    def _():
        ys_ref[...] = jnp.zeros(ys_ref.shape, F32)


def _expert_mlp(tile_expert, n_used, xs, wg, wu, wd, tg, tf):
    n_rows, d = xs.shape
    ff = wg.shape[-1]
    nf = ff // tf
    fblk = lambda i, f, te, nu: jnp.where(i < nu[0], f, nf - 1)
    return pl.pallas_call(
        _expert_kernel,
        grid_spec=pltpu.PrefetchScalarGridSpec(
            num_scalar_prefetch=2,
            grid=(n_rows // tg, nf),
            in_specs=[pl.BlockSpec((tg, d), lambda i, f, te, nu: (i, 0)),
                      pl.BlockSpec((None, d, tf), lambda i, f, te, nu: (te[i], 0, fblk(i, f, te, nu))),
                      pl.BlockSpec((None, d, tf), lambda i, f, te, nu: (te[i], 0, fblk(i, f, te, nu))),
                      pl.BlockSpec((None, tf, d), lambda i, f, te, nu: (te[i], fblk(i, f, te, nu), 0))],
            out_specs=pl.BlockSpec((tg, d), lambda i, f, te, nu: (i, 0)),
            scratch_shapes=[pltpu.VMEM((tg, d), BF16), pltpu.VMEM((tg, d), F32)]),
        out_shape=jax.ShapeDtypeStruct((n_rows, d), F32),
        compiler_params=_cparams(2),
        name="moe_experts",
    )(tile_expert, n_used, xs, wg, wu, wd)


def _combine_kernel(dest_ref, x_ref, mp_ref, fin_ref, ys_ref, o_ref, buf_ref, sem, *, n_tokens, final_norm):
    tc = x_ref.shape[0]
    base = pl.program_id(0) * tc

    def issue(r, carry):
        for k in range(TOP_K):
            d = dest_ref[k * n_tokens + base + r]
            _row_copy(ys_ref.at[pl.ds(d, 1), :], buf_ref.at[k, pl.ds(r, 1), :], sem).start()
        return carry

    def drain(r, carry):
        for k in range(TOP_K):
            _row_copy(ys_ref.at[pl.ds(0, 1), :], buf_ref.at[k, pl.ds(0, 1), :], sem).wait()
        return carry

    lax.fori_loop(0, tc, issue, 0)
    lax.fori_loop(0, tc, drain, 0)
    y = x_ref[...]
    for k in range(TOP_K):
        y = y + mp_ref[:, k:k + 1] * buf_ref[k]
    o_ref[...] = _rms_rows(y, fin_ref[...]) if final_norm else y


def _combine(dest, x, mp, fin, ys, final_norm, tc):
    t, d = x.shape
    return pl.pallas_call(
        functools.partial(_combine_kernel, n_tokens=t, final_norm=final_norm),
        grid_spec=pltpu.PrefetchScalarGridSpec(
            num_scalar_prefetch=1,
            grid=(t // tc,),
            in_specs=[pl.BlockSpec((tc, d), lambda i, dest: (i, 0)),
                      pl.BlockSpec((tc, LANES), lambda i, dest: (i, 0)),
                      pl.BlockSpec((1, d), lambda i, dest: (0, 0)),
                      pl.BlockSpec(memory_space=pl.ANY)],
            out_specs=pl.BlockSpec((tc, d), lambda i, dest: (i, 0)),
            scratch_shapes=[pltpu.VMEM((TOP_K, tc, d), F32), pltpu.SemaphoreType.DMA(())]),
        out_shape=jax.ShapeDtypeStruct((t, d), F32),
        compiler_params=_cparams(1),
        name="moe_combine",
    )(dest, x, mp, fin, ys)


def _moe(x, g, router, wg, wu, wd, fin, final_norm, tm, tg, tf):
    t, d = x.shape
    n_experts = router.shape[1]
    h, mi, mp, cnt = _router(x, g, router, tm)
    counts = cnt[0, :n_experts].astype(jnp.int32)
    padded = ((counts + tg - 1) // tg) * tg
    ends = jnp.cumsum(padded)
    offs = ends - padded
    dest = jnp.concatenate([offs[mi[:, k]] + mi[:, TOP_K + k] for k in range(TOP_K)]).astype(jnp.int32)
    n_tiles = (TOP_K * t) // tg + n_experts
    n_used = (ends[-1] // tg).astype(jnp.int32)
    tidx = jnp.minimum(jnp.arange(n_tiles, dtype=jnp.int32), n_used - 1)
    tile_expert = jnp.sum((tidx[:, None] * tg >= ends[None, :]).astype(jnp.int32), axis=1)
    xs = _dispatch(dest, h, n_tiles * tg, _pick_tile(t, 512))
    ys = _expert_mlp(tile_expert, n_used.reshape(1), xs, wg, wu, wd, tg, tf)
    return _combine(dest, x, mp, fin, ys, final_norm, _pick_tile(t, 512))


def _pack_w_in(w_in, d_model, mixw):
    qw = ATT_HEADS * HEAD_DIM
    kvw = 2 * ATT_KV_HEADS * HEAD_DIM
    qiw = IDX_HEADS * IDX_DIM
    bcw = 2 * SSM_GROUPS * SSM_STATE
    sizes = (mixw, qw, kvw, qiw, IDX_DIM, IDX_HEADS, 2 * mixw, mixw, mixw + bcw, SSM_HEADS, 4 * d_model)
    offs = np.concatenate([[0], np.cumsum(sizes)])
    assert offs[-1] == w_in.shape[1]
    seg = lambda i: w_in[:, offs[i]:offs[i + 1]]
    u_pool, q, kv, qi, ki, wi, u_conv, z, xbc, dt, gates = (seg(i) for i in range(len(sizes)))
    d = w_in.shape[0]
    zeros = lambda n: jnp.zeros((d, n), w_in.dtype)
    idx_blk = jnp.concatenate([qi, ki, wi, zeros(3 * LANES - qiw - IDX_DIM - IDX_HEADS),
                               dt, zeros(LANES - SSM_HEADS)], axis=1)
    packed = jnp.concatenate([gates, u_conv, u_pool, q, z, xbc[:, :mixw], idx_blk, kv, xbc[:, mixw:]], axis=1)
    widths = dict(gates=4 * d_model, conv=2 * mixw, pool=mixw, q=qw, z=mixw, xs=mixw, idx=4 * LANES,
                  kv=kvw, bc=bcw)
    blocks, off = {}, 0
    for name in ("gates", "conv", "pool", "q", "z", "xs", "idx", "kv", "bc"):
        assert off % widths[name] == 0
        blocks[name] = off // widths[name]
        off += widths[name]
    blocks["dt"] = (blocks["idx"] * 4 * LANES + 3 * LANES) // LANES
    return packed.astype(BF16), blocks


def _pad_lanes(v):
    return jnp.zeros((1, LANES), F32).at[0, :v.shape[0]].set(v.astype(F32))


def _pick_tile(n, target):
    t = min(n, target)
    while n % t:
        t //= 2
    return t


def _pick_lane_tile(n, target):
    best = None
    for m in range(LANES, min(n, target) + 1, LANES):
        if n % m == 0:
            best = m
    return n if best is None else best


def kernel(x, norm_mix, w_in, pool_w, pool_scale, conv_dw, conv_b, conv_ln_g, conv_ln_b, ssm_conv_w, ssm_conv_b, ssm_dt_bias, ssm_a_log, ssm_d, ssm_norm, w_br, w_out, norm_ffn, ffn_w_gate, ffn_w_up, ffn_w_down, moe_router, moe_w_gate, moe_w_up, moe_w_down, final_norm):
    bsz, seq, d = x.shape
    depth = norm_mix.shape[0]
    mixw = pool_scale.shape[-1]
    t = bsz * seq
    row = lambda v: v.reshape(1, -1).astype(F32)

    tab_att = _rope_tables(seq, HEAD_DIM, ROPE_DIM, LANES // HEAD_DIM)
    tab_idx = _rope_tables(seq, IDX_DIM, IDX_ROPE_DIM, LANES // IDX_DIM)
    tab_tail = _rope_tables(seq, IDX_DIM, IDX_ROPE_DIM, 1)

    tm = _pick_tile(t, 1024)
    ts_seq = _pick_tile(seq, 512)
    fin = row(final_norm)

    xf = x.reshape(t, d)
    for layer in range(depth):
        w_packed, blk = _pack_w_in(w_in[layer], d, mixw)
        proj = _norm_matmul(xf, row(norm_mix[layer]), w_packed, tm, _pick_tile(w_packed.shape[1], 1024))

        y_a = _pool_mixer(proj, blk["pool"], pool_w[layer].astype(BF16), row(pool_scale[layer]), bsz, seq, ts_seq)
        qt, k_r, vt, qit, ki_r, wt = _rope_prep(proj, blk["q"], blk["kv"], blk["idx"], tab_att, tab_idx,
                                                tab_tail, bsz, seq, ts_seq)
        y_b = _dsa_attention(qt, qit, wt, k_r, ki_r, vt, bsz, seq)
        y_c = _conformer(proj, blk["conv"], conv_dw[layer], row(conv_b[layer]), row(conv_ln_g[layer]),
                         row(conv_ln_b[layer]), bsz, seq, _pick_tile(seq, 256))
        y_d = _ssd(proj, blk["z"], blk["xs"], blk["bc"], blk["dt"], ssm_conv_w[layer], row(ssm_conv_b[layer]),
                   _pad_lanes(ssm_dt_bias[layer]), _pad_lanes(-jnp.exp(ssm_a_log[layer].astype(F32))),
                   _pad_lanes(ssm_d[layer]), row(ssm_norm[layer]), bsz, seq)
        xf = _merge(xf, y_a, y_b, y_c, y_d, proj, blk["gates"], w_br[layer].astype(BF16),
                    w_out[layer].astype(BF16), _pick_tile(t, 512))

        last = layer == depth - 1
        jj = layer // 2
        if layer % 2 == 0:
            ff = ffn_w_gate.shape[-1]
            xf = _ffn(xf, row(norm_ffn[layer]), ffn_w_gate[jj].astype(BF16), ffn_w_up[jj].astype(BF16),
                      ffn_w_down[jj].astype(BF16), fin, last, _pick_tile(t, 512), _pick_lane_tile(ff, 1408))
        else:
            ff = moe_w_gate.shape[-1]
            xf = _moe(xf, row(norm_ffn[layer]), moe_router[jj], moe_w_gate[jj].astype(BF16),
                      moe_w_up[jj].astype(BF16), moe_w_down[jj].astype(BF16), fin, last, tm,
                      _pick_tile(t, 1024), _pick_lane_tile(ff, 896))
    if depth == 0:
        raise ValueError("depth must be positive")
    return xf.reshape(bsz, seq, d)
```

```python
import functools

import numpy as np
import jax
import jax.numpy as jnp
from jax import lax
from jax.experimental import pallas as pl
from jax.experimental.pallas import tpu as pltpu

POOL_WINDOWS = (2, 4, 8, 16)
ATT_HEADS = 8
ATT_KV_HEADS = 2
HEAD_DIM = 64
ROPE_DIM = HEAD_DIM // 4
ROPE_THETA = 500000.0
IDX_HEADS = 8
IDX_DIM = 32
IDX_ROPE_DIM = IDX_DIM // 4
TOPK_MAX = 256
SSM_HEADS = 8
SSM_HEAD_DIM = 64
SSM_GROUPS = 2
SSM_STATE = 64
SSM_CHUNK = 128
TOP_K = 2
NORM_EPS = 1e-6

LANES = 128
SUBLANES = 8
VMEM_LIMIT_BYTES = 56 * 1024 * 1024

F32 = jnp.float32
BF16 = jnp.bfloat16
INT_MIN = -(2 ** 31)
NEG_BIG = -1e30


def _cparams(n_axes):
    return pltpu.CompilerParams(dimension_semantics=("arbitrary",) * n_axes,
                                vmem_limit_bytes=VMEM_LIMIT_BYTES)


def _sigmoid(x):
    return 1.0 / (1.0 + jnp.exp(-x))


def _rms_rows(x, g):
    return x * lax.rsqrt(jnp.mean(x * x, axis=-1, keepdims=True) + NORM_EPS) * g


def _norm_matmul_kernel(x_ref, g_ref, w_ref, o_ref, aux_ref, h_ref, *, aux_j, aux_off):
    @pl.when(pl.program_id(1) == 0)
    def _():
        h_ref[...] = _rms_rows(x_ref[...], g_ref[...]).astype(h_ref.dtype)

    res = jnp.dot(h_ref[...], w_ref[...], preferred_element_type=F32)
    o_ref[...] = res.astype(o_ref.dtype)

    @pl.when(pl.program_id(1) == aux_j)
    def _():
        aux_ref[...] = res[:, aux_off:aux_off + aux_ref.shape[1]]


def _norm_matmul(x, g, w, tm, tn, aux_col, aux_w):
    t, d = x.shape
    n = w.shape[1]
    assert aux_col // tn == (aux_col + aux_w - 1) // tn
    kern = functools.partial(_norm_matmul_kernel, aux_j=aux_col // tn, aux_off=aux_col % tn)
    return pl.pallas_call(
        kern,
        grid=(t // tm, n // tn),
        in_specs=[pl.BlockSpec((tm, d), lambda i, j: (i, 0)),
                  pl.BlockSpec((1, d), lambda i, j: (0, 0)),
                  pl.BlockSpec((d, tn), lambda i, j: (0, j))],
        out_specs=[pl.BlockSpec((tm, tn), lambda i, j: (i, j)),
                   pl.BlockSpec((tm, aux_w), lambda i, j: (i, 0))],
        out_shape=[jax.ShapeDtypeStruct((t, n), BF16), jax.ShapeDtypeStruct((t, aux_w), F32)],
        scratch_shapes=[pltpu.VMEM((tm, d), BF16)],
        compiler_params=_cparams(2),
        name="norm_in_proj",
    )(x, g, w)


POOL_HIST = 16


def _pool_kernel(u_ref, w_ref, sc_ref, o_ref, buf_ref):
    s = pl.program_id(1)
    ts = u_ref.shape[0]
    gw = w_ref.shape[1]

    @pl.when(s == 0)
    def _():
        buf_ref[0:POOL_HIST, :] = jnp.zeros((POOL_HIST, buf_ref.shape[1]), F32)

    @pl.when(s > 0)
    def _():
        buf_ref[0:POOL_HIST, :] = buf_ref[ts:ts + POOL_HIST, :]

    buf_ref[POOL_HIST:POOL_HIST + ts, :] = u_ref[...].astype(F32)
    pos = s * ts + lax.broadcasted_iota(jnp.int32, (ts, 1), 0)
    for g, win in enumerate(POOL_WINDOWS):
        cols = slice(g * gw, (g + 1) * gw)
        acc = buf_ref[POOL_HIST:POOL_HIST + ts, cols]
        cur = acc
        for k in range(1, win):
            acc = acc + buf_ref[POOL_HIST - k:POOL_HIST - k + ts, cols]
        cnt = jnp.minimum(pos + 1, win).astype(F32)
        p = acc / cnt - cur
        y = jnp.dot(p.astype(BF16), w_ref[g], preferred_element_type=F32)
        o_ref[:, cols] = (y * sc_ref[:, cols]).astype(o_ref.dtype)


def _pool_mixer(proj, col_block, w, scale, bsz, seq, ts):
    mixw = scale.shape[-1]
    nblk = seq // ts
    return pl.pallas_call(
        _pool_kernel,
        grid=(bsz, nblk),
        in_specs=[pl.BlockSpec((ts, mixw), lambda b, s: (b * nblk + s, col_block)),
                  pl.BlockSpec(w.shape, lambda b, s: (0, 0, 0)),
                  pl.BlockSpec((1, mixw), lambda b, s: (0, 0))],
        out_specs=pl.BlockSpec((ts, mixw), lambda b, s: (b * nblk + s, 0)),
        out_shape=jax.ShapeDtypeStruct((bsz * seq, mixw), BF16),
        scratch_shapes=[pltpu.VMEM((POOL_HIST + ts, mixw), F32)],
        compiler_params=_cparams(2),
        name="pool_mixer",
    )(proj, w, scale)


CONV_HIST = 32


def _conformer_kernel(u_ref, dw_ref, db_ref, lg_ref, lb_ref, o_ref, buf_ref, sh_ref, tmp_ref):
    s = pl.program_id(1)
    ts = u_ref.shape[0]
    c = o_ref.shape[1]
    width = dw_ref.shape[0]

    @pl.when(s == 0)
    def _():
        buf_ref[0:CONV_HIST, :] = jnp.zeros((CONV_HIST, c), F32)

    @pl.when(s > 0)
    def _():
        buf_ref[0:CONV_HIST, :] = buf_ref[ts:ts + CONV_HIST, :]

    a = u_ref[:, 0:c].astype(F32)
    gt = u_ref[:, c:2 * c].astype(F32)
    buf_ref[CONV_HIST:CONV_HIST + ts, :] = a * _sigmoid(gt)
    nsh = CONV_HIST + ts - SUBLANES
    for sft in range(1, SUBLANES):
        sh_ref[sft - 1, 0:nsh, :] = buf_ref[sft:sft + nsh, :]
    rc = min(ts, LANES)
    for r0 in range(0, ts, rc):
        for c0 in range(0, c, LANES):
            acc = jnp.zeros((rc, LANES), F32) + db_ref[:, c0:c0 + LANES]
            for k in range(width):
                off = CONV_HIST - (width - 1) + k + r0
                sft = off % SUBLANES
                if sft == 0:
                    tap = buf_ref[off:off + rc, c0:c0 + LANES]
                else:
                    tap = sh_ref[sft - 1, off - sft:off - sft + rc, c0:c0 + LANES]
                acc = acc + tap * dw_ref[k:k + 1, c0:c0 + LANES]
            tmp_ref[r0:r0 + rc, c0:c0 + LANES] = acc
    acc = tmp_ref[...]
    mu = jnp.mean(acc, axis=-1, keepdims=True)
    xc = acc - mu
    y = xc * lax.rsqrt(jnp.mean(xc * xc, axis=-1, keepdims=True) + NORM_EPS)
    y = y * lg_ref[...] + lb_ref[...]
    o_ref[...] = (y * _sigmoid(y)).astype(o_ref.dtype)


def _conformer(proj, col_block, dw, db, lg, lb, bsz, seq, ts):
    c = dw.shape[1]
    nblk = seq // ts
    return pl.pallas_call(
        _conformer_kernel,
        grid=(bsz, nblk),
        in_specs=[pl.BlockSpec((ts, 2 * c), lambda b, s: (b * nblk + s, col_block)),
                  pl.BlockSpec(dw.shape, lambda b, s: (0, 0)),
                  pl.BlockSpec((1, c), lambda b, s: (0, 0)),
                  pl.BlockSpec((1, c), lambda b, s: (0, 0)),
                  pl.BlockSpec((1, c), lambda b, s: (0, 0))],
        out_specs=pl.BlockSpec((ts, c), lambda b, s: (b * nblk + s, 0)),
        out_shape=jax.ShapeDtypeStruct((bsz * seq, c), BF16),
        scratch_shapes=[pltpu.VMEM((CONV_HIST + ts, c), F32),
                        pltpu.VMEM((SUBLANES - 1, CONV_HIST + ts, c), F32),
                        pltpu.VMEM((ts, c), F32)],
        compiler_params=_cparams(2),
        name="conformer_conv",
    )(proj, dw, db, lg, lb)


SSM_HIST = 8


def _ssd_kernel(z_ref, xs_ref, bc_ref, dt_ref, cw_ref, cb_ref, dtb_ref, a_ref, dsk_ref, ng_ref,
                o_ref, y_ref, xbuf_ref, bcbuf_ref, state_ref):
    c_idx = pl.program_id(1)
    L = xs_ref.shape[0]
    mixw = xs_ref.shape[1]
    bcw = bc_ref.shape[1]
    width = cw_ref.shape[0]
    hp = SSM_HEAD_DIM
    ns = SSM_STATE
    rep = SSM_HEADS // SSM_GROUPS

    @pl.when(c_idx == 0)
    def _():
        xbuf_ref[0:SSM_HIST, :] = jnp.zeros((SSM_HIST, mixw), F32)
        bcbuf_ref[0:SSM_HIST, :] = jnp.zeros((SSM_HIST, bcw), F32)
        state_ref[...] = jnp.zeros(state_ref.shape, F32)

    @pl.when(c_idx > 0)
    def _():
        xbuf_ref[0:SSM_HIST, :] = xbuf_ref[L:L + SSM_HIST, :]
        bcbuf_ref[0:SSM_HIST, :] = bcbuf_ref[L:L + SSM_HIST, :]

    xbuf_ref[SSM_HIST:SSM_HIST + L, :] = xs_ref[...].astype(F32)
    bcbuf_ref[SSM_HIST:SSM_HIST + L, :] = bc_ref[...].astype(F32)

    xc = jnp.zeros((L, mixw), F32) + cb_ref[:, 0:mixw]
    bcc = jnp.zeros((L, bcw), F32) + cb_ref[:, mixw:mixw + bcw]
    for k in range(width):
        off = SSM_HIST - (width - 1) + k
        xc = xc + xbuf_ref[off:off + L, :] * cw_ref[k:k + 1, 0:mixw]
        bcc = bcc + bcbuf_ref[off:off + L, :] * cw_ref[k:k + 1, mixw:mixw + bcw]
    xc = xc * _sigmoid(xc)
    bcc = bcc * _sigmoid(bcc)

    dtr = dt_ref[...] + dtb_ref[...]
    dt = jnp.maximum(dtr, 0.0) + jnp.log(1.0 + jnp.exp(-jnp.abs(dtr)))
    da = dt * a_ref[...]
    row_i = lax.broadcasted_iota(jnp.int32, (L, L), 0)
    col_i = lax.broadcasted_iota(jnp.int32, (L, L), 1)
    tri = col_i <= row_i
    a_cum = jnp.dot(tri.astype(F32), da, preferred_element_type=F32,
                    precision=lax.Precision.HIGHEST)
    a_cum_t = a_cum.T
    a_last = a_cum[L - 1:L, :]
    dec_last = jnp.exp(a_last)
    dec_out = jnp.exp(a_cum)
    dec_st = jnp.exp(a_last - a_cum)

    cbs = []
    for g in range(SSM_GROUPS):
        bg = bcc[:, g * ns:(g + 1) * ns].astype(BF16)
        cg = bcc[:, SSM_GROUPS * ns + g * ns:SSM_GROUPS * ns + (g + 1) * ns].astype(BF16)
        cbs.append(lax.dot_general(cg, bg, (((1,), (1,)), ((), ())), preferred_element_type=F32))

    for h in range(SSM_HEADS):
        g = h // rep
        x_h = xc[:, h * hp:(h + 1) * hp]
        b_g = bcc[:, g * ns:(g + 1) * ns]
        c_g = bcc[:, SSM_GROUPS * ns + g * ns:SSM_GROUPS * ns + (g + 1) * ns]
        xd = x_h * dt[:, h:h + 1]
        seg = a_cum[:, h:h + 1] - a_cum_t[h:h + 1, :]
        lmat = jnp.exp(jnp.where(tri, seg, -jnp.inf))
        y_diag = jnp.dot((cbs[g] * lmat).astype(BF16), xd.astype(BF16), preferred_element_type=F32)
        st = state_ref[h]
        y_off = jnp.dot(c_g.astype(BF16), st.astype(BF16), preferred_element_type=F32) * dec_out[:, h:h + 1]
        bd = (b_g * dec_st[:, h:h + 1]).astype(BF16)
        st_new = lax.dot_general(bd, xd.astype(BF16), (((0,), (0,)), ((), ())),
                                 preferred_element_type=F32)
        state_ref[h] = st * dec_last[:, h:h + 1] + st_new
        y_h = y_diag + y_off + dsk_ref[:, h:h + 1] * x_h
        zz = z_ref[:, h * hp:(h + 1) * hp].astype(F32)
        y_ref[:, h * hp:(h + 1) * hp] = y_h * (zz * _sigmoid(zz))

    o_ref[...] = _rms_rows(y_ref[...], ng_ref[...]).astype(o_ref.dtype)


def _ssd(proj, aux, z_block, xs_block, bc_block, dt_block, cw, cb, dtb, a_neg, dsk, ng, bsz, seq):
    mixw = ng.shape[-1]
    bcw = cw.shape[1] - mixw
    L = SSM_CHUNK
    nblk = seq // L
    row = lambda b, s: b * nblk + s
    return pl.pallas_call(
        _ssd_kernel,
        grid=(bsz, nblk),
        in_specs=[pl.BlockSpec((L, mixw), lambda b, s: (row(b, s), z_block)),
                  pl.BlockSpec((L, mixw), lambda b, s: (row(b, s), xs_block)),
                  pl.BlockSpec((L, bcw), lambda b, s: (row(b, s), bc_block)),
                  pl.BlockSpec((L, LANES), lambda b, s: (row(b, s), dt_block)),
                  pl.BlockSpec(cw.shape, lambda b, s: (0, 0)),
                  pl.BlockSpec(cb.shape, lambda b, s: (0, 0)),
                  pl.BlockSpec((1, LANES), lambda b, s: (0, 0)),
                  pl.BlockSpec((1, LANES), lambda b, s: (0, 0)),
                  pl.BlockSpec((1, LANES), lambda b, s: (0, 0)),
                  pl.BlockSpec((1, mixw), lambda b, s: (0, 0))],
        out_specs=pl.BlockSpec((L, mixw), lambda b, s: (row(b, s), 0)),
        out_shape=jax.ShapeDtypeStruct((bsz * seq, mixw), BF16),
        scratch_shapes=[pltpu.VMEM((L, mixw), F32),
                        pltpu.VMEM((SSM_HIST + L, mixw), F32),
                        pltpu.VMEM((SSM_HIST + L, bcw), F32),
                        pltpu.VMEM((SSM_HEADS, SSM_STATE, SSM_HEAD_DIM), F32)],
        compiler_params=_cparams(2),
        name="ssd_mixer",
    )(proj, proj, proj, aux, cw, cb, dtb, a_neg, dsk, ng)


def _rope_cols(x, c, s1, s2, half):
    return (x * c + pltpu.roll(x, LANES - half, axis=1) * s1 + pltpu.roll(x, half, axis=1) * s2)


LOG2E = 1.4426950408889634


def _rope_prep_kernel(q_ref, kv_ref, idx_ref, ta_ref, ti_ref, tt_ref,
                      qg_ref, k_ref, vt_ref, qig_ref, ki_ref, wg_ref):
    hd = HEAD_DIM
    ts = q_ref.shape[0]
    rep = ATT_HEADS // ATT_KV_HEADS
    scale = (hd ** -0.5) * LOG2E
    ca, s1a, s2a = ta_ref[0], ta_ref[1], ta_ref[2]
    heads_per_col = LANES // hd
    for c in range(q_ref.shape[1] // LANES):
        qc = _rope_cols(q_ref[:, c * LANES:(c + 1) * LANES].astype(F32), ca, s1a, s2a, ROPE_DIM // 2) * scale
        qct = qc.T.astype(qg_ref.dtype)
        for r in range(heads_per_col):
            h = c * heads_per_col + r
            g, hh = h // rep, h % rep
            for jb in range(ts // LANES):
                dst = jb * rep * LANES + hh * LANES
                qg_ref[g, :, dst:dst + LANES] = qct[r * hd:(r + 1) * hd, jb * LANES:(jb + 1) * LANES]
    kw = ATT_KV_HEADS * hd
    kc = _rope_cols(kv_ref[:, 0:kw].astype(F32), ca, s1a, s2a, ROPE_DIM // 2)
    for g in range(ATT_KV_HEADS):
        k_ref[g] = kc[:, g * hd:(g + 1) * hd].astype(k_ref.dtype)
    vt_ref[...] = kv_ref[:, kw:2 * kw].astype(F32).T.astype(vt_ref.dtype)
    ci, s1i, s2i = ti_ref[0], ti_ref[1], ti_ref[2]
    nq = IDX_HEADS * IDX_DIM
    iheads_per_col = LANES // IDX_DIM
    for c in range(nq // LANES):
        qc = _rope_cols(idx_ref[:, c * LANES:(c + 1) * LANES], ci, s1i, s2i, IDX_ROPE_DIM // 2)
        qct = qc.T.astype(qig_ref.dtype)
        for r in range(iheads_per_col):
            h = c * iheads_per_col + r
            for jb in range(ts // LANES):
                dst = jb * IDX_HEADS * LANES + h * LANES
                qig_ref[:, dst:dst + LANES] = qct[r * IDX_DIM:(r + 1) * IDX_DIM, jb * LANES:(jb + 1) * LANES]
    tail = _rope_cols(idx_ref[:, nq:nq + LANES], tt_ref[0], tt_ref[1], tt_ref[2], IDX_ROPE_DIM // 2)
    ki_ref[...] = tail[:, 0:IDX_DIM].astype(ki_ref.dtype)
    tail_t = tail.T
    for h in range(IDX_HEADS):
        for jb in range(ts // LANES):
            dst = jb * IDX_HEADS * LANES + h * LANES
            wg_ref[:, dst:dst + LANES] = tail_t[IDX_DIM + h:IDX_DIM + h + 1, jb * LANES:(jb + 1) * LANES]


def _rope_prep(proj, aux, q_block, kv_block, tab_att, tab_idx, tab_tail, bsz, seq, ts):
    nblk = seq // ts
    qw = ATT_HEADS * HEAD_DIM
    kw = ATT_KV_HEADS * HEAD_DIM
    idxw = 4 * LANES
    rep = ATT_HEADS // ATT_KV_HEADS
    row = lambda b, s: b * nblk + s
    tab_spec = pl.BlockSpec((3, ts, LANES), lambda b, s: (0, s, 0))
    return pl.pallas_call(
        _rope_prep_kernel,
        grid=(bsz, nblk),
        in_specs=[pl.BlockSpec((ts, qw), lambda b, s: (row(b, s), q_block)),
                  pl.BlockSpec((ts, 2 * kw), lambda b, s: (row(b, s), kv_block)),
                  pl.BlockSpec((ts, idxw), lambda b, s: (row(b, s), 0)),
                  tab_spec, tab_spec, tab_spec],
        out_specs=[pl.BlockSpec((None, ATT_KV_HEADS, HEAD_DIM, rep * ts), lambda b, s: (b, 0, 0, s)),
                   pl.BlockSpec((None, ATT_KV_HEADS, ts, HEAD_DIM), lambda b, s: (b, 0, s, 0)),
                   pl.BlockSpec((None, kw, ts), lambda b, s: (b, 0, s)),
                   pl.BlockSpec((None, IDX_DIM, IDX_HEADS * ts), lambda b, s: (b, 0, s)),
                   pl.BlockSpec((None, ts, IDX_DIM), lambda b, s: (b, s, 0)),
                   pl.BlockSpec((None, 1, IDX_HEADS * ts), lambda b, s: (b, 0, s))],
        out_shape=[jax.ShapeDtypeStruct((bsz, ATT_KV_HEADS, HEAD_DIM, rep * seq), BF16),
                   jax.ShapeDtypeStruct((bsz, ATT_KV_HEADS, seq, HEAD_DIM), BF16),
                   jax.ShapeDtypeStruct((bsz, kw, seq), BF16),
                   jax.ShapeDtypeStruct((bsz, IDX_DIM, IDX_HEADS * seq), BF16),
                   jax.ShapeDtypeStruct((bsz, seq, IDX_DIM), BF16),
                   jax.ShapeDtypeStruct((bsz, 1, IDX_HEADS * seq), F32)],
        compiler_params=_cparams(2),
        name="rope_prep",
    )(proj, proj, aux, tab_att, tab_idx, tab_tail)


def _rope_tables(seq, head_dim, rot_dim, n_rot_heads):
    half = rot_dim // 2
    inv_freq = jnp.power(jnp.float32(ROPE_THETA), -jnp.arange(half, dtype=F32) * (2.0 / rot_dim))
    ang = jnp.arange(seq, dtype=F32)[:, None] * inv_freq[None, :]
    cos, sin = jnp.cos(ang), jnp.sin(ang)
    lane = np.arange(LANES)
    j = lane % head_dim
    rot_head = lane < n_rot_heads * head_dim
    first = rot_head & (j < half)
    second = rot_head & (j >= half) & (j < rot_dim)
    fi = np.where(first, j, 0)
    si = np.where(second, j - half, 0)
    c = jnp.where(first[None, :], cos[:, fi], jnp.where(second[None, :], cos[:, si], 1.0))
    s1 = jnp.where(first[None, :], -sin[:, fi], 0.0)
    s2 = jnp.where(second[None, :], sin[:, si], 0.0)
    return jnp.stack([c, s1, s2]).astype(F32)


def _sortable_key(score):
    score = jnp.where(score == 0.0, 0.0, score)
    bits = pltpu.bitcast(score, jnp.int32)
    return bits ^ (lax.shift_right_arithmetic(bits, 31) & jnp.int32(0x7FFFFFFF))


KEY_CHUNK = 256
COUNT_ROWS = 4 * SUBLANES


def _dsa_kernel(qg_ref, qig_ref, wg_ref, k_ref, ki_ref, vt_ref, o_ref, keys_ref, acc_ref, m_ref, s_ref,
                sa_ref, la_ref,
                *, top_k, seq_bits):
    j = pl.program_id(1)
    qb = LANES
    kc = KEY_CHUNK
    hd = HEAD_DIM
    rep = ATT_HEADS // ATT_KV_HEADS
    n_kc = lax.div(j * qb + (qb + kc - 1), kc)
    sub_i = lax.broadcasted_iota(jnp.int32, (kc, qb), 0)
    q_pos = j * qb + lax.broadcasted_iota(jnp.int32, (kc, qb), 1)

    n_pair = lax.div(j * qb + (qb + 2 * kc - 1), 2 * kc)
    last_chunk = 2 * n_pair - 1

    def idx_matmul(c):
        start = pl.multiple_of(c * kc, kc)
        return jnp.dot(ki_ref[pl.ds(start, kc), :], qig_ref[...], preferred_element_type=F32)

    def idx_keys(c, s_all):
        start = pl.multiple_of(c * kc, kc)
        score = jnp.zeros((kc, qb), F32)
        for h in range(IDX_HEADS):
            score = score + jnp.maximum(s_all[:, h * qb:(h + 1) * qb], 0.0) * wg_ref[:, h * qb:(h + 1) * qb]
        causal = (start + sub_i) <= q_pos
        keys_ref[pl.ds(start, kc), :] = jnp.where(causal, _sortable_key(score), jnp.int32(INT_MIN))

    sa_ref[...] = idx_matmul(0)

    def score_pair(p, carry):
        s_b = idx_matmul(2 * p + 1)
        idx_keys(2 * p, sa_ref[...])
        sa_ref[...] = idx_matmul(jnp.minimum(2 * p + 2, last_chunk))
        idx_keys(2 * p + 1, s_b)
        return carry

    lax.fori_loop(0, n_pair, score_pair, 0)

    def count_where(pred_fn):
        def body(c, acc):
            start = pl.multiple_of(c * kc, kc)
            ones = jnp.where(pred_fn(keys_ref[pl.ds(start, kc), :], start), 1.0, 0.0)
            return acc + jnp.sum(ones.reshape(kc // COUNT_ROWS, COUNT_ROWS, qb), axis=0)
        acc = lax.fori_loop(0, n_kc, body, jnp.zeros((COUNT_ROWS, qb), F32))
        return jnp.sum(acc, axis=0, keepdims=True)

    def bit_body(i, tau):
        cand = tau ^ lax.shift_left(jnp.int32(1), 31 - i)
        cnt = count_where(lambda kk, _: kk >= cand)
        return jnp.where(cnt >= top_k, cand, tau)

    tau = lax.fori_loop(0, 32, bit_body, jnp.full((1, qb), INT_MIN, jnp.int32))
    tau = jnp.maximum(tau, jnp.int32(INT_MIN + 1))

    n_gt = count_where(lambda kk, _: kk > tau)
    n_eq = count_where(lambda kk, _: kk == tau)
    need = top_k - n_gt
    has_excess = jnp.max(jnp.where(n_eq > need, 1.0, 0.0)) > 0.5

    @pl.when(has_excess)
    def _():
        def pos_body(i, xcut):
            cand = xcut | lax.shift_left(jnp.int32(1), seq_bits - 1 - i)
            cnt = count_where(lambda kk, st: (kk == tau) & ((st + sub_i) < cand))
            return jnp.where(cnt < need, cand, xcut)

        xcut = lax.fori_loop(0, seq_bits, pos_body, jnp.zeros((1, qb), jnp.int32))
        def fix_body(c, carry):
            start = pl.multiple_of(c * kc, kc)
            kk = keys_ref[pl.ds(start, kc), :]
            drop = (kk == tau) & (((start + sub_i) > xcut) | (need < 1))
            keys_ref[pl.ds(start, kc), :] = jnp.where(drop, jnp.int32(INT_MIN), kk)
            return carry

        lax.fori_loop(0, n_kc, fix_body, 0)

    m_ref[...] = jnp.full(m_ref.shape, NEG_BIG, F32)
    s_ref[...] = jnp.zeros(s_ref.shape, F32)
    acc_ref[...] = jnp.zeros(acc_ref.shape, F32)

    def qk_logits(c):
        start = pl.multiple_of(c * kc, kc)
        return [jnp.dot(k_ref[g, pl.ds(start, kc), :], qg_ref[g], preferred_element_type=F32)
                for g in range(ATT_KV_HEADS)]

    def softmax_pv(c, lgs):
        start = pl.multiple_of(c * kc, kc)
        bias = jnp.where(keys_ref[pl.ds(start, kc), :] >= tau, 0.0, NEG_BIG)
        for g in range(ATT_KV_HEADS):
            ps, alphas = [], []
            for hh in range(rep):
                h = g * rep + hh
                logit = lgs[g][:, hh * qb:(hh + 1) * qb] + bias
                m_old = m_ref[h:h + 1, :]
                m_new = jnp.maximum(m_old, jnp.max(logit, axis=0, keepdims=True))
                p = jnp.exp2(logit - m_new)
                alpha = jnp.exp2(m_old - m_new)
                s_ref[h:h + 1, :] = s_ref[h:h + 1, :] * alpha + jnp.sum(p, axis=0, keepdims=True)
                m_ref[h:h + 1, :] = m_new
                ps.append(p.astype(BF16))
                alphas.append(alpha)
            pv = jnp.dot(vt_ref[g * hd:(g + 1) * hd, pl.ds(start, kc)], jnp.concatenate(ps, axis=1),
                         preferred_element_type=F32)
            for hh in range(rep):
                rows = slice((g * rep + hh) * hd, (g * rep + hh + 1) * hd)
                acc_ref[rows, :] = acc_ref[rows, :] * alphas[hh] + pv[:, hh * qb:(hh + 1) * qb]

    for g, lg in enumerate(qk_logits(0)):
        la_ref[g] = lg

    def att_pair(p, carry):
        l_b = qk_logits(2 * p + 1)
        softmax_pv(2 * p, [la_ref[g] for g in range(ATT_KV_HEADS)])
        for g, lg in enumerate(qk_logits(jnp.minimum(2 * p + 2, last_chunk))):
            la_ref[g] = lg
        softmax_pv(2 * p + 1, l_b)
        return carry

    lax.fori_loop(0, n_pair, att_pair, 0)
    for h in range(ATT_HEADS):
        rows = slice(h * hd, (h + 1) * hd)
        acc_ref[rows, :] = acc_ref[rows, :] / s_ref[h:h + 1, :]
    for c0 in range(0, ATT_HEADS * hd, LANES):
        o_ref[:, c0:c0 + LANES] = acc_ref[c0:c0 + LANES, :].T.astype(o_ref.dtype)


def _dsa_attention(qg, qig, wg, k, ki, vt, bsz, seq):
    qb = LANES
    nblk = seq // qb
    qw = ATT_HEADS * HEAD_DIM
    kw = ATT_KV_HEADS * HEAD_DIM
    rep = ATT_HEADS // ATT_KV_HEADS
    top_k = min(TOPK_MAX, seq // 4)
    seq_bits = int(np.log2(seq))
    assert 2 ** seq_bits == seq and seq % (2 * KEY_CHUNK) == 0
    kern = functools.partial(_dsa_kernel, top_k=top_k, seq_bits=seq_bits)
    return pl.pallas_call(
        kern,
        grid=(bsz, nblk),
        in_specs=[pl.BlockSpec((None, ATT_KV_HEADS, HEAD_DIM, rep * qb), lambda b, j: (b, 0, 0, j)),
                  pl.BlockSpec((None, IDX_DIM, IDX_HEADS * qb), lambda b, j: (b, 0, j)),
                  pl.BlockSpec((None, 1, IDX_HEADS * qb), lambda b, j: (b, 0, j)),
                  pl.BlockSpec((None, ATT_KV_HEADS, seq, HEAD_DIM), lambda b, j: (b, 0, 0, 0)),
                  pl.BlockSpec((None, seq, IDX_DIM), lambda b, j: (b, 0, 0)),
                  pl.BlockSpec((None, kw, seq), lambda b, j: (b, 0, 0))],
        out_specs=pl.BlockSpec((qb, qw), lambda b, j: (b * nblk + j, 0)),
        out_shape=jax.ShapeDtypeStruct((bsz * seq, qw), BF16),
        scratch_shapes=[pltpu.VMEM((seq, qb), jnp.int32), pltpu.VMEM((qw, qb), F32),
                        pltpu.VMEM((ATT_HEADS, qb), F32), pltpu.VMEM((ATT_HEADS, qb), F32),
                        pltpu.VMEM((KEY_CHUNK, IDX_HEADS * qb), F32),
                        pltpu.VMEM((ATT_KV_HEADS, KEY_CHUNK, rep * qb), F32)],
        compiler_params=_cparams(2),
        name="dsa_attention",
    )(qg, qig, wg, k, ki, vt)


def _merge_kernel(x_ref, ya_ref, yb_ref, yc_ref, yd_ref, gl_ref, wbr_ref, wout_ref, o_ref):
    d = x_ref.shape[1]
    merged = None
    for n, y_ref in enumerate((ya_ref, yb_ref, yc_ref, yd_ref)):
        u = jnp.dot(y_ref[...], wbr_ref[n], preferred_element_type=F32)
        term = _sigmoid(gl_ref[:, n * d:(n + 1) * d].astype(F32)) * u
        merged = term if merged is None else merged + term
    o_ref[...] = x_ref[...] + jnp.dot(merged.astype(BF16), wout_ref[...], preferred_element_type=F32)


def _merge(x, ya, yb, yc, yd, proj, gate_block, wbr, wout, tm):
    t, d = x.shape
    mixw = ya.shape[1]
    nbr = wbr.shape[0]
    y_spec = pl.BlockSpec((tm, mixw), lambda i: (i, 0))
    return pl.pallas_call(
        _merge_kernel,
        grid=(t // tm,),
        in_specs=[pl.BlockSpec((tm, d), lambda i: (i, 0)), y_spec, y_spec, y_spec, y_spec,
                  pl.BlockSpec((tm, nbr * d), lambda i: (i, gate_block)),
                  pl.BlockSpec(wbr.shape, lambda i: (0, 0, 0)),
                  pl.BlockSpec(wout.shape, lambda i: (0, 0))],
        out_specs=pl.BlockSpec((tm, d), lambda i: (i, 0)),
        out_shape=jax.ShapeDtypeStruct((t, d), F32),
        compiler_params=_cparams(1),
        name="merge_out_proj",
    )(x, ya, yb, yc, yd, proj, wbr, wout)


def _ffn_kernel(x_ref, g_ref, wg_ref, wu_ref, wd_ref, fin_ref, o_ref, h_ref, acc_ref, *, final_norm):
    f = pl.program_id(1)

    @pl.when(f == 0)
    def _():
        h_ref[...] = _rms_rows(x_ref[...], g_ref[...]).astype(h_ref.dtype)
        acc_ref[...] = jnp.zeros(acc_ref.shape, F32)

    h = h_ref[...]
    a = jnp.dot(h, wg_ref[...], preferred_element_type=F32)
    b = jnp.dot(h, wu_ref[...], preferred_element_type=F32)
    t = (a * _sigmoid(a)) * b
    acc_ref[...] += jnp.dot(t.astype(BF16), wd_ref[...], preferred_element_type=F32)

    @pl.when(f == pl.num_programs(1) - 1)
    def _():
        y = x_ref[...] + acc_ref[...]
        o_ref[...] = _rms_rows(y, fin_ref[...]) if final_norm else y


def _ffn(x, g, wg, wu, wd, fin, final_norm, tm, tf):
    t, d = x.shape
    ff = wg.shape[1]
    return pl.pallas_call(
        functools.partial(_ffn_kernel, final_norm=final_norm),
        grid=(t // tm, ff // tf),
        in_specs=[pl.BlockSpec((tm, d), lambda i, f: (i, 0)),
                  pl.BlockSpec((1, d), lambda i, f: (0, 0)),
                  pl.BlockSpec((d, tf), lambda i, f: (0, f)),
                  pl.BlockSpec((d, tf), lambda i, f: (0, f)),
                  pl.BlockSpec((tf, d), lambda i, f: (f, 0)),
                  pl.BlockSpec((1, d), lambda i, f: (0, 0))],
        out_specs=pl.BlockSpec((tm, d), lambda i, f: (i, 0)),
        out_shape=jax.ShapeDtypeStruct((t, d), F32),
        scratch_shapes=[pltpu.VMEM((tm, d), BF16), pltpu.VMEM((tm, d), F32)],
        compiler_params=_cparams(2),
        name="dense_swiglu",
    )(x, g, wg, wu, wd, fin)


def _router_kernel(x_ref, g_ref, r_ref, h_ref, mi_ref, mp_ref, cnt_ref, tri_ref, carry_ref, *, n_experts):
    i = pl.program_id(0)
    tm = x_ref.shape[0]
    lane = lax.broadcasted_iota(jnp.int32, (tm, LANES), 1)

    @pl.when(i == 0)
    def _():
        carry_ref[...] = jnp.zeros(carry_ref.shape, F32)
        r_i = lax.broadcasted_iota(jnp.int32, (tm, tm), 0)
        c_i = lax.broadcasted_iota(jnp.int32, (tm, tm), 1)
        tri_ref[...] = jnp.where(c_i < r_i, 1.0, 0.0).astype(tri_ref.dtype)

    hf = _rms_rows(x_ref[...], g_ref[...])
    h_ref[...] = hf
    logits = jnp.dot(hf, r_ref[...], preferred_element_type=F32, precision=lax.Precision.HIGHEST)
    logits = jnp.where(lane < n_experts, logits, -jnp.inf)
    m1 = jnp.max(logits, axis=-1, keepdims=True)
    i1 = jnp.min(jnp.where(logits == m1, lane, LANES), axis=-1, keepdims=True)
    rest = jnp.where(lane == i1, -jnp.inf, logits)
    m2 = jnp.max(rest, axis=-1, keepdims=True)
    i2 = jnp.min(jnp.where(rest == m2, lane, LANES), axis=-1, keepdims=True)
    e2 = jnp.exp(m2 - m1)
    p1 = 1.0 / (1.0 + e2)
    p2 = e2 / (1.0 + e2)
    oh1 = lane == i1
    oh2 = lane == i2
    ohs = jnp.where(oh1, 1.0, jnp.where(oh2, 1.0, 0.0))
    pref = jnp.dot(tri_ref[...], ohs.astype(tri_ref.dtype), preferred_element_type=F32) + carry_ref[...]
    rank1 = jnp.sum(jnp.where(oh1, pref, 0.0), axis=-1, keepdims=True).astype(jnp.int32)
    rank2 = jnp.sum(jnp.where(oh2, pref, 0.0), axis=-1, keepdims=True).astype(jnp.int32)
    carry_ref[...] = carry_ref[...] + jnp.sum(ohs, axis=0, keepdims=True)
    cnt_ref[...] = carry_ref[...]
    mi_ref[...] = jnp.where(lane == 0, i1, jnp.where(lane == 1, i2, jnp.where(lane == 2, rank1,
                                                                          jnp.where(lane == 3, rank2, 0))))
    mp_ref[...] = jnp.where(lane == 0, p1, jnp.where(lane == 1, p2, 0.0))


def _router(x, g, router, tm):
    t, d = x.shape
    n_experts = router.shape[1]
    rpad = jnp.zeros((d, LANES), F32).at[:, :n_experts].set(router.astype(F32))
    return pl.pallas_call(
        functools.partial(_router_kernel, n_experts=n_experts),
        grid=(t // tm,),
        in_specs=[pl.BlockSpec((tm, d), lambda i: (i, 0)),
                  pl.BlockSpec((1, d), lambda i: (0, 0)),
                  pl.BlockSpec((d, LANES), lambda i: (0, 0))],
        out_specs=[pl.BlockSpec((tm, d), lambda i: (i, 0)),
                   pl.BlockSpec((tm, LANES), lambda i: (i, 0)),
                   pl.BlockSpec((tm, LANES), lambda i: (i, 0)),
                   pl.BlockSpec((1, LANES), lambda i: (0, 0))],
        out_shape=[jax.ShapeDtypeStruct((t, d), F32),
                   jax.ShapeDtypeStruct((t, LANES), jnp.int32),
                   jax.ShapeDtypeStruct((t, LANES), F32),
                   jax.ShapeDtypeStruct((1, LANES), F32)],
        scratch_shapes=[pltpu.VMEM((tm, tm), BF16), pltpu.VMEM((1, LANES), F32)],
        compiler_params=_cparams(1),
        name="moe_router",
    )(x, g, rpad)


def _row_copy(src, dst, sem):
    return pltpu.make_async_copy(src, dst, sem)


def _dispatch_kernel(dest_ref, h_ref, xs_in_ref, xs_ref, sem, *, n_tokens):
    del xs_in_ref
    td = h_ref.shape[0]
    base = pl.program_id(0) * td

    def issue(r, carry):
        for k in range(TOP_K):
            d = dest_ref[k * n_tokens + base + r]
            _row_copy(h_ref.at[pl.ds(r, 1), :], xs_ref.at[pl.ds(d, 1), :], sem).start(priority=k % 2)
        return carry

    def drain(r, carry):
        for k in range(TOP_K):
            _row_copy(h_ref.at[pl.ds(0, 1), :], xs_ref.at[pl.ds(0, 1), :], sem).wait()
        return carry

    lax.fori_loop(0, td, issue, 0)
    lax.fori_loop(0, td, drain, 0)


def _dispatch(dest, h, n_rows, td):
    t, d = h.shape
    xs0 = jnp.zeros((n_rows, d), F32)
    return pl.pallas_call(
        functools.partial(_dispatch_kernel, n_tokens=t),
        grid_spec=pltpu.PrefetchScalarGridSpec(
            num_scalar_prefetch=1,
            grid=(t // td,),
            in_specs=[pl.BlockSpec((td, d), lambda i, dest: (i, 0)),
                      pl.BlockSpec(memory_space=pl.ANY)],
            out_specs=pl.BlockSpec(memory_space=pl.ANY),
            scratch_shapes=[pltpu.SemaphoreType.DMA(())]),
        out_shape=jax.ShapeDtypeStruct((n_rows, d), F32),
        input_output_aliases={2: 0},
        compiler_params=_cparams(1),
        name="moe_dispatch",
    )(dest, h, xs0)


def _expert_kernel(te_ref, nu_ref, xs_ref, wg_ref, wu_ref, wd_ref, ys_ref, h_ref, acc_ref):
    i = pl.program_id(0)
    f = pl.program_id(1)
    last_f = pl.num_programs(1) - 1
    valid = i < nu_ref[0]

    @pl.when(valid & (f == 0))
    def _():
        h_ref[...] = xs_ref[...].astype(h_ref.dtype)
        acc_ref[...] = jnp.zeros(acc_ref.shape, F32)

    @pl.when(valid)
    def _():
        h = h_ref[...]
        a = jnp.dot(h, wg_ref[...], preferred_element_type=F32)
        b = jnp.dot(h, wu_ref[...], preferred_element_type=F32)
        t = (a * _sigmoid(a)) * b
        acc_ref[...] += jnp.dot(t.astype(BF16), wd_ref[...], preferred_element_type=F32)

    @pl.when(valid & (f == last_f))
    def _():
        ys_ref[...] = acc_ref[...]

    @pl.when(jnp.logical_not(valid) & (f == last_f))
    def _():
        ys_ref[...] = jnp.zeros(ys_ref.shape, F32)


def _expert_mlp(tile_expert, n_used, xs, wg, wu, wd, tg, tf):
    n_rows, d = xs.shape
    ff = wg.shape[-1]
    nf = ff // tf
    fblk = lambda i, f, te, nu: jnp.where(i < nu[0], f, nf - 1)
    return pl.pallas_call(
        _expert_kernel,
        grid_spec=pltpu.PrefetchScalarGridSpec(
            num_scalar_prefetch=2,
            grid=(n_rows // tg, nf),
            in_specs=[pl.BlockSpec((tg, d), lambda i, f, te, nu: (i, 0)),
                      pl.BlockSpec((None, d, tf), lambda i, f, te, nu: (te[i], 0, fblk(i, f, te, nu))),
                      pl.BlockSpec((None, d, tf), lambda i, f, te, nu: (te[i], 0, fblk(i, f, te, nu))),
                      pl.BlockSpec((None, tf, d), lambda i, f, te, nu: (te[i], fblk(i, f, te, nu), 0))],
            out_specs=pl.BlockSpec((tg, d), lambda i, f, te, nu: (i, 0)),
            scratch_shapes=[pltpu.VMEM((tg, d), BF16), pltpu.VMEM((tg, d), F32)]),
        out_shape=jax.ShapeDtypeStruct((n_rows, d), F32),
        compiler_params=_cparams(2),
        name="moe_experts",
    )(tile_expert, n_used, xs, wg, wu, wd)


def _combine_kernel(dest_ref, x_ref, mp_ref, fin_ref, ys_ref, o_ref, buf_ref, sem, *, n_tokens, final_norm):
    tc = x_ref.shape[0]
    base = pl.program_id(0) * tc

    def issue(r, carry):
        for k in range(TOP_K):
            d = dest_ref[k * n_tokens + base + r]
            _row_copy(ys_ref.at[pl.ds(d, 1), :], buf_ref.at[k, pl.ds(r, 1), :], sem).start(priority=k % 2)
        return carry

    def drain(r, carry):
        for k in range(TOP_K):
            _row_copy(ys_ref.at[pl.ds(0, 1), :], buf_ref.at[k, pl.ds(0, 1), :], sem).wait()
        return carry

    lax.fori_loop(0, tc, issue, 0)
    lax.fori_loop(0, tc, drain, 0)
    y = x_ref[...]
    for k in range(TOP_K):
        y = y + mp_ref[:, k:k + 1] * buf_ref[k]
    o_ref[...] = _rms_rows(y, fin_ref[...]) if final_norm else y


def _combine(dest, x, mp, fin, ys, final_norm, tc):
    t, d = x.shape
    return pl.pallas_call(
        functools.partial(_combine_kernel, n_tokens=t, final_norm=final_norm),
        grid_spec=pltpu.PrefetchScalarGridSpec(
            num_scalar_prefetch=1,
            grid=(t // tc,),
            in_specs=[pl.BlockSpec((tc, d), lambda i, dest: (i, 0)),
                      pl.BlockSpec((tc, LANES), lambda i, dest: (i, 0)),
                      pl.BlockSpec((1, d), lambda i, dest: (0, 0)),
                      pl.BlockSpec(memory_space=pl.ANY)],
            out_specs=pl.BlockSpec((tc, d), lambda i, dest: (i, 0)),
            scratch_shapes=[pltpu.VMEM((TOP_K, tc, d), F32), pltpu.SemaphoreType.DMA(())]),
        out_shape=jax.ShapeDtypeStruct((t, d), F32),
        compiler_params=_cparams(1),
        name="moe_combine",
    )(dest, x, mp, fin, ys)


def _moe(x, g, router, wg, wu, wd, fin, final_norm, tm, tg, tf):
    t, d = x.shape
    n_experts = router.shape[1]
    h, mi, mp, cnt = _router(x, g, router, tm)
    counts = cnt[0, :n_experts].astype(jnp.int32)
    padded = ((counts + tg - 1) // tg) * tg
    ends = jnp.cumsum(padded)
    offs = ends - padded
    dest = jnp.concatenate([offs[mi[:, k]] + mi[:, TOP_K + k] for k in range(TOP_K)]).astype(jnp.int32)
    n_tiles = (TOP_K * t) // tg + n_experts
    n_used = (ends[-1] // tg).astype(jnp.int32)
    tidx = jnp.minimum(jnp.arange(n_tiles, dtype=jnp.int32), n_used - 1)
    tile_expert = jnp.sum((tidx[:, None] * tg >= ends[None, :]).astype(jnp.int32), axis=1)
    xs = _dispatch(dest, h, n_tiles * tg, _pick_tile(t, 512))
    ys = _expert_mlp(tile_expert, n_used.reshape(1), xs, wg, wu, wd, tg, tf)
    return _combine(dest, x, mp, fin, ys, final_norm, _pick_tile(t, 512))


def _pack_w_in(w_in, d_model, mixw):
    qw = ATT_HEADS * HEAD_DIM
    kvw = 2 * ATT_KV_HEADS * HEAD_DIM
    qiw = IDX_HEADS * IDX_DIM
    bcw = 2 * SSM_GROUPS * SSM_STATE
    sizes = (mixw, qw, kvw, qiw, IDX_DIM, IDX_HEADS, 2 * mixw, mixw, mixw + bcw, SSM_HEADS, 4 * d_model)
    offs = np.concatenate([[0], np.cumsum(sizes)])
    assert offs[-1] == w_in.shape[1]
    seg = lambda i: w_in[:, offs[i]:offs[i + 1]]
    u_pool, q, kv, qi, ki, wi, u_conv, z, xbc, dt, gates = (seg(i) for i in range(len(sizes)))
    d = w_in.shape[0]
    zeros = lambda n: jnp.zeros((d, n), w_in.dtype)
    idx_blk = jnp.concatenate([qi, ki, wi, zeros(3 * LANES - qiw - IDX_DIM - IDX_HEADS),
                               dt, zeros(LANES - SSM_HEADS)], axis=1)
    packed = jnp.concatenate([gates, u_conv, u_pool, q, z, xbc[:, :mixw], idx_blk, kv, xbc[:, mixw:]], axis=1)
    widths = dict(gates=4 * d_model, conv=2 * mixw, pool=mixw, q=qw, z=mixw, xs=mixw, idx=4 * LANES,
                  kv=kvw, bc=bcw)
    blocks, off = {}, 0
    for name in ("gates", "conv", "pool", "q", "z", "xs", "idx", "kv", "bc"):
        assert off % widths[name] == 0
        blocks[name] = off // widths[name]
        off += widths[name]
    blocks["dt"] = 3
    return packed.astype(BF16), blocks


def _pad_lanes(v):
    return jnp.zeros((1, LANES), F32).at[0, :v.shape[0]].set(v.astype(F32))


def _pick_tile(n, target):
    t = min(n, target)
    while n % t:
        t //= 2
    return t


def _pick_lane_tile(n, target):
    best = None
    for m in range(LANES, min(n, target) + 1, LANES):
        if n % m == 0:
            best = m
    return n if best is None else best


def kernel(x, norm_mix, w_in, pool_w, pool_scale, conv_dw, conv_b, conv_ln_g, conv_ln_b, ssm_conv_w, ssm_conv_b, ssm_dt_bias, ssm_a_log, ssm_d, ssm_norm, w_br, w_out, norm_ffn, ffn_w_gate, ffn_w_up, ffn_w_down, moe_router, moe_w_gate, moe_w_up, moe_w_down, final_norm):
    bsz, seq, d = x.shape
    depth = norm_mix.shape[0]
    mixw = pool_scale.shape[-1]
    t = bsz * seq
    row = lambda v: v.reshape(1, -1).astype(F32)

    tab_att = _rope_tables(seq, HEAD_DIM, ROPE_DIM, LANES // HEAD_DIM)
    tab_idx = _rope_tables(seq, IDX_DIM, IDX_ROPE_DIM, LANES // IDX_DIM)
    tab_tail = _rope_tables(seq, IDX_DIM, IDX_ROPE_DIM, 1)

    tm = _pick_tile(t, 1024)
    ts_seq = _pick_tile(seq, 512)
    fin = row(final_norm)

    xf = x.reshape(t, d)
    for layer in range(depth):
        w_packed, blk = _pack_w_in(w_in[layer], d, mixw)
        idx_w = 4 * LANES
        proj, aux = _norm_matmul(xf, row(norm_mix[layer]), w_packed, tm, _pick_tile(w_packed.shape[1], 1024),
                                 blk["idx"] * idx_w, idx_w)

        y_a = _pool_mixer(proj, blk["pool"], pool_w[layer].astype(BF16), row(pool_scale[layer]), bsz, seq, ts_seq)
        qt, k_r, vt, qit, ki_r, wt = _rope_prep(proj, aux, blk["q"], blk["kv"], tab_att, tab_idx,
                                                tab_tail, bsz, seq, ts_seq)
        y_b = _dsa_attention(qt, qit, wt, k_r, ki_r, vt, bsz, seq)
        y_c = _conformer(proj, blk["conv"], conv_dw[layer], row(conv_b[layer]), row(conv_ln_g[layer]),
                         row(conv_ln_b[layer]), bsz, seq, _pick_tile(seq, 256))
        y_d = _ssd(proj, aux, blk["z"], blk["xs"], blk["bc"], blk["dt"], ssm_conv_w[layer], row(ssm_conv_b[layer]),
                   _pad_lanes(ssm_dt_bias[layer]), _pad_lanes(-jnp.exp(ssm_a_log[layer].astype(F32))),
                   _pad_lanes(ssm_d[layer]), row(ssm_norm[layer]), bsz, seq)
        xf = _merge(xf, y_a, y_b, y_c, y_d, proj, blk["gates"], w_br[layer].astype(BF16),
                    w_out[layer].astype(BF16), _pick_tile(t, 512))

        last = layer == depth - 1
        jj = layer // 2
        if layer % 2 == 0:
            ff = ffn_w_gate.shape[-1]
            xf = _ffn(xf, row(norm_ffn[layer]), ffn_w_gate[jj].astype(BF16), ffn_w_up[jj].astype(BF16),
                      ffn_w_down[jj].astype(BF16), fin, last, _pick_tile(t, 512), _pick_lane_tile(ff, 1408))
        else:
            ff = moe_w_gate.shape[-1]
            xf = _moe(xf, row(norm_ffn[layer]), moe_router[jj], moe_w_gate[jj].astype(BF16),
                      moe_w_up[jj].astype(BF16), moe_w_down[jj].astype(BF16), fin, last, tm,
                      _pick_tile(t, 1024), _pick_lane_tile(ff, 896))
    if depth == 0:
        raise ValueError("depth must be positive")
    return xf.reshape(bsz, seq, d)
```

```python
import functools

import numpy as np
import jax
import jax.numpy as jnp
from jax import lax
from jax.experimental import pallas as pl
from jax.experimental.pallas import tpu as pltpu

POOL_WINDOWS = (2, 4, 8, 16)
ATT_HEADS = 8
ATT_KV_HEADS = 2
HEAD_DIM = 64
ROPE_DIM = HEAD_DIM // 4
ROPE_THETA = 500000.0
IDX_HEADS = 8
IDX_DIM = 32
IDX_ROPE_DIM = IDX_DIM // 4
TOPK_MAX = 256
SSM_HEADS = 8
SSM_HEAD_DIM = 64
SSM_GROUPS = 2
SSM_STATE = 64
SSM_CHUNK = 128
TOP_K = 2
NORM_EPS = 1e-6

LANES = 128
SUBLANES = 8
VMEM_LIMIT_BYTES = 56 * 1024 * 1024

F32 = jnp.float32
BF16 = jnp.bfloat16
INT_MIN = -(2 ** 31)
NEG_BIG = -1e30


def _cparams(n_axes):
    return pltpu.CompilerParams(dimension_semantics=("arbitrary",) * n_axes,
                                vmem_limit_bytes=VMEM_LIMIT_BYTES)


def _sigmoid(x):
    return 1.0 / (1.0 + jnp.exp(-x))


def _rms_rows(x, g):
    return x * lax.rsqrt(jnp.mean(x * x, axis=-1, keepdims=True) + NORM_EPS) * g


def _norm_matmul_kernel(x_ref, g_ref, w_ref, o_ref, aux_ref, h_ref, *, aux_j, aux_off):
    @pl.when(pl.program_id(1) == 0)
    def _():
        h_ref[...] = _rms_rows(x_ref[...], g_ref[...]).astype(h_ref.dtype)

    res = jnp.dot(h_ref[...], w_ref[...], preferred_element_type=F32)
    o_ref[...] = res.astype(o_ref.dtype)

    @pl.when(pl.program_id(1) == aux_j)
    def _():
        aux_ref[...] = res[:, aux_off:aux_off + aux_ref.shape[1]]


def _norm_matmul(x, g, w, tm, tn, aux_col, aux_w):
    t, d = x.shape
    n = w.shape[1]
    assert aux_col // tn == (aux_col + aux_w - 1) // tn
    kern = functools.partial(_norm_matmul_kernel, aux_j=aux_col // tn, aux_off=aux_col % tn)
    return pl.pallas_call(
        kern,
        grid=(t // tm, n // tn),
        in_specs=[pl.BlockSpec((tm, d), lambda i, j: (i, 0)),
                  pl.BlockSpec((1, d), lambda i, j: (0, 0)),
                  pl.BlockSpec((d, tn), lambda i, j: (0, j))],
        out_specs=[pl.BlockSpec((tm, tn), lambda i, j: (i, j)),
                   pl.BlockSpec((tm, aux_w), lambda i, j: (i, 0))],
        out_shape=[jax.ShapeDtypeStruct((t, n), BF16), jax.ShapeDtypeStruct((t, aux_w), F32)],
        scratch_shapes=[pltpu.VMEM((tm, d), BF16)],
        compiler_params=_cparams(2),
        name="norm_in_proj",
    )(x, g, w)


POOL_HIST = 16


def _pool_kernel(u_ref, w_ref, sc_ref, o_ref, buf_ref):
    s = pl.program_id(1)
    ts = u_ref.shape[0]
    gw = w_ref.shape[1]

    @pl.when(s == 0)
    def _():
        buf_ref[0:POOL_HIST, :] = jnp.zeros((POOL_HIST, buf_ref.shape[1]), F32)

    @pl.when(s > 0)
    def _():
        buf_ref[0:POOL_HIST, :] = buf_ref[ts:ts + POOL_HIST, :]

    buf_ref[POOL_HIST:POOL_HIST + ts, :] = u_ref[...].astype(F32)
    pos = s * ts + lax.broadcasted_iota(jnp.int32, (ts, 1), 0)
    for g, win in enumerate(POOL_WINDOWS):
        cols = slice(g * gw, (g + 1) * gw)
        acc = buf_ref[POOL_HIST:POOL_HIST + ts, cols]
        cur = acc
        for k in range(1, win):
            acc = acc + buf_ref[POOL_HIST - k:POOL_HIST - k + ts, cols]
        cnt = jnp.minimum(pos + 1, win).astype(F32)
        p = acc / cnt - cur
        y = jnp.dot(p.astype(BF16), w_ref[g], preferred_element_type=F32)
        o_ref[:, cols] = (y * sc_ref[:, cols]).astype(o_ref.dtype)


def _pool_mixer(proj, col_block, w, scale, bsz, seq, ts):
    mixw = scale.shape[-1]
    nblk = seq // ts
    return pl.pallas_call(
        _pool_kernel,
        grid=(bsz, nblk),
        in_specs=[pl.BlockSpec((ts, mixw), lambda b, s: (b * nblk + s, col_block)),
                  pl.BlockSpec(w.shape, lambda b, s: (0, 0, 0)),
                  pl.BlockSpec((1, mixw), lambda b, s: (0, 0))],
        out_specs=pl.BlockSpec((ts, mixw), lambda b, s: (b * nblk + s, 0)),
        out_shape=jax.ShapeDtypeStruct((bsz * seq, mixw), BF16),
        scratch_shapes=[pltpu.VMEM((POOL_HIST + ts, mixw), F32)],
        compiler_params=_cparams(2),
        name="pool_mixer",
    )(proj, w, scale)


CONV_HIST = 32


def _conformer_kernel(u_ref, dw_ref, db_ref, lg_ref, lb_ref, o_ref, buf_ref, sh_ref, tmp_ref):
    s = pl.program_id(1)
    ts = u_ref.shape[0]
    c = o_ref.shape[1]
    width = dw_ref.shape[0]

    @pl.when(s == 0)
    def _():
        buf_ref[0:CONV_HIST, :] = jnp.zeros((CONV_HIST, c), F32)

    @pl.when(s > 0)
    def _():
        buf_ref[0:CONV_HIST, :] = buf_ref[ts:ts + CONV_HIST, :]

    a = u_ref[:, 0:c].astype(F32)
    gt = u_ref[:, c:2 * c].astype(F32)
    buf_ref[CONV_HIST:CONV_HIST + ts, :] = a * _sigmoid(gt)
    nsh = CONV_HIST + ts - SUBLANES
    for sft in range(1, SUBLANES):
        sh_ref[sft - 1, 0:nsh, :] = buf_ref[sft:sft + nsh, :]
    rc = min(ts, LANES)
    for r0 in range(0, ts, rc):
        for c0 in range(0, c, LANES):
            acc = jnp.zeros((rc, LANES), F32) + db_ref[:, c0:c0 + LANES]
            for k in range(width):
                off = CONV_HIST - (width - 1) + k + r0
                sft = off % SUBLANES
                if sft == 0:
                    tap = buf_ref[off:off + rc, c0:c0 + LANES]
                else:
                    tap = sh_ref[sft - 1, off - sft:off - sft + rc, c0:c0 + LANES]
                acc = acc + tap * dw_ref[k:k + 1, c0:c0 + LANES]
            tmp_ref[r0:r0 + rc, c0:c0 + LANES] = acc
    acc = tmp_ref[...]
    mu = jnp.mean(acc, axis=-1, keepdims=True)
    xc = acc - mu
    y = xc * lax.rsqrt(jnp.mean(xc * xc, axis=-1, keepdims=True) + NORM_EPS)
    y = y * lg_ref[...] + lb_ref[...]
    o_ref[...] = (y * _sigmoid(y)).astype(o_ref.dtype)


def _conformer(proj, col_block, dw, db, lg, lb, bsz, seq, ts):
    c = dw.shape[1]
    nblk = seq // ts
    return pl.pallas_call(
        _conformer_kernel,
        grid=(bsz, nblk),
        in_specs=[pl.BlockSpec((ts, 2 * c), lambda b, s: (b * nblk + s, col_block)),
                  pl.BlockSpec(dw.shape, lambda b, s: (0, 0)),
                  pl.BlockSpec((1, c), lambda b, s: (0, 0)),
                  pl.BlockSpec((1, c), lambda b, s: (0, 0)),
                  pl.BlockSpec((1, c), lambda b, s: (0, 0))],
        out_specs=pl.BlockSpec((ts, c), lambda b, s: (b * nblk + s, 0)),
        out_shape=jax.ShapeDtypeStruct((bsz * seq, c), BF16),
        scratch_shapes=[pltpu.VMEM((CONV_HIST + ts, c), F32),
                        pltpu.VMEM((SUBLANES - 1, CONV_HIST + ts, c), F32),
                        pltpu.VMEM((ts, c), F32)],
        compiler_params=_cparams(2),
        name="conformer_conv",
    )(proj, dw, db, lg, lb)


SSM_HIST = 8


def _ssd_kernel(z_ref, xs_ref, bc_ref, dt_ref, cw_ref, cb_ref, dtb_ref, a_ref, dsk_ref, ng_ref,
                o_ref, y_ref, xbuf_ref, bcbuf_ref, state_ref):
    c_idx = pl.program_id(1)
    L = xs_ref.shape[0]
    mixw = xs_ref.shape[1]
    bcw = bc_ref.shape[1]
    width = cw_ref.shape[0]
    hp = SSM_HEAD_DIM
    ns = SSM_STATE
    rep = SSM_HEADS // SSM_GROUPS

    @pl.when(c_idx == 0)
    def _():
        xbuf_ref[0:SSM_HIST, :] = jnp.zeros((SSM_HIST, mixw), F32)
        bcbuf_ref[0:SSM_HIST, :] = jnp.zeros((SSM_HIST, bcw), F32)
        state_ref[...] = jnp.zeros(state_ref.shape, F32)

    @pl.when(c_idx > 0)
    def _():
        xbuf_ref[0:SSM_HIST, :] = xbuf_ref[L:L + SSM_HIST, :]
        bcbuf_ref[0:SSM_HIST, :] = bcbuf_ref[L:L + SSM_HIST, :]

    xbuf_ref[SSM_HIST:SSM_HIST + L, :] = xs_ref[...].astype(F32)
    bcbuf_ref[SSM_HIST:SSM_HIST + L, :] = bc_ref[...].astype(F32)

    xc = jnp.zeros((L, mixw), F32) + cb_ref[:, 0:mixw]
    bcc = jnp.zeros((L, bcw), F32) + cb_ref[:, mixw:mixw + bcw]
    for k in range(width):
        off = SSM_HIST - (width - 1) + k
        xc = xc + xbuf_ref[off:off + L, :] * cw_ref[k:k + 1, 0:mixw]
        bcc = bcc + bcbuf_ref[off:off + L, :] * cw_ref[k:k + 1, mixw:mixw + bcw]
    xc = xc * _sigmoid(xc)
    bcc = bcc * _sigmoid(bcc)

    dtr = dt_ref[...] + dtb_ref[...]
    dt = jnp.maximum(dtr, 0.0) + jnp.log(1.0 + jnp.exp(-jnp.abs(dtr)))
    da = dt * a_ref[...]
    row_i = lax.broadcasted_iota(jnp.int32, (L, L), 0)
    col_i = lax.broadcasted_iota(jnp.int32, (L, L), 1)
    tri = col_i <= row_i
    a_cum = jnp.dot(tri.astype(F32), da, preferred_element_type=F32,
                    precision=lax.Precision.HIGHEST)
    a_cum_t = a_cum.T
    a_last = a_cum[L - 1:L, :]
    dec_last = jnp.exp(a_last)
    dec_out = jnp.exp(a_cum)
    dec_st = jnp.exp(a_last - a_cum)

    cbs = []
    for g in range(SSM_GROUPS):
        bg = bcc[:, g * ns:(g + 1) * ns].astype(BF16)
        cg = bcc[:, SSM_GROUPS * ns + g * ns:SSM_GROUPS * ns + (g + 1) * ns].astype(BF16)
        cbs.append(lax.dot_general(cg, bg, (((1,), (1,)), ((), ())), preferred_element_type=F32))

    for h in range(SSM_HEADS):
        g = h // rep
        x_h = xc[:, h * hp:(h + 1) * hp]
        b_g = bcc[:, g * ns:(g + 1) * ns]
        c_g = bcc[:, SSM_GROUPS * ns + g * ns:SSM_GROUPS * ns + (g + 1) * ns]
        xd = x_h * dt[:, h:h + 1]
        seg = a_cum[:, h:h + 1] - a_cum_t[h:h + 1, :]
        lmat = jnp.exp(jnp.where(tri, seg, -jnp.inf))
        y_diag = jnp.dot((cbs[g] * lmat).astype(BF16), xd.astype(BF16), preferred_element_type=F32)
        st = state_ref[h]
        y_off = jnp.dot(c_g.astype(BF16), st.astype(BF16), preferred_element_type=F32) * dec_out[:, h:h + 1]
        bd = (b_g * dec_st[:, h:h + 1]).astype(BF16)
        st_new = lax.dot_general(bd, xd.astype(BF16), (((0,), (0,)), ((), ())),
                                 preferred_element_type=F32)
        state_ref[h] = st * dec_last[:, h:h + 1] + st_new
        y_h = y_diag + y_off + dsk_ref[:, h:h + 1] * x_h
        zz = z_ref[:, h * hp:(h + 1) * hp].astype(F32)
        y_ref[:, h * hp:(h + 1) * hp] = y_h * (zz * _sigmoid(zz))

    o_ref[...] = _rms_rows(y_ref[...], ng_ref[...]).astype(o_ref.dtype)


def _ssd(proj, aux, z_block, xs_block, bc_block, dt_block, cw, cb, dtb, a_neg, dsk, ng, bsz, seq):
    mixw = ng.shape[-1]
    bcw = cw.shape[1] - mixw
    L = SSM_CHUNK
    nblk = seq // L
    row = lambda b, s: b * nblk + s
    return pl.pallas_call(
        _ssd_kernel,
        grid=(bsz, nblk),
        in_specs=[pl.BlockSpec((L, mixw), lambda b, s: (row(b, s), z_block)),
                  pl.BlockSpec((L, mixw), lambda b, s: (row(b, s), xs_block)),
                  pl.BlockSpec((L, bcw), lambda b, s: (row(b, s), bc_block)),
                  pl.BlockSpec((L, LANES), lambda b, s: (row(b, s), dt_block)),
                  pl.BlockSpec(cw.shape, lambda b, s: (0, 0)),
                  pl.BlockSpec(cb.shape, lambda b, s: (0, 0)),
                  pl.BlockSpec((1, LANES), lambda b, s: (0, 0)),
                  pl.BlockSpec((1, LANES), lambda b, s: (0, 0)),
                  pl.BlockSpec((1, LANES), lambda b, s: (0, 0)),
                  pl.BlockSpec((1, mixw), lambda b, s: (0, 0))],
        out_specs=pl.BlockSpec((L, mixw), lambda b, s: (row(b, s), 0)),
        out_shape=jax.ShapeDtypeStruct((bsz * seq, mixw), BF16),
        scratch_shapes=[pltpu.VMEM((L, mixw), F32),
                        pltpu.VMEM((SSM_HIST + L, mixw), F32),
                        pltpu.VMEM((SSM_HIST + L, bcw), F32),
                        pltpu.VMEM((SSM_HEADS, SSM_STATE, SSM_HEAD_DIM), F32)],
        compiler_params=_cparams(2),
        name="ssd_mixer",
    )(proj, proj, proj, aux, cw, cb, dtb, a_neg, dsk, ng)


def _rope_cols(x, c, s1, s2, half):
    return (x * c + pltpu.roll(x, LANES - half, axis=1) * s1 + pltpu.roll(x, half, axis=1) * s2)


LOG2E = 1.4426950408889634


def _rope_prep_kernel(q_ref, kv_ref, idx_ref, ta_ref, ti_ref, tt_ref,
                      qg_ref, k_ref, vt_ref, qig_ref, ki_ref, wg_ref):
    hd = HEAD_DIM
    ts = q_ref.shape[0]
    rep = ATT_HEADS // ATT_KV_HEADS
    scale = (hd ** -0.5) * LOG2E
    ca, s1a, s2a = ta_ref[0], ta_ref[1], ta_ref[2]
    heads_per_col = LANES // hd
    for c in range(q_ref.shape[1] // LANES):
        qc = _rope_cols(q_ref[:, c * LANES:(c + 1) * LANES].astype(F32), ca, s1a, s2a, ROPE_DIM // 2) * scale
        qct = qc.T.astype(qg_ref.dtype)
        for r in range(heads_per_col):
            h = c * heads_per_col + r
            g, hh = h // rep, h % rep
            for jb in range(ts // LANES):
                dst = jb * rep * LANES + hh * LANES
                qg_ref[g, :, dst:dst + LANES] = qct[r * hd:(r + 1) * hd, jb * LANES:(jb + 1) * LANES]
    kw = ATT_KV_HEADS * hd
    kc = _rope_cols(kv_ref[:, 0:kw].astype(F32), ca, s1a, s2a, ROPE_DIM // 2)
    for g in range(ATT_KV_HEADS):
        k_ref[g] = kc[:, g * hd:(g + 1) * hd].astype(k_ref.dtype)
    vt_ref[...] = kv_ref[:, kw:2 * kw].astype(F32).T.astype(vt_ref.dtype)
    ci, s1i, s2i = ti_ref[0], ti_ref[1], ti_ref[2]
    nq = IDX_HEADS * IDX_DIM
    iheads_per_col = LANES // IDX_DIM
    for c in range(nq // LANES):
        qc = _rope_cols(idx_ref[:, c * LANES:(c + 1) * LANES], ci, s1i, s2i, IDX_ROPE_DIM // 2)
        qct = qc.T.astype(qig_ref.dtype)
        for r in range(iheads_per_col):
            h = c * iheads_per_col + r
            for jb in range(ts // LANES):
                dst = jb * IDX_HEADS * LANES + h * LANES
                qig_ref[:, dst:dst + LANES] = qct[r * IDX_DIM:(r + 1) * IDX_DIM, jb * LANES:(jb + 1) * LANES]
    tail = _rope_cols(idx_ref[:, nq:nq + LANES], tt_ref[0], tt_ref[1], tt_ref[2], IDX_ROPE_DIM // 2)
    ki_ref[...] = tail[:, 0:IDX_DIM].astype(ki_ref.dtype)
    tail_t = tail.T
    for h in range(IDX_HEADS):
        for jb in range(ts // LANES):
            dst = jb * IDX_HEADS * LANES + h * LANES
            wg_ref[:, dst:dst + LANES] = tail_t[IDX_DIM + h:IDX_DIM + h + 1, jb * LANES:(jb + 1) * LANES]


def _rope_prep(proj, aux, q_block, kv_block, tab_att, tab_idx, tab_tail, bsz, seq, ts):
    nblk = seq // ts
    qw = ATT_HEADS * HEAD_DIM
    kw = ATT_KV_HEADS * HEAD_DIM
    idxw = 4 * LANES
    rep = ATT_HEADS // ATT_KV_HEADS
    row = lambda b, s: b * nblk + s
    tab_spec = pl.BlockSpec((3, ts, LANES), lambda b, s: (0, s, 0))
    return pl.pallas_call(
        _rope_prep_kernel,
        grid=(bsz, nblk),
        in_specs=[pl.BlockSpec((ts, qw), lambda b, s: (row(b, s), q_block)),
                  pl.BlockSpec((ts, 2 * kw), lambda b, s: (row(b, s), kv_block)),
                  pl.BlockSpec((ts, idxw), lambda b, s: (row(b, s), 0)),
                  tab_spec, tab_spec, tab_spec],
        out_specs=[pl.BlockSpec((None, ATT_KV_HEADS, HEAD_DIM, rep * ts), lambda b, s: (b, 0, 0, s)),
                   pl.BlockSpec((None, ATT_KV_HEADS, ts, HEAD_DIM), lambda b, s: (b, 0, s, 0)),
                   pl.BlockSpec((None, kw, ts), lambda b, s: (b, 0, s)),
                   pl.BlockSpec((None, IDX_DIM, IDX_HEADS * ts), lambda b, s: (b, 0, s)),
                   pl.BlockSpec((None, ts, IDX_DIM), lambda b, s: (b, s, 0)),
                   pl.BlockSpec((None, 1, IDX_HEADS * ts), lambda b, s: (b, 0, s))],
        out_shape=[jax.ShapeDtypeStruct((bsz, ATT_KV_HEADS, HEAD_DIM, rep * seq), BF16),
                   jax.ShapeDtypeStruct((bsz, ATT_KV_HEADS, seq, HEAD_DIM), BF16),
                   jax.ShapeDtypeStruct((bsz, kw, seq), BF16),
                   jax.ShapeDtypeStruct((bsz, IDX_DIM, IDX_HEADS * seq), BF16),
                   jax.ShapeDtypeStruct((bsz, seq, IDX_DIM), BF16),
                   jax.ShapeDtypeStruct((bsz, 1, IDX_HEADS * seq), F32)],
        compiler_params=_cparams(2),
        name="rope_prep",
    )(proj, proj, aux, tab_att, tab_idx, tab_tail)


def _rope_tables(seq, head_dim, rot_dim, n_rot_heads):
    half = rot_dim // 2
    inv_freq = jnp.power(jnp.float32(ROPE_THETA), -jnp.arange(half, dtype=F32) * (2.0 / rot_dim))
    ang = jnp.arange(seq, dtype=F32)[:, None] * inv_freq[None, :]
    cos, sin = jnp.cos(ang), jnp.sin(ang)
    lane = np.arange(LANES)
    j = lane % head_dim
    rot_head = lane < n_rot_heads * head_dim
    first = rot_head & (j < half)
    second = rot_head & (j >= half) & (j < rot_dim)
    fi = np.where(first, j, 0)
    si = np.where(second, j - half, 0)
    c = jnp.where(first[None, :], cos[:, fi], jnp.where(second[None, :], cos[:, si], 1.0))
    s1 = jnp.where(first[None, :], -sin[:, fi], 0.0)
    s2 = jnp.where(second[None, :], sin[:, si], 0.0)
    return jnp.stack([c, s1, s2]).astype(F32)


def _sortable_key(score):
    score = jnp.where(score == 0.0, 0.0, score)
    bits = pltpu.bitcast(score, jnp.int32)
    return bits ^ (lax.shift_right_arithmetic(bits, 31) & jnp.int32(0x7FFFFFFF))


KEY_CHUNK = 256
COUNT_ROWS = 4 * SUBLANES
COUNT16_ROWS = 64
I16_MIN = -(2 ** 15)
I16_MAX = 2 ** 15 - 1


def _dsa_kernel(qg_ref, qig_ref, wg_ref, k_ref, ki_ref, vt_ref, o_ref, keys_ref, hi_ref, lo_ref, acc_ref,
                m_ref, s_ref, sa_ref, la_ref,
                *, top_k, seq_bits):
    j = pl.program_id(1)
    qb = LANES
    kc = KEY_CHUNK
    hd = HEAD_DIM
    rep = ATT_HEADS // ATT_KV_HEADS
    n_kc = lax.div(j * qb + (qb + kc - 1), kc)
    sub_i = lax.broadcasted_iota(jnp.int32, (kc, qb), 0)
    q_pos = j * qb + lax.broadcasted_iota(jnp.int32, (kc, qb), 1)

    n_pair = lax.div(j * qb + (qb + 2 * kc - 1), 2 * kc)
    last_chunk = 2 * n_pair - 1

    def idx_matmul(c):
        start = pl.multiple_of(c * kc, kc)
        return jnp.dot(ki_ref[pl.ds(start, kc), :], qig_ref[...], preferred_element_type=F32)

    def idx_keys(c, s_all):
        start = pl.multiple_of(c * kc, kc)
        score = jnp.zeros((kc, qb), F32)
        for h in range(IDX_HEADS):
            score = score + jnp.maximum(s_all[:, h * qb:(h + 1) * qb], 0.0) * wg_ref[:, h * qb:(h + 1) * qb]
        causal = (start + sub_i) <= q_pos
        key = jnp.where(causal, _sortable_key(score), jnp.int32(INT_MIN))
        keys_ref[pl.ds(start, kc), :] = key
        hi_ref[pl.ds(start, kc), :] = lax.shift_right_arithmetic(key, 16).astype(jnp.int16)
        lo_ref[pl.ds(start, kc), :] = ((key & jnp.int32(0xFFFF)) + jnp.int32(I16_MIN)).astype(jnp.int16)

    sa_ref[...] = idx_matmul(0)

    def score_pair(p, carry):
        s_b = idx_matmul(2 * p + 1)
        idx_keys(2 * p, sa_ref[...])
        sa_ref[...] = idx_matmul(jnp.minimum(2 * p + 2, last_chunk))
        idx_keys(2 * p + 1, s_b)
        return carry

    lax.fori_loop(0, n_pair, score_pair, 0)

    def count_where(pred_fn):
        def body(c, acc):
            start = pl.multiple_of(c * kc, kc)
            ones = jnp.where(pred_fn(keys_ref[pl.ds(start, kc), :], start), jnp.int32(1), jnp.int32(0))
            for r in range(0, kc, COUNT_ROWS):
                acc = acc + ones[r:r + COUNT_ROWS]
            return acc
        acc = lax.fori_loop(0, n_kc, body, jnp.zeros((COUNT_ROWS, qb), jnp.int32))
        return jnp.sum(acc, axis=0, keepdims=True)

    def count16(ref, cand):
        cand16 = cand.astype(jnp.int16)

        def body(p, acc):
            start = pl.multiple_of(p * (2 * kc), 2 * kc)
            ones = jnp.where(ref[pl.ds(start, 2 * kc), :] >= cand16, jnp.int16(1), jnp.int16(0))
            for r in range(0, 2 * kc, COUNT16_ROWS):
                acc = acc + ones[r:r + COUNT16_ROWS]
            return acc
        acc = lax.fori_loop(0, n_pair, body, jnp.zeros((COUNT16_ROWS, qb), jnp.int16))
        return jnp.sum(acc.astype(jnp.int32), axis=0, keepdims=True)

    def search16(ref, need):
        def body(i, t):
            cand = t + lax.shift_left(jnp.int32(1), 15 - i)
            return jnp.where(count16(ref, cand) >= need, cand, t)
        return lax.fori_loop(0, 16, body, jnp.full((1, qb), I16_MIN, jnp.int32))

    t_hi = search16(hi_ref, jnp.int32(top_k))
    n_hi_gt = jnp.where(t_hi == I16_MAX, 0, count16(hi_ref, jnp.minimum(t_hi + 1, I16_MAX)))
    t_hi16 = t_hi.astype(jnp.int16)

    def bucket_body(c, carry):
        start = pl.multiple_of(c * kc, kc)
        in_bucket = hi_ref[pl.ds(start, kc), :] == t_hi16
        lo_ref[pl.ds(start, kc), :] = jnp.where(in_bucket, lo_ref[pl.ds(start, kc), :], jnp.int16(I16_MIN))
        return carry

    lax.fori_loop(0, n_kc, bucket_body, 0)
    t_lo = search16(lo_ref, top_k - n_hi_gt)
    tau = t_hi * 65536 + (t_lo - I16_MIN)
    tau = jnp.maximum(tau, jnp.int32(INT_MIN + 1))

    n_gt = count_where(lambda kk, _: kk > tau)
    n_eq = count_where(lambda kk, _: kk == tau)
    need = top_k - n_gt
    has_excess = jnp.max(jnp.where(n_eq > need, 1.0, 0.0)) > 0.5

    @pl.when(has_excess)
    def _():
        def pos_body(i, xcut):
            cand = xcut | lax.shift_left(jnp.int32(1), seq_bits - 1 - i)
            cnt = count_where(lambda kk, st: (kk == tau) & ((st + sub_i) < cand))
            return jnp.where(cnt < need, cand, xcut)

        xcut = lax.fori_loop(0, seq_bits, pos_body, jnp.zeros((1, qb), jnp.int32))
        def fix_body(c, carry):
            start = pl.multiple_of(c * kc, kc)
            kk = keys_ref[pl.ds(start, kc), :]
            drop = (kk == tau) & (((start + sub_i) > xcut) | (need < 1))
            keys_ref[pl.ds(start, kc), :] = jnp.where(drop, jnp.int32(INT_MIN), kk)
            return carry

        lax.fori_loop(0, n_kc, fix_body, 0)

    m_ref[...] = jnp.full(m_ref.shape, NEG_BIG, F32)
    s_ref[...] = jnp.zeros(s_ref.shape, F32)
    acc_ref[...] = jnp.zeros(acc_ref.shape, F32)

    def qk_logits(c):
        start = pl.multiple_of(c * kc, kc)
        return [jnp.dot(k_ref[g, pl.ds(start, kc), :], qg_ref[g], preferred_element_type=F32)
                for g in range(ATT_KV_HEADS)]

    def softmax_pv(c, lgs):
        start = pl.multiple_of(c * kc, kc)
        bias = jnp.where(keys_ref[pl.ds(start, kc), :] >= tau, 0.0, NEG_BIG)
        for g in range(ATT_KV_HEADS):
            ps, alphas = [], []
            for hh in range(rep):
                h = g * rep + hh
                logit = lgs[g][:, hh * qb:(hh + 1) * qb] + bias
                m_old = m_ref[h:h + 1, :]
                m_new = jnp.maximum(m_old, jnp.max(logit, axis=0, keepdims=True))
                p = jnp.exp2(logit - m_new)
                alpha = jnp.exp2(m_old - m_new)
                s_ref[h:h + 1, :] = s_ref[h:h + 1, :] * alpha + jnp.sum(p, axis=0, keepdims=True)
                m_ref[h:h + 1, :] = m_new
                ps.append(p.astype(BF16))
                alphas.append(alpha)
            pv = jnp.dot(vt_ref[g * hd:(g + 1) * hd, pl.ds(start, kc)], jnp.concatenate(ps, axis=1),
                         preferred_element_type=F32)
            for hh in range(rep):
                rows = slice((g * rep + hh) * hd, (g * rep + hh + 1) * hd)
                acc_ref[rows, :] = acc_ref[rows, :] * alphas[hh] + pv[:, hh * qb:(hh + 1) * qb]

    for g, lg in enumerate(qk_logits(0)):
        la_ref[g] = lg

    def att_pair(p, carry):
        l_b = qk_logits(2 * p + 1)
        softmax_pv(2 * p, [la_ref[g] for g in range(ATT_KV_HEADS)])
        for g, lg in enumerate(qk_logits(jnp.minimum(2 * p + 2, last_chunk))):
            la_ref[g] = lg
        softmax_pv(2 * p + 1, l_b)
        return carry

    lax.fori_loop(0, n_pair, att_pair, 0)
    for h in range(ATT_HEADS):
        rows = slice(h * hd, (h + 1) * hd)
        acc_ref[rows, :] = acc_ref[rows, :] / s_ref[h:h + 1, :]
    for c0 in range(0, ATT_HEADS * hd, LANES):
        o_ref[:, c0:c0 + LANES] = acc_ref[c0:c0 + LANES, :].T.astype(o_ref.dtype)


def _dsa_attention(qg, qig, wg, k, ki, vt, bsz, seq):
    qb = LANES
    nblk = seq // qb
    qw = ATT_HEADS * HEAD_DIM
    kw = ATT_KV_HEADS * HEAD_DIM
    rep = ATT_HEADS // ATT_KV_HEADS
    top_k = min(TOPK_MAX, seq // 4)
    seq_bits = int(np.log2(seq))
    assert 2 ** seq_bits == seq and seq % (2 * KEY_CHUNK) == 0
    assert seq // COUNT16_ROWS <= I16_MAX
    kern = functools.partial(_dsa_kernel, top_k=top_k, seq_bits=seq_bits)
    return pl.pallas_call(
        kern,
        grid=(bsz, nblk),
        in_specs=[pl.BlockSpec((None, ATT_KV_HEADS, HEAD_DIM, rep * qb), lambda b, j: (b, 0, 0, j)),
                  pl.BlockSpec((None, IDX_DIM, IDX_HEADS * qb), lambda b, j: (b, 0, j)),
                  pl.BlockSpec((None, 1, IDX_HEADS * qb), lambda b, j: (b, 0, j)),
                  pl.BlockSpec((None, ATT_KV_HEADS, seq, HEAD_DIM), lambda b, j: (b, 0, 0, 0)),
                  pl.BlockSpec((None, seq, IDX_DIM), lambda b, j: (b, 0, 0)),
                  pl.BlockSpec((None, kw, seq), lambda b, j: (b, 0, 0))],
        out_specs=pl.BlockSpec((qb, qw), lambda b, j: (b * nblk + j, 0)),
        out_shape=jax.ShapeDtypeStruct((bsz * seq, qw), BF16),
        scratch_shapes=[pltpu.VMEM((seq, qb), jnp.int32), pltpu.VMEM((seq, qb), jnp.int16),
                        pltpu.VMEM((seq, qb), jnp.int16), pltpu.VMEM((qw, qb), F32),
                        pltpu.VMEM((ATT_HEADS, qb), F32), pltpu.VMEM((ATT_HEADS, qb), F32),
                        pltpu.VMEM((KEY_CHUNK, IDX_HEADS * qb), F32),
                        pltpu.VMEM((ATT_KV_HEADS, KEY_CHUNK, rep * qb), F32)],
        compiler_params=_cparams(2),
        name="dsa_attention",
    )(qg, qig, wg, k, ki, vt)


def _merge_kernel(x_ref, ya_ref, yb_ref, yc_ref, yd_ref, gl_ref, wbr_ref, wout_ref, o_ref):
    d = x_ref.shape[1]
    merged = None
    for n, y_ref in enumerate((ya_ref, yb_ref, yc_ref, yd_ref)):
        u = jnp.dot(y_ref[...], wbr_ref[n], preferred_element_type=F32)
        term = _sigmoid(gl_ref[:, n * d:(n + 1) * d].astype(F32)) * u
        merged = term if merged is None else merged + term
    o_ref[...] = x_ref[...] + jnp.dot(merged.astype(BF16), wout_ref[...], preferred_element_type=F32)


def _merge(x, ya, yb, yc, yd, proj, gate_block, wbr, wout, tm):
    t, d = x.shape
    mixw = ya.shape[1]
    nbr = wbr.shape[0]
    y_spec = pl.BlockSpec((tm, mixw), lambda i: (i, 0))
    return pl.pallas_call(
        _merge_kernel,
        grid=(t // tm,),
        in_specs=[pl.BlockSpec((tm, d), lambda i: (i, 0)), y_spec, y_spec, y_spec, y_spec,
                  pl.BlockSpec((tm, nbr * d), lambda i: (i, gate_block)),
                  pl.BlockSpec(wbr.shape, lambda i: (0, 0, 0)),
                  pl.BlockSpec(wout.shape, lambda i: (0, 0))],
        out_specs=pl.BlockSpec((tm, d), lambda i: (i, 0)),
        out_shape=jax.ShapeDtypeStruct((t, d), F32),
        compiler_params=_cparams(1),
        name="merge_out_proj",
    )(x, ya, yb, yc, yd, proj, wbr, wout)


def _ffn_kernel(x_ref, g_ref, wg_ref, wu_ref, wd_ref, fin_ref, o_ref, h_ref, acc_ref, *, final_norm):
    f = pl.program_id(1)

    @pl.when(f == 0)
    def _():
        h_ref[...] = _rms_rows(x_ref[...], g_ref[...]).astype(h_ref.dtype)
        acc_ref[...] = jnp.zeros(acc_ref.shape, F32)

    h = h_ref[...]
    a = jnp.dot(h, wg_ref[...], preferred_element_type=F32)
    b = jnp.dot(h, wu_ref[...], preferred_element_type=F32)
    t = (a * _sigmoid(a)) * b
    acc_ref[...] += jnp.dot(t.astype(BF16), wd_ref[...], preferred_element_type=F32)

    @pl.when(f == pl.num_programs(1) - 1)
    def _():
        y = x_ref[...] + acc_ref[...]
        o_ref[...] = _rms_rows(y, fin_ref[...]) if final_norm else y


def _ffn(x, g, wg, wu, wd, fin, final_norm, tm, tf):
    t, d = x.shape
    ff = wg.shape[1]
    return pl.pallas_call(
        functools.partial(_ffn_kernel, final_norm=final_norm),
        grid=(t // tm, ff // tf),
        in_specs=[pl.BlockSpec((tm, d), lambda i, f: (i, 0)),
                  pl.BlockSpec((1, d), lambda i, f: (0, 0)),
                  pl.BlockSpec((d, tf), lambda i, f: (0, f)),
                  pl.BlockSpec((d, tf), lambda i, f: (0, f)),
                  pl.BlockSpec((tf, d), lambda i, f: (f, 0)),
                  pl.BlockSpec((1, d), lambda i, f: (0, 0))],
        out_specs=pl.BlockSpec((tm, d), lambda i, f: (i, 0)),
        out_shape=jax.ShapeDtypeStruct((t, d), F32),
        scratch_shapes=[pltpu.VMEM((tm, d), BF16), pltpu.VMEM((tm, d), F32)],
        compiler_params=_cparams(2),
        name="dense_swiglu",
    )(x, g, wg, wu, wd, fin)


def _router_kernel(x_ref, g_ref, r_ref, h_ref, mi_ref, mp_ref, cnt_ref, tri_ref, carry_ref, *, n_experts):
    i = pl.program_id(0)
    tm = x_ref.shape[0]
    lane = lax.broadcasted_iota(jnp.int32, (tm, LANES), 1)

    @pl.when(i == 0)
    def _():
        carry_ref[...] = jnp.zeros(carry_ref.shape, F32)
        r_i = lax.broadcasted_iota(jnp.int32, (tm, tm), 0)
        c_i = lax.broadcasted_iota(jnp.int32, (tm, tm), 1)
        tri_ref[...] = jnp.where(c_i < r_i, 1.0, 0.0).astype(tri_ref.dtype)

    hf = _rms_rows(x_ref[...], g_ref[...])
    h_ref[...] = hf
    logits = jnp.dot(hf, r_ref[...], preferred_element_type=F32, precision=lax.Precision.HIGHEST)
    logits = jnp.where(lane < n_experts, logits, -jnp.inf)
    m1 = jnp.max(logits, axis=-1, keepdims=True)
    i1 = jnp.min(jnp.where(logits == m1, lane, LANES), axis=-1, keepdims=True)
    rest = jnp.where(lane == i1, -jnp.inf, logits)
    m2 = jnp.max(rest, axis=-1, keepdims=True)
    i2 = jnp.min(jnp.where(rest == m2, lane, LANES), axis=-1, keepdims=True)
    e2 = jnp.exp(m2 - m1)
    p1 = 1.0 / (1.0 + e2)
    p2 = e2 / (1.0 + e2)
    oh1 = lane == i1
    oh2 = lane == i2
    ohs = jnp.where(oh1, 1.0, jnp.where(oh2, 1.0, 0.0))
    pref = jnp.dot(tri_ref[...], ohs.astype(tri_ref.dtype), preferred_element_type=F32) + carry_ref[...]
    rank1 = jnp.sum(jnp.where(oh1, pref, 0.0), axis=-1, keepdims=True).astype(jnp.int32)
    rank2 = jnp.sum(jnp.where(oh2, pref, 0.0), axis=-1, keepdims=True).astype(jnp.int32)
    carry_ref[...] = carry_ref[...] + jnp.sum(ohs, axis=0, keepdims=True)
    cnt_ref[...] = carry_ref[...]
    mi_ref[...] = jnp.where(lane == 0, i1, jnp.where(lane == 1, i2, jnp.where(lane == 2, rank1,
                                                                          jnp.where(lane == 3, rank2, 0))))
    mp_ref[...] = jnp.where(lane == 0, p1, jnp.where(lane == 1, p2, 0.0))


def _router(x, g, router, tm):
    t, d = x.shape
    n_experts = router.shape[1]
    rpad = jnp.zeros((d, LANES), F32).at[:, :n_experts].set(router.astype(F32))
    return pl.pallas_call(
        functools.partial(_router_kernel, n_experts=n_experts),
        grid=(t // tm,),
        in_specs=[pl.BlockSpec((tm, d), lambda i: (i, 0)),
                  pl.BlockSpec((1, d), lambda i: (0, 0)),
                  pl.BlockSpec((d, LANES), lambda i: (0, 0))],
        out_specs=[pl.BlockSpec((tm, d), lambda i: (i, 0)),
                   pl.BlockSpec((tm, LANES), lambda i: (i, 0)),
                   pl.BlockSpec((tm, LANES), lambda i: (i, 0)),
                   pl.BlockSpec((1, LANES), lambda i: (0, 0))],
        out_shape=[jax.ShapeDtypeStruct((t, d), F32),
                   jax.ShapeDtypeStruct((t, LANES), jnp.int32),
                   jax.ShapeDtypeStruct((t, LANES), F32),
                   jax.ShapeDtypeStruct((1, LANES), F32)],
        scratch_shapes=[pltpu.VMEM((tm, tm), BF16), pltpu.VMEM((1, LANES), F32)],
        compiler_params=_cparams(1),
        name="moe_router",
    )(x, g, rpad)


DMA_UNROLL = 8


def _row_copy(src, dst, sem):
    return pltpu.make_async_copy(src, dst, sem)


def _dispatch_kernel(dest_ref, h_ref, xs_in_ref, xs_ref, sem, *, n_tokens):
    del xs_in_ref
    td = h_ref.shape[0]
    base = pl.program_id(0) * td

    def issue(r, carry):
        for k in range(TOP_K):
            d = dest_ref[k * n_tokens + base + r]
            _row_copy(h_ref.at[pl.ds(r, 1), :], xs_ref.at[pl.ds(d, 1), :], sem).start(priority=k % 2)
        return carry

    def drain(r, carry):
        for k in range(TOP_K):
            _row_copy(h_ref.at[pl.ds(0, 1), :], xs_ref.at[pl.ds(0, 1), :], sem).wait()
        return carry

    lax.fori_loop(0, td, issue, 0, unroll=DMA_UNROLL)
    lax.fori_loop(0, td, drain, 0, unroll=DMA_UNROLL)


def _dispatch(dest, h, n_rows, td):
    t, d = h.shape
    xs0 = jnp.zeros((n_rows, d), F32)
    return pl.pallas_call(
        functools.partial(_dispatch_kernel, n_tokens=t),
        grid_spec=pltpu.PrefetchScalarGridSpec(
            num_scalar_prefetch=1,
            grid=(t // td,),
            in_specs=[pl.BlockSpec((td, d), lambda i, dest: (i, 0)),
                      pl.BlockSpec(memory_space=pl.ANY)],
            out_specs=pl.BlockSpec(memory_space=pl.ANY),
            scratch_shapes=[pltpu.SemaphoreType.DMA(())]),
        out_shape=jax.ShapeDtypeStruct((n_rows, d), F32),
        input_output_aliases={2: 0},
        compiler_params=_cparams(1),
        name="moe_dispatch",
    )(dest, h, xs0)


def _expert_kernel(te_ref, nu_ref, xs_ref, wg_ref, wu_ref, wd_ref, ys_ref, h_ref, acc_ref):
    i = pl.program_id(0)
    f = pl.program_id(1)
    last_f = pl.num_programs(1) - 1
    valid = i < nu_ref[0]

    @pl.when(valid & (f == 0))
    def _():
        h_ref[...] = xs_ref[...].astype(h_ref.dtype)
        acc_ref[...] = jnp.zeros(acc_ref.shape, F32)

    @pl.when(valid)
    def _():
        h = h_ref[...]
        a = jnp.dot(h, wg_ref[...], preferred_element_type=F32)
        b = jnp.dot(h, wu_ref[...], preferred_element_type=F32)
        t = (a * _sigmoid(a)) * b
        acc_ref[...] += jnp.dot(t.astype(BF16), wd_ref[...], preferred_element_type=F32)

    @pl.when(valid & (f == last_f))
    def _():
        ys_ref[...] = acc_ref[...]

    @pl.when(jnp.logical_not(valid) & (f == last_f))
    def _():
        ys_ref[...] = jnp.zeros(ys_ref.shape, F32)


def _expert_mlp(tile_expert, n_used, xs, wg, wu, wd, tg, tf):
    n_rows, d = xs.shape
    ff = wg.shape[-1]
    nf = ff // tf
    fblk = lambda i, f, te, nu: jnp.where(i < nu[0], f, nf - 1)
    return pl.pallas_call(
        _expert_kernel,
        grid_spec=pltpu.PrefetchScalarGridSpec(
            num_scalar_prefetch=2,
            grid=(n_rows // tg, nf),
            in_specs=[pl.BlockSpec((tg, d), lambda i, f, te, nu: (i, 0)),
                      pl.BlockSpec((None, d, tf), lambda i, f, te, nu: (te[i], 0, fblk(i, f, te, nu))),
                      pl.BlockSpec((None, d, tf), lambda i, f, te, nu: (te[i], 0, fblk(i, f, te, nu))),
                      pl.BlockSpec((None, tf, d), lambda i, f, te, nu: (te[i], fblk(i, f, te, nu), 0))],
            out_specs=pl.BlockSpec((tg, d), lambda i, f, te, nu: (i, 0)),
            scratch_shapes=[pltpu.VMEM((tg, d), BF16), pltpu.VMEM((tg, d), F32)]),
        out_shape=jax.ShapeDtypeStruct((n_rows, d), F32),
        compiler_params=_cparams(2),
        name="moe_experts",
    )(tile_expert, n_used, xs, wg, wu, wd)


def _combine_kernel(dest_ref, x_ref, mp_ref, fin_ref, ys_ref, o_ref, buf_ref, sem, *, n_tokens, final_norm):
    tc = x_ref.shape[0]
    base = pl.program_id(0) * tc

    def issue(r, carry):
        for k in range(TOP_K):
            d = dest_ref[k * n_tokens + base + r]
            _row_copy(ys_ref.at[pl.ds(d, 1), :], buf_ref.at[k, pl.ds(r, 1), :], sem).start(priority=k % 2)
        return carry

    def drain(r, carry):
        for k in range(TOP_K):
            _row_copy(ys_ref.at[pl.ds(0, 1), :], buf_ref.at[k, pl.ds(0, 1), :], sem).wait()
        return carry

    lax.fori_loop(0, tc, issue, 0, unroll=DMA_UNROLL)
    lax.fori_loop(0, tc, drain, 0, unroll=DMA_UNROLL)
    y = x_ref[...]
    for k in range(TOP_K):
        y = y + mp_ref[:, k:k + 1] * buf_ref[k]
    o_ref[...] = _rms_rows(y, fin_ref[...]) if final_norm else y


def _combine(dest, x, mp, fin, ys, final_norm, tc):
    t, d = x.shape
    return pl.pallas_call(
        functools.partial(_combine_kernel, n_tokens=t, final_norm=final_norm),
        grid_spec=pltpu.PrefetchScalarGridSpec(
            num_scalar_prefetch=1,
            grid=(t // tc,),
            in_specs=[pl.BlockSpec((tc, d), lambda i, dest: (i, 0)),
                      pl.BlockSpec((tc, LANES), lambda i, dest: (i, 0)),
                      pl.BlockSpec((1, d), lambda i, dest: (0, 0)),
                      pl.BlockSpec(memory_space=pl.ANY)],
            out_specs=pl.BlockSpec((tc, d), lambda i, dest: (i, 0)),
            scratch_shapes=[pltpu.VMEM((TOP_K, tc, d), F32), pltpu.SemaphoreType.DMA(())]),
        out_shape=jax.ShapeDtypeStruct((t, d), F32),
        compiler_params=_cparams(1),
        name="moe_combine",
    )(dest, x, mp, fin, ys)


def _moe(x, g, router, wg, wu, wd, fin, final_norm, tm, tg, tf):
    t, d = x.shape
    n_experts = router.shape[1]
    h, mi, mp, cnt = _router(x, g, router, tm)
    counts = cnt[0, :n_experts].astype(jnp.int32)
    padded = ((counts + tg - 1) // tg) * tg
    ends = jnp.cumsum(padded)
    offs = ends - padded
    dest = jnp.concatenate([offs[mi[:, k]] + mi[:, TOP_K + k] for k in range(TOP_K)]).astype(jnp.int32)
    n_tiles = (TOP_K * t) // tg + n_experts
    n_used = (ends[-1] // tg).astype(jnp.int32)
    tidx = jnp.minimum(jnp.arange(n_tiles, dtype=jnp.int32), n_used - 1)
    tile_expert = jnp.sum((tidx[:, None] * tg >= ends[None, :]).astype(jnp.int32), axis=1)
    xs = _dispatch(dest, h, n_tiles * tg, _pick_tile(t, 512))
    ys = _expert_mlp(tile_expert, n_used.reshape(1), xs, wg, wu, wd, tg, tf)
    return _combine(dest, x, mp, fin, ys, final_norm, _pick_tile(t, 512))


def _pack_w_in(w_in, d_model, mixw):
    qw = ATT_HEADS * HEAD_DIM
    kvw = 2 * ATT_KV_HEADS * HEAD_DIM
    qiw = IDX_HEADS * IDX_DIM
    bcw = 2 * SSM_GROUPS * SSM_STATE
    sizes = (mixw, qw, kvw, qiw, IDX_DIM, IDX_HEADS, 2 * mixw, mixw, mixw + bcw, SSM_HEADS, 4 * d_model)
    offs = np.concatenate([[0], np.cumsum(sizes)])
    assert offs[-1] == w_in.shape[1]
    seg = lambda i: w_in[:, offs[i]:offs[i + 1]]
    u_pool, q, kv, qi, ki, wi, u_conv, z, xbc, dt, gates = (seg(i) for i in range(len(sizes)))
    d = w_in.shape[0]
    zeros = lambda n: jnp.zeros((d, n), w_in.dtype)
    idx_blk = jnp.concatenate([qi, ki, wi, zeros(3 * LANES - qiw - IDX_DIM - IDX_HEADS),
                               dt, zeros(LANES - SSM_HEADS)], axis=1)
    packed = jnp.concatenate([gates, u_conv, u_pool, q, z, xbc[:, :mixw], idx_blk, kv, xbc[:, mixw:]], axis=1)
    widths = dict(gates=4 * d_model, conv=2 * mixw, pool=mixw, q=qw, z=mixw, xs=mixw, idx=4 * LANES,
                  kv=kvw, bc=bcw)
    blocks, off = {}, 0
    for name in ("gates", "conv", "pool", "q", "z", "xs", "idx", "kv", "bc"):
        assert off % widths[name] == 0
        blocks[name] = off // widths[name]
        off += widths[name]
    blocks["dt"] = 3
    return packed.astype(BF16), blocks


def _pad_lanes(v):
    return jnp.zeros((1, LANES), F32).at[0, :v.shape[0]].set(v.astype(F32))


def _pick_tile(n, target):
    t = min(n, target)
    while n % t:
        t //= 2
    return t


def _pick_lane_tile(n, target):
    best = None
    for m in range(LANES, min(n, target) + 1, LANES):
        if n % m == 0:
            best = m
    return n if best is None else best


def kernel(x, norm_mix, w_in, pool_w, pool_scale, conv_dw, conv_b, conv_ln_g, conv_ln_b, ssm_conv_w, ssm_conv_b, ssm_dt_bias, ssm_a_log, ssm_d, ssm_norm, w_br, w_out, norm_ffn, ffn_w_gate, ffn_w_up, ffn_w_down, moe_router, moe_w_gate, moe_w_up, moe_w_down, final_norm):
    bsz, seq, d = x.shape
    depth = norm_mix.shape[0]
    mixw = pool_scale.shape[-1]
    t = bsz * seq
    row = lambda v: v.reshape(1, -1).astype(F32)

    tab_att = _rope_tables(seq, HEAD_DIM, ROPE_DIM, LANES // HEAD_DIM)
    tab_idx = _rope_tables(seq, IDX_DIM, IDX_ROPE_DIM, LANES // IDX_DIM)
    tab_tail = _rope_tables(seq, IDX_DIM, IDX_ROPE_DIM, 1)

    tm = _pick_tile(t, 1024)
    ts_seq = _pick_tile(seq, 512)
    fin = row(final_norm)

    xf = x.reshape(t, d)
    for layer in range(depth):
        w_packed, blk = _pack_w_in(w_in[layer], d, mixw)
        idx_w = 4 * LANES
        proj, aux = _norm_matmul(xf, row(norm_mix[layer]), w_packed, tm, _pick_tile(w_packed.shape[1], 1024),
                                 blk["idx"] * idx_w, idx_w)

        y_a = _pool_mixer(proj, blk["pool"], pool_w[layer].astype(BF16), row(pool_scale[layer]), bsz, seq, ts_seq)
        qt, k_r, vt, qit, ki_r, wt = _rope_prep(proj, aux, blk["q"], blk["kv"], tab_att, tab_idx,
                                                tab_tail, bsz, seq, ts_seq)
        y_b = _dsa_attention(qt, qit, wt, k_r, ki_r, vt, bsz, seq)
        y_c = _conformer(proj, blk["conv"], conv_dw[layer], row(conv_b[layer]), row(conv_ln_g[layer]),
                         row(conv_ln_b[layer]), bsz, seq, _pick_tile(seq, 256))
        y_d = _ssd(proj, aux, blk["z"], blk["xs"], blk["bc"], blk["dt"], ssm_conv_w[layer], row(ssm_conv_b[layer]),
                   _pad_lanes(ssm_dt_bias[layer]), _pad_lanes(-jnp.exp(ssm_a_log[layer].astype(F32))),
                   _pad_lanes(ssm_d[layer]), row(ssm_norm[layer]), bsz, seq)
        xf = _merge(xf, y_a, y_b, y_c, y_d, proj, blk["gates"], w_br[layer].astype(BF16),
                    w_out[layer].astype(BF16), _pick_tile(t, 512))

        last = layer == depth - 1
        jj = layer // 2
        if layer % 2 == 0:
            ff = ffn_w_gate.shape[-1]
            xf = _ffn(xf, row(norm_ffn[layer]), ffn_w_gate[jj].astype(BF16), ffn_w_up[jj].astype(BF16),
                      ffn_w_down[jj].astype(BF16), fin, last, _pick_tile(t, 512), _pick_lane_tile(ff, 1408))
        else:
            ff = moe_w_gate.shape[-1]
            xf = _moe(xf, row(norm_ffn[layer]), moe_router[jj], moe_w_gate[jj].astype(BF16),
                      moe_w_up[jj].astype(BF16), moe_w_down[jj].astype(BF16), fin, last, tm,
                      _pick_tile(t, 1024), _pick_lane_tile(ff, 896))
    if depth == 0:
        raise ValueError("depth must be positive")
    return xf.reshape(bsz, seq, d)
```

```python
import functools

import numpy as np
import jax
import jax.numpy as jnp
from jax import lax
from jax.experimental import pallas as pl
from jax.experimental.pallas import tpu as pltpu

POOL_WINDOWS = (2, 4, 8, 16)
ATT_HEADS = 8
ATT_KV_HEADS = 2
HEAD_DIM = 64
ROPE_DIM = HEAD_DIM // 4
ROPE_THETA = 500000.0
IDX_HEADS = 8
IDX_DIM = 32
IDX_ROPE_DIM = IDX_DIM // 4
TOPK_MAX = 256
SSM_HEADS = 8
SSM_HEAD_DIM = 64
SSM_GROUPS = 2
SSM_STATE = 64
SSM_CHUNK = 128
TOP_K = 2
NORM_EPS = 1e-6

LANES = 128
SUBLANES = 8
VMEM_LIMIT_BYTES = 56 * 1024 * 1024

F32 = jnp.float32
BF16 = jnp.bfloat16
INT_MIN = -(2 ** 31)
NEG_BIG = -1e30


def _cparams(n_axes):
    return pltpu.CompilerParams(dimension_semantics=("arbitrary",) * n_axes,
                                vmem_limit_bytes=VMEM_LIMIT_BYTES)


def _sigmoid(x):
    return 1.0 / (1.0 + jnp.exp(-x))


def _rms_rows(x, g):
    return x * lax.rsqrt(jnp.mean(x * x, axis=-1, keepdims=True) + NORM_EPS) * g


def _norm_matmul_kernel(x_ref, g_ref, w_ref, o_ref, aux_ref, h_ref, *, aux_j, aux_off):
    @pl.when(pl.program_id(1) == 0)
    def _():
        h_ref[...] = _rms_rows(x_ref[...], g_ref[...]).astype(h_ref.dtype)

    res = jnp.dot(h_ref[...], w_ref[...], preferred_element_type=F32)
    o_ref[...] = res.astype(o_ref.dtype)

    @pl.when(pl.program_id(1) == aux_j)
    def _():
        aux_ref[...] = res[:, aux_off:aux_off + aux_ref.shape[1]]


def _norm_matmul(x, g, w, tm, tn, aux_col, aux_w):
    t, d = x.shape
    n = w.shape[1]
    assert aux_col // tn == (aux_col + aux_w - 1) // tn
    kern = functools.partial(_norm_matmul_kernel, aux_j=aux_col // tn, aux_off=aux_col % tn)
    return pl.pallas_call(
        kern,
        grid=(t // tm, n // tn),
        in_specs=[pl.BlockSpec((tm, d), lambda i, j: (i, 0)),
                  pl.BlockSpec((1, d), lambda i, j: (0, 0)),
                  pl.BlockSpec((d, tn), lambda i, j: (0, j))],
        out_specs=[pl.BlockSpec((tm, tn), lambda i, j: (i, j)),
                   pl.BlockSpec((tm, aux_w), lambda i, j: (i, 0))],
        out_shape=[jax.ShapeDtypeStruct((t, n), BF16), jax.ShapeDtypeStruct((t, aux_w), F32)],
        scratch_shapes=[pltpu.VMEM((tm, d), BF16)],
        compiler_params=_cparams(2),
        name="norm_in_proj",
    )(x, g, w)


POOL_HIST = 16


def _pool_kernel(u_ref, w_ref, sc_ref, o_ref, buf_ref):
    s = pl.program_id(1)
    ts = u_ref.shape[0]
    gw = w_ref.shape[1]

    @pl.when(s == 0)
    def _():
        buf_ref[0:POOL_HIST, :] = jnp.zeros((POOL_HIST, buf_ref.shape[1]), F32)

    @pl.when(s > 0)
    def _():
        buf_ref[0:POOL_HIST, :] = buf_ref[ts:ts + POOL_HIST, :]

    buf_ref[POOL_HIST:POOL_HIST + ts, :] = u_ref[...].astype(F32)
    pos = s * ts + lax.broadcasted_iota(jnp.int32, (ts, 1), 0)
    for g, win in enumerate(POOL_WINDOWS):
        cols = slice(g * gw, (g + 1) * gw)
        acc = buf_ref[POOL_HIST:POOL_HIST + ts, cols]
        cur = acc
        for k in range(1, win):
            acc = acc + buf_ref[POOL_HIST - k:POOL_HIST - k + ts, cols]
        cnt = jnp.minimum(pos + 1, win).astype(F32)
        p = acc / cnt - cur
        y = jnp.dot(p.astype(BF16), w_ref[g], preferred_element_type=F32)
        o_ref[:, cols] = (y * sc_ref[:, cols]).astype(o_ref.dtype)


def _pool_mixer(proj, col_block, w, scale, bsz, seq, ts):
    mixw = scale.shape[-1]
    nblk = seq // ts
    return pl.pallas_call(
        _pool_kernel,
        grid=(bsz, nblk),
        in_specs=[pl.BlockSpec((ts, mixw), lambda b, s: (b * nblk + s, col_block)),
                  pl.BlockSpec(w.shape, lambda b, s: (0, 0, 0)),
                  pl.BlockSpec((1, mixw), lambda b, s: (0, 0))],
        out_specs=pl.BlockSpec((ts, mixw), lambda b, s: (b * nblk + s, 0)),
        out_shape=jax.ShapeDtypeStruct((bsz * seq, mixw), BF16),
        scratch_shapes=[pltpu.VMEM((POOL_HIST + ts, mixw), F32)],
        compiler_params=_cparams(2),
        name="pool_mixer",
    )(proj, w, scale)


CONV_HIST = 32


def _conformer_kernel(u_ref, dw_ref, db_ref, lg_ref, lb_ref, o_ref, buf_ref, sh_ref, tmp_ref):
    s = pl.program_id(1)
    ts = u_ref.shape[0]
    c = o_ref.shape[1]
    width = dw_ref.shape[0]

    @pl.when(s == 0)
    def _():
        buf_ref[0:CONV_HIST, :] = jnp.zeros((CONV_HIST, c), F32)

    @pl.when(s > 0)
    def _():
        buf_ref[0:CONV_HIST, :] = buf_ref[ts:ts + CONV_HIST, :]

    a = u_ref[:, 0:c].astype(F32)
    gt = u_ref[:, c:2 * c].astype(F32)
    buf_ref[CONV_HIST:CONV_HIST + ts, :] = a * _sigmoid(gt)
    nsh = CONV_HIST + ts - SUBLANES
    for sft in range(1, SUBLANES):
        sh_ref[sft - 1, 0:nsh, :] = buf_ref[sft:sft + nsh, :]
    rc = min(ts, LANES)
    for r0 in range(0, ts, rc):
        for c0 in range(0, c, LANES):
            acc = jnp.zeros((rc, LANES), F32) + db_ref[:, c0:c0 + LANES]
            for k in range(width):
                off = CONV_HIST - (width - 1) + k + r0
                sft = off % SUBLANES
                if sft == 0:
                    tap = buf_ref[off:off + rc, c0:c0 + LANES]
                else:
                    tap = sh_ref[sft - 1, off - sft:off - sft + rc, c0:c0 + LANES]
                acc = acc + tap * dw_ref[k:k + 1, c0:c0 + LANES]
            tmp_ref[r0:r0 + rc, c0:c0 + LANES] = acc
    acc = tmp_ref[...]
    mu = jnp.mean(acc, axis=-1, keepdims=True)
    xc = acc - mu
    y = xc * lax.rsqrt(jnp.mean(xc * xc, axis=-1, keepdims=True) + NORM_EPS)
    y = y * lg_ref[...] + lb_ref[...]
    o_ref[...] = (y * _sigmoid(y)).astype(o_ref.dtype)


def _conformer(proj, col_block, dw, db, lg, lb, bsz, seq, ts):
    c = dw.shape[1]
    nblk = seq // ts
    return pl.pallas_call(
        _conformer_kernel,
        grid=(bsz, nblk),
        in_specs=[pl.BlockSpec((ts, 2 * c), lambda b, s: (b * nblk + s, col_block)),
                  pl.BlockSpec(dw.shape, lambda b, s: (0, 0)),
                  pl.BlockSpec((1, c), lambda b, s: (0, 0)),
                  pl.BlockSpec((1, c), lambda b, s: (0, 0)),
                  pl.BlockSpec((1, c), lambda b, s: (0, 0))],
        out_specs=pl.BlockSpec((ts, c), lambda b, s: (b * nblk + s, 0)),
        out_shape=jax.ShapeDtypeStruct((bsz * seq, c), BF16),
        scratch_shapes=[pltpu.VMEM((CONV_HIST + ts, c), F32),
                        pltpu.VMEM((SUBLANES - 1, CONV_HIST + ts, c), F32),
                        pltpu.VMEM((ts, c), F32)],
        compiler_params=_cparams(2),
        name="conformer_conv",
    )(proj, dw, db, lg, lb)


SSM_HIST = 8


def _ssd_kernel(z_ref, xs_ref, bc_ref, dt_ref, cw_ref, cb_ref, dtb_ref, a_ref, dsk_ref, ng_ref,
                o_ref, y_ref, xbuf_ref, bcbuf_ref, state_ref):
    c_idx = pl.program_id(1)
    L = xs_ref.shape[0]
    mixw = xs_ref.shape[1]
    bcw = bc_ref.shape[1]
    width = cw_ref.shape[0]
    hp = SSM_HEAD_DIM
    ns = SSM_STATE
    rep = SSM_HEADS // SSM_GROUPS

    @pl.when(c_idx == 0)
    def _():
        xbuf_ref[0:SSM_HIST, :] = jnp.zeros((SSM_HIST, mixw), F32)
        bcbuf_ref[0:SSM_HIST, :] = jnp.zeros((SSM_HIST, bcw), F32)
        state_ref[...] = jnp.zeros(state_ref.shape, F32)

    @pl.when(c_idx > 0)
    def _():
        xbuf_ref[0:SSM_HIST, :] = xbuf_ref[L:L + SSM_HIST, :]
        bcbuf_ref[0:SSM_HIST, :] = bcbuf_ref[L:L + SSM_HIST, :]

    xbuf_ref[SSM_HIST:SSM_HIST + L, :] = xs_ref[...].astype(F32)
    bcbuf_ref[SSM_HIST:SSM_HIST + L, :] = bc_ref[...].astype(F32)

    xc = jnp.zeros((L, mixw), F32) + cb_ref[:, 0:mixw]
    bcc = jnp.zeros((L, bcw), F32) + cb_ref[:, mixw:mixw + bcw]
    for k in range(width):
        off = SSM_HIST - (width - 1) + k
        xc = xc + xbuf_ref[off:off + L, :] * cw_ref[k:k + 1, 0:mixw]
        bcc = bcc + bcbuf_ref[off:off + L, :] * cw_ref[k:k + 1, mixw:mixw + bcw]
    xc = xc * _sigmoid(xc)
    bcc = bcc * _sigmoid(bcc)

    dtr = dt_ref[...] + dtb_ref[...]
    dt = jnp.maximum(dtr, 0.0) + jnp.log(1.0 + jnp.exp(-jnp.abs(dtr)))
    da = dt * a_ref[...]
    row_i = lax.broadcasted_iota(jnp.int32, (L, L), 0)
    col_i = lax.broadcasted_iota(jnp.int32, (L, L), 1)
    tri = col_i <= row_i
    a_cum = jnp.dot(tri.astype(F32), da, preferred_element_type=F32,
                    precision=lax.Precision.HIGHEST)
    a_cum_t = a_cum.T
    a_last = a_cum[L - 1:L, :]
    dec_last = jnp.exp(a_last)
    dec_out = jnp.exp(a_cum)
    dec_st = jnp.exp(a_last - a_cum)

    cbs = []
    for g in range(SSM_GROUPS):
        bg = bcc[:, g * ns:(g + 1) * ns].astype(BF16)
        cg = bcc[:, SSM_GROUPS * ns + g * ns:SSM_GROUPS * ns + (g + 1) * ns].astype(BF16)
        cbs.append(lax.dot_general(cg, bg, (((1,), (1,)), ((), ())), preferred_element_type=F32))

    for h in range(SSM_HEADS):
        g = h // rep
        x_h = xc[:, h * hp:(h + 1) * hp]
        b_g = bcc[:, g * ns:(g + 1) * ns]
        c_g = bcc[:, SSM_GROUPS * ns + g * ns:SSM_GROUPS * ns + (g + 1) * ns]
        xd = x_h * dt[:, h:h + 1]
        seg = a_cum[:, h:h + 1] - a_cum_t[h:h + 1, :]
        lmat = jnp.exp(jnp.where(tri, seg, -jnp.inf))
        y_diag = jnp.dot((cbs[g] * lmat).astype(BF16), xd.astype(BF16), preferred_element_type=F32)
        st = state_ref[h]
        y_off = jnp.dot(c_g.astype(BF16), st.astype(BF16), preferred_element_type=F32) * dec_out[:, h:h + 1]
        bd = (b_g * dec_st[:, h:h + 1]).astype(BF16)
        st_new = lax.dot_general(bd, xd.astype(BF16), (((0,), (0,)), ((), ())),
                                 preferred_element_type=F32)
        state_ref[h] = st * dec_last[:, h:h + 1] + st_new
        y_h = y_diag + y_off + dsk_ref[:, h:h + 1] * x_h
        zz = z_ref[:, h * hp:(h + 1) * hp].astype(F32)
        y_ref[:, h * hp:(h + 1) * hp] = y_h * (zz * _sigmoid(zz))

    o_ref[...] = _rms_rows(y_ref[...], ng_ref[...]).astype(o_ref.dtype)


def _ssd(proj, aux, z_block, xs_block, bc_block, dt_block, cw, cb, dtb, a_neg, dsk, ng, bsz, seq):
    mixw = ng.shape[-1]
    bcw = cw.shape[1] - mixw
    L = SSM_CHUNK
    nblk = seq // L
    row = lambda b, s: b * nblk + s
    return pl.pallas_call(
        _ssd_kernel,
        grid=(bsz, nblk),
        in_specs=[pl.BlockSpec((L, mixw), lambda b, s: (row(b, s), z_block)),
                  pl.BlockSpec((L, mixw), lambda b, s: (row(b, s), xs_block)),
                  pl.BlockSpec((L, bcw), lambda b, s: (row(b, s), bc_block)),
                  pl.BlockSpec((L, LANES), lambda b, s: (row(b, s), dt_block)),
                  pl.BlockSpec(cw.shape, lambda b, s: (0, 0)),
                  pl.BlockSpec(cb.shape, lambda b, s: (0, 0)),
                  pl.BlockSpec((1, LANES), lambda b, s: (0, 0)),
                  pl.BlockSpec((1, LANES), lambda b, s: (0, 0)),
                  pl.BlockSpec((1, LANES), lambda b, s: (0, 0)),
                  pl.BlockSpec((1, mixw), lambda b, s: (0, 0))],
        out_specs=pl.BlockSpec((L, mixw), lambda b, s: (row(b, s), 0)),
        out_shape=jax.ShapeDtypeStruct((bsz * seq, mixw), BF16),
        scratch_shapes=[pltpu.VMEM((L, mixw), F32),
                        pltpu.VMEM((SSM_HIST + L, mixw), F32),
                        pltpu.VMEM((SSM_HIST + L, bcw), F32),
                        pltpu.VMEM((SSM_HEADS, SSM_STATE, SSM_HEAD_DIM), F32)],
        compiler_params=_cparams(2),
        name="ssd_mixer",
    )(proj, proj, proj, aux, cw, cb, dtb, a_neg, dsk, ng)


def _rope_cols(x, c, s1, s2, half):
    return (x * c + pltpu.roll(x, LANES - half, axis=1) * s1 + pltpu.roll(x, half, axis=1) * s2)


LOG2E = 1.4426950408889634


def _rope_prep_kernel(q_ref, kv_ref, idx_ref, ta_ref, ti_ref, tt_ref,
                      qg_ref, k_ref, vt_ref, qig_ref, ki_ref, wg_ref):
    hd = HEAD_DIM
    ts = q_ref.shape[0]
    rep = ATT_HEADS // ATT_KV_HEADS
    scale = (hd ** -0.5) * LOG2E
    ca, s1a, s2a = ta_ref[0], ta_ref[1], ta_ref[2]
    heads_per_col = LANES // hd
    for c in range(q_ref.shape[1] // LANES):
        qc = _rope_cols(q_ref[:, c * LANES:(c + 1) * LANES].astype(F32), ca, s1a, s2a, ROPE_DIM // 2) * scale
        qct = qc.T.astype(qg_ref.dtype)
        for r in range(heads_per_col):
            h = c * heads_per_col + r
            g, hh = h // rep, h % rep
            for jb in range(ts // LANES):
                dst = jb * rep * LANES + hh * LANES
                qg_ref[g, :, dst:dst + LANES] = qct[r * hd:(r + 1) * hd, jb * LANES:(jb + 1) * LANES]
    kw = ATT_KV_HEADS * hd
    kc = _rope_cols(kv_ref[:, 0:kw].astype(F32), ca, s1a, s2a, ROPE_DIM // 2)
    for g in range(ATT_KV_HEADS):
        k_ref[g] = kc[:, g * hd:(g + 1) * hd].astype(k_ref.dtype)
    vt_ref[...] = kv_ref[:, kw:2 * kw].astype(F32).T.astype(vt_ref.dtype)
    ci, s1i, s2i = ti_ref[0], ti_ref[1], ti_ref[2]
    nq = IDX_HEADS * IDX_DIM
    iheads_per_col = LANES // IDX_DIM
    for c in range(nq // LANES):
        qc = _rope_cols(idx_ref[:, c * LANES:(c + 1) * LANES], ci, s1i, s2i, IDX_ROPE_DIM // 2)
        qct = qc.T.astype(qig_ref.dtype)
        for r in range(iheads_per_col):
            h = c * iheads_per_col + r
            for jb in range(ts // LANES):
                dst = jb * IDX_HEADS * LANES + h * LANES
                qig_ref[:, dst:dst + LANES] = qct[r * IDX_DIM:(r + 1) * IDX_DIM, jb * LANES:(jb + 1) * LANES]
    tail = _rope_cols(idx_ref[:, nq:nq + LANES], tt_ref[0], tt_ref[1], tt_ref[2], IDX_ROPE_DIM // 2)
    ki_ref[...] = tail[:, 0:IDX_DIM].astype(ki_ref.dtype)
    tail_t = tail.T
    for h in range(IDX_HEADS):
        for jb in range(ts // LANES):
            dst = jb * IDX_HEADS * LANES + h * LANES
            wg_ref[:, dst:dst + LANES] = tail_t[IDX_DIM + h:IDX_DIM + h + 1, jb * LANES:(jb + 1) * LANES]


def _rope_prep(proj, aux, q_block, kv_block, tab_att, tab_idx, tab_tail, bsz, seq, ts):
    nblk = seq // ts
    qw = ATT_HEADS * HEAD_DIM
    kw = ATT_KV_HEADS * HEAD_DIM
    idxw = 4 * LANES
    rep = ATT_HEADS // ATT_KV_HEADS
    row = lambda b, s: b * nblk + s
    tab_spec = pl.BlockSpec((3, ts, LANES), lambda b, s: (0, s, 0))
    return pl.pallas_call(
        _rope_prep_kernel,
        grid=(bsz, nblk),
        in_specs=[pl.BlockSpec((ts, qw), lambda b, s: (row(b, s), q_block)),
                  pl.BlockSpec((ts, 2 * kw), lambda b, s: (row(b, s), kv_block)),
                  pl.BlockSpec((ts, idxw), lambda b, s: (row(b, s), 0)),
                  tab_spec, tab_spec, tab_spec],
        out_specs=[pl.BlockSpec((None, ATT_KV_HEADS, HEAD_DIM, rep * ts), lambda b, s: (b, 0, 0, s)),
                   pl.BlockSpec((None, ATT_KV_HEADS, ts, HEAD_DIM), lambda b, s: (b, 0, s, 0)),
                   pl.BlockSpec((None, kw, ts), lambda b, s: (b, 0, s)),
                   pl.BlockSpec((None, IDX_DIM, IDX_HEADS * ts), lambda b, s: (b, 0, s)),
                   pl.BlockSpec((None, ts, IDX_DIM), lambda b, s: (b, s, 0)),
                   pl.BlockSpec((None, 1, IDX_HEADS * ts), lambda b, s: (b, 0, s))],
        out_shape=[jax.ShapeDtypeStruct((bsz, ATT_KV_HEADS, HEAD_DIM, rep * seq), BF16),
                   jax.ShapeDtypeStruct((bsz, ATT_KV_HEADS, seq, HEAD_DIM), BF16),
                   jax.ShapeDtypeStruct((bsz, kw, seq), BF16),
                   jax.ShapeDtypeStruct((bsz, IDX_DIM, IDX_HEADS * seq), BF16),
                   jax.ShapeDtypeStruct((bsz, seq, IDX_DIM), BF16),
                   jax.ShapeDtypeStruct((bsz, 1, IDX_HEADS * seq), F32)],
        compiler_params=_cparams(2),
        name="rope_prep",
    )(proj, proj, aux, tab_att, tab_idx, tab_tail)


def _rope_tables(seq, head_dim, rot_dim, n_rot_heads):
    half = rot_dim // 2
    inv_freq = jnp.power(jnp.float32(ROPE_THETA), -jnp.arange(half, dtype=F32) * (2.0 / rot_dim))
    ang = jnp.arange(seq, dtype=F32)[:, None] * inv_freq[None, :]
    cos, sin = jnp.cos(ang), jnp.sin(ang)
    lane = np.arange(LANES)
    j = lane % head_dim
    rot_head = lane < n_rot_heads * head_dim
    first = rot_head & (j < half)
    second = rot_head & (j >= half) & (j < rot_dim)
    fi = np.where(first, j, 0)
    si = np.where(second, j - half, 0)
    c = jnp.where(first[None, :], cos[:, fi], jnp.where(second[None, :], cos[:, si], 1.0))
    s1 = jnp.where(first[None, :], -sin[:, fi], 0.0)
    s2 = jnp.where(second[None, :], sin[:, si], 0.0)
    return jnp.stack([c, s1, s2]).astype(F32)


def _sortable_key(score):
    score = jnp.where(score == 0.0, 0.0, score)
    bits = pltpu.bitcast(score, jnp.int32)
    return bits ^ (lax.shift_right_arithmetic(bits, 31) & jnp.int32(0x7FFFFFFF))


KEY_CHUNK = 256
COUNT_ROWS = 4 * SUBLANES


def _dsa_kernel(qg_ref, qig_ref, wg_ref, k_ref, ki_ref, vt_ref, o_ref, keys_ref, acc_ref, m_ref, s_ref,
                sa_ref, la_ref,
                *, top_k, seq_bits):
    j = pl.program_id(1)
    qb = LANES
    kc = KEY_CHUNK
    hd = HEAD_DIM
    rep = ATT_HEADS // ATT_KV_HEADS
    n_kc = lax.div(j * qb + (qb + kc - 1), kc)
    sub_i = lax.broadcasted_iota(jnp.int32, (kc, qb), 0)
    q_pos = j * qb + lax.broadcasted_iota(jnp.int32, (kc, qb), 1)

    n_pair = lax.div(j * qb + (qb + 2 * kc - 1), 2 * kc)
    last_chunk = 2 * n_pair - 1

    def idx_matmul(c):
        start = pl.multiple_of(c * kc, kc)
        return jnp.dot(ki_ref[pl.ds(start, kc), :], qig_ref[...], preferred_element_type=F32)

    def idx_keys(c, s_all):
        start = pl.multiple_of(c * kc, kc)
        score = jnp.zeros((kc, qb), F32)
        for h in range(IDX_HEADS):
            score = score + jnp.maximum(s_all[:, h * qb:(h + 1) * qb], 0.0) * wg_ref[:, h * qb:(h + 1) * qb]
        causal = (start + sub_i) <= q_pos
        keys_ref[pl.ds(start, kc), :] = jnp.where(causal, _sortable_key(score), jnp.int32(INT_MIN))

    sa_ref[...] = idx_matmul(0)

    def score_pair(p, carry):
        s_b = idx_matmul(2 * p + 1)
        idx_keys(2 * p, sa_ref[...])
        sa_ref[...] = idx_matmul(jnp.minimum(2 * p + 2, last_chunk))
        idx_keys(2 * p + 1, s_b)
        return carry

    lax.fori_loop(0, n_pair, score_pair, 0)

    def count_where(pred_fn):
        def body(c, acc):
            start = pl.multiple_of(c * kc, kc)
            ones = jnp.where(pred_fn(keys_ref[pl.ds(start, kc), :], start), 1.0, 0.0)
            return acc + jnp.sum(ones.reshape(kc // COUNT_ROWS, COUNT_ROWS, qb), axis=0)
        acc = lax.fori_loop(0, n_kc, body, jnp.zeros((COUNT_ROWS, qb), F32))
        return jnp.sum(acc, axis=0, keepdims=True)

    def bit_body(i, carry):
        tau, n_ge = carry
        cand = tau ^ lax.shift_left(jnp.int32(1), 31 - i)
        cnt = count_where(lambda kk, _: kk >= cand)
        ok = cnt >= top_k
        return jnp.where(ok, cand, tau), jnp.where(ok, cnt, n_ge)

    tau, n_ge = lax.fori_loop(0, 32, bit_body,
                              (jnp.full((1, qb), INT_MIN, jnp.int32), jnp.zeros((1, qb), F32)))
    tau = jnp.maximum(tau, jnp.int32(INT_MIN + 1))

    has_excess = jnp.max(jnp.where(n_ge > top_k, 1.0, 0.0)) > 0.5

    @pl.when(has_excess)
    def _():
        n_gt = count_where(lambda kk, _: kk > tau)
        need = top_k - n_gt

        def pos_body(i, xcut):
            cand = xcut | lax.shift_left(jnp.int32(1), seq_bits - 1 - i)
            cnt = count_where(lambda kk, st: (kk == tau) & ((st + sub_i) < cand))
            return jnp.where(cnt < need, cand, xcut)

        xcut = lax.fori_loop(0, seq_bits, pos_body, jnp.zeros((1, qb), jnp.int32))
        def fix_body(c, carry):
            start = pl.multiple_of(c * kc, kc)
            kk = keys_ref[pl.ds(start, kc), :]
            drop = (kk == tau) & (((start + sub_i) > xcut) | (need < 1))
            keys_ref[pl.ds(start, kc), :] = jnp.where(drop, jnp.int32(INT_MIN), kk)
            return carry

        lax.fori_loop(0, n_kc, fix_body, 0)

    m_ref[...] = jnp.full(m_ref.shape, NEG_BIG, F32)
    s_ref[...] = jnp.zeros(s_ref.shape, F32)
    acc_ref[...] = jnp.zeros(acc_ref.shape, F32)

    def qk_logits(c):
        start = pl.multiple_of(c * kc, kc)
        return [jnp.dot(k_ref[g, pl.ds(start, kc), :], qg_ref[g], preferred_element_type=F32)
                for g in range(ATT_KV_HEADS)]

    def softmax_pv(c, lgs):
        start = pl.multiple_of(c * kc, kc)
        bias = jnp.where(keys_ref[pl.ds(start, kc), :] >= tau, 0.0, NEG_BIG)
        for g in range(ATT_KV_HEADS):
            ps, alphas = [], []
            for hh in range(rep):
                h = g * rep + hh
                logit = lgs[g][:, hh * qb:(hh + 1) * qb] + bias
                m_old = m_ref[h:h + 1, :]
                m_new = jnp.maximum(m_old, jnp.max(logit, axis=0, keepdims=True))
                p = jnp.exp2(logit - m_new)
                alpha = jnp.exp2(m_old - m_new)
                s_ref[h:h + 1, :] = s_ref[h:h + 1, :] * alpha + jnp.sum(p, axis=0, keepdims=True)
                m_ref[h:h + 1, :] = m_new
                ps.append(p.astype(BF16))
                alphas.append(alpha)
            pv = jnp.dot(vt_ref[g * hd:(g + 1) * hd, pl.ds(start, kc)], jnp.concatenate(ps, axis=1),
                         preferred_element_type=F32)
            for hh in range(rep):
                rows = slice((g * rep + hh) * hd, (g * rep + hh + 1) * hd)
                acc_ref[rows, :] = acc_ref[rows, :] * alphas[hh] + pv[:, hh * qb:(hh + 1) * qb]

    for g, lg in enumerate(qk_logits(0)):
        la_ref[g] = lg

    def att_pair(p, carry):
        l_b = qk_logits(2 * p + 1)
        softmax_pv(2 * p, [la_ref[g] for g in range(ATT_KV_HEADS)])
        for g, lg in enumerate(qk_logits(jnp.minimum(2 * p + 2, last_chunk))):
            la_ref[g] = lg
        softmax_pv(2 * p + 1, l_b)
        return carry

    lax.fori_loop(0, n_pair, att_pair, 0)
    for h in range(ATT_HEADS):
        rows = slice(h * hd, (h + 1) * hd)
        acc_ref[rows, :] = acc_ref[rows, :] / s_ref[h:h + 1, :]
    for c0 in range(0, ATT_HEADS * hd, LANES):
        o_ref[:, c0:c0 + LANES] = acc_ref[c0:c0 + LANES, :].T.astype(o_ref.dtype)


def _dsa_attention(qg, qig, wg, k, ki, vt, bsz, seq):
    qb = LANES
    nblk = seq // qb
    qw = ATT_HEADS * HEAD_DIM
    kw = ATT_KV_HEADS * HEAD_DIM
    rep = ATT_HEADS // ATT_KV_HEADS
    top_k = min(TOPK_MAX, seq // 4)
    seq_bits = int(np.log2(seq))
    assert 2 ** seq_bits == seq and seq % (2 * KEY_CHUNK) == 0
    kern = functools.partial(_dsa_kernel, top_k=top_k, seq_bits=seq_bits)
    return pl.pallas_call(
        kern,
        grid=(bsz, nblk),
        in_specs=[pl.BlockSpec((None, ATT_KV_HEADS, HEAD_DIM, rep * qb), lambda b, j: (b, 0, 0, j)),
                  pl.BlockSpec((None, IDX_DIM, IDX_HEADS * qb), lambda b, j: (b, 0, j)),
                  pl.BlockSpec((None, 1, IDX_HEADS * qb), lambda b, j: (b, 0, j)),
                  pl.BlockSpec((None, ATT_KV_HEADS, seq, HEAD_DIM), lambda b, j: (b, 0, 0, 0)),
                  pl.BlockSpec((None, seq, IDX_DIM), lambda b, j: (b, 0, 0)),
                  pl.BlockSpec((None, kw, seq), lambda b, j: (b, 0, 0))],
        out_specs=pl.BlockSpec((qb, qw), lambda b, j: (b * nblk + j, 0)),
        out_shape=jax.ShapeDtypeStruct((bsz * seq, qw), BF16),
        scratch_shapes=[pltpu.VMEM((seq, qb), jnp.int32), pltpu.VMEM((qw, qb), F32),
                        pltpu.VMEM((ATT_HEADS, qb), F32), pltpu.VMEM((ATT_HEADS, qb), F32),
                        pltpu.VMEM((KEY_CHUNK, IDX_HEADS * qb), F32),
                        pltpu.VMEM((ATT_KV_HEADS, KEY_CHUNK, rep * qb), F32)],
        compiler_params=_cparams(2),
        name="dsa_attention",
    )(qg, qig, wg, k, ki, vt)


def _merge_kernel(x_ref, ya_ref, yb_ref, yc_ref, yd_ref, gl_ref, wbr_ref, wout_ref, o_ref):
    d = x_ref.shape[1]
    merged = None
    for n, y_ref in enumerate((ya_ref, yb_ref, yc_ref, yd_ref)):
        u = jnp.dot(y_ref[...], wbr_ref[n], preferred_element_type=F32)
        term = _sigmoid(gl_ref[:, n * d:(n + 1) * d].astype(F32)) * u
        merged = term if merged is None else merged + term
    o_ref[...] = x_ref[...] + jnp.dot(merged.astype(BF16), wout_ref[...], preferred_element_type=F32)


def _merge(x, ya, yb, yc, yd, proj, gate_block, wbr, wout, tm):
    t, d = x.shape
    mixw = ya.shape[1]
    nbr = wbr.shape[0]
    y_spec = pl.BlockSpec((tm, mixw), lambda i: (i, 0))
    return pl.pallas_call(
        _merge_kernel,
        grid=(t // tm,),
        in_specs=[pl.BlockSpec((tm, d), lambda i: (i, 0)), y_spec, y_spec, y_spec, y_spec,
                  pl.BlockSpec((tm, nbr * d), lambda i: (i, gate_block)),
                  pl.BlockSpec(wbr.shape, lambda i: (0, 0, 0)),
                  pl.BlockSpec(wout.shape, lambda i: (0, 0))],
        out_specs=pl.BlockSpec((tm, d), lambda i: (i, 0)),
        out_shape=jax.ShapeDtypeStruct((t, d), F32),
        compiler_params=_cparams(1),
        name="merge_out_proj",
    )(x, ya, yb, yc, yd, proj, wbr, wout)


def _ffn_kernel(x_ref, g_ref, wg_ref, wu_ref, wd_ref, fin_ref, o_ref, h_ref, acc_ref, *, final_norm):
    f = pl.program_id(1)

    @pl.when(f == 0)
    def _():
        h_ref[...] = _rms_rows(x_ref[...], g_ref[...]).astype(h_ref.dtype)
        acc_ref[...] = jnp.zeros(acc_ref.shape, F32)

    h = h_ref[...]
    a = jnp.dot(h, wg_ref[...], preferred_element_type=F32)
    b = jnp.dot(h, wu_ref[...], preferred_element_type=F32)
    t = (a * _sigmoid(a)) * b
    acc_ref[...] += jnp.dot(t.astype(BF16), wd_ref[...], preferred_element_type=F32)

    @pl.when(f == pl.num_programs(1) - 1)
    def _():
        y = x_ref[...] + acc_ref[...]
        o_ref[...] = _rms_rows(y, fin_ref[...]) if final_norm else y


def _ffn(x, g, wg, wu, wd, fin, final_norm, tm, tf):
    t, d = x.shape
    ff = wg.shape[1]
    return pl.pallas_call(
        functools.partial(_ffn_kernel, final_norm=final_norm),
        grid=(t // tm, ff // tf),
        in_specs=[pl.BlockSpec((tm, d), lambda i, f: (i, 0)),
                  pl.BlockSpec((1, d), lambda i, f: (0, 0)),
                  pl.BlockSpec((d, tf), lambda i, f: (0, f)),
                  pl.BlockSpec((d, tf), lambda i, f: (0, f)),
                  pl.BlockSpec((tf, d), lambda i, f: (f, 0)),
                  pl.BlockSpec((1, d), lambda i, f: (0, 0))],
        out_specs=pl.BlockSpec((tm, d), lambda i, f: (i, 0)),
        out_shape=jax.ShapeDtypeStruct((t, d), F32),
        scratch_shapes=[pltpu.VMEM((tm, d), BF16), pltpu.VMEM((tm, d), F32)],
        compiler_params=_cparams(2),
        name="dense_swiglu",
    )(x, g, wg, wu, wd, fin)


def _router_kernel(x_ref, g_ref, r_ref, h_ref, mi_ref, mp_ref, cnt_ref, tri_ref, carry_ref, *, n_experts):
    i = pl.program_id(0)
    tm = x_ref.shape[0]
    lane = lax.broadcasted_iota(jnp.int32, (tm, LANES), 1)

    @pl.when(i == 0)
    def _():
        carry_ref[...] = jnp.zeros(carry_ref.shape, F32)
        r_i = lax.broadcasted_iota(jnp.int32, (tm, tm), 0)
        c_i = lax.broadcasted_iota(jnp.int32, (tm, tm), 1)
        tri_ref[...] = jnp.where(c_i < r_i, 1.0, 0.0).astype(tri_ref.dtype)

    hf = _rms_rows(x_ref[...], g_ref[...])
    h_ref[...] = hf
    logits = jnp.dot(hf, r_ref[...], preferred_element_type=F32, precision=lax.Precision.HIGHEST)
    logits = jnp.where(lane < n_experts, logits, -jnp.inf)
    m1 = jnp.max(logits, axis=-1, keepdims=True)
    i1 = jnp.min(jnp.where(logits == m1, lane, LANES), axis=-1, keepdims=True)
    rest = jnp.where(lane == i1, -jnp.inf, logits)
    m2 = jnp.max(rest, axis=-1, keepdims=True)
    i2 = jnp.min(jnp.where(rest == m2, lane, LANES), axis=-1, keepdims=True)
    e2 = jnp.exp(m2 - m1)
    p1 = 1.0 / (1.0 + e2)
    p2 = e2 / (1.0 + e2)
    oh1 = lane == i1
    oh2 = lane == i2
    ohs = jnp.where(oh1, 1.0, jnp.where(oh2, 1.0, 0.0))
    pref = jnp.dot(tri_ref[...], ohs.astype(tri_ref.dtype), preferred_element_type=F32) + carry_ref[...]
    rank1 = jnp.sum(jnp.where(oh1, pref, 0.0), axis=-1, keepdims=True).astype(jnp.int32)
    rank2 = jnp.sum(jnp.where(oh2, pref, 0.0), axis=-1, keepdims=True).astype(jnp.int32)
    carry_ref[...] = carry_ref[...] + jnp.sum(ohs, axis=0, keepdims=True)
    cnt_ref[...] = carry_ref[...]
    mi_ref[...] = jnp.where(lane == 0, i1, jnp.where(lane == 1, i2, jnp.where(lane == 2, rank1,
                                                                          jnp.where(lane == 3, rank2, 0))))
    mp_ref[...] = jnp.where(lane == 0, p1, jnp.where(lane == 1, p2, 0.0))


def _router(x, g, router, tm):
    t, d = x.shape
    n_experts = router.shape[1]
    rpad = jnp.zeros((d, LANES), F32).at[:, :n_experts].set(router.astype(F32))
    return pl.pallas_call(
        functools.partial(_router_kernel, n_experts=n_experts),
        grid=(t // tm,),
        in_specs=[pl.BlockSpec((tm, d), lambda i: (i, 0)),
                  pl.BlockSpec((1, d), lambda i: (0, 0)),
                  pl.BlockSpec((d, LANES), lambda i: (0, 0))],
        out_specs=[pl.BlockSpec((tm, d), lambda i: (i, 0)),
                   pl.BlockSpec((tm, LANES), lambda i: (i, 0)),
                   pl.BlockSpec((tm, LANES), lambda i: (i, 0)),
                   pl.BlockSpec((1, LANES), lambda i: (0, 0))],
        out_shape=[jax.ShapeDtypeStruct((t, d), F32),
                   jax.ShapeDtypeStruct((t, LANES), jnp.int32),
                   jax.ShapeDtypeStruct((t, LANES), F32),
                   jax.ShapeDtypeStruct((1, LANES), F32)],
        scratch_shapes=[pltpu.VMEM((tm, tm), BF16), pltpu.VMEM((1, LANES), F32)],
        compiler_params=_cparams(1),
        name="moe_router",
    )(x, g, rpad)


DMA_UNROLL = 8


def _row_copy(src, dst, sem):
    return pltpu.make_async_copy(src, dst, sem)


def _dispatch_kernel(dest_ref, h_ref, xs_in_ref, xs_ref, sem, *, n_tokens):
    del xs_in_ref
    td = h_ref.shape[0]
    base = pl.program_id(0) * td

    def issue(r, carry):
        for k in range(TOP_K):
            d = dest_ref[k * n_tokens + base + r]
            _row_copy(h_ref.at[pl.ds(r, 1), :], xs_ref.at[pl.ds(d, 1), :], sem).start(priority=k % 2)
        return carry

    def drain(r, carry):
        for k in range(TOP_K):
            _row_copy(h_ref.at[pl.ds(0, 1), :], xs_ref.at[pl.ds(0, 1), :], sem).wait()
        return carry

    lax.fori_loop(0, td, issue, 0, unroll=DMA_UNROLL)
    lax.fori_loop(0, td, drain, 0, unroll=DMA_UNROLL)


def _dispatch(dest, h, n_rows, td):
    t, d = h.shape
    xs0 = jnp.zeros((n_rows, d), F32)
    return pl.pallas_call(
        functools.partial(_dispatch_kernel, n_tokens=t),
        grid_spec=pltpu.PrefetchScalarGridSpec(
            num_scalar_prefetch=1,
            grid=(t // td,),
            in_specs=[pl.BlockSpec((td, d), lambda i, dest: (i, 0)),
                      pl.BlockSpec(memory_space=pl.ANY)],
            out_specs=pl.BlockSpec(memory_space=pl.ANY),
            scratch_shapes=[pltpu.SemaphoreType.DMA(())]),
        out_shape=jax.ShapeDtypeStruct((n_rows, d), F32),
        input_output_aliases={2: 0},
        compiler_params=_cparams(1),
        name="moe_dispatch",
    )(dest, h, xs0)


def _expert_kernel(te_ref, nu_ref, xs_ref, wg_ref, wu_ref, wd_ref, ys_ref, h_ref, acc_ref):
    i = pl.program_id(0)
    f = pl.program_id(1)
    last_f = pl.num_programs(1) - 1
    valid = i < nu_ref[0]

    @pl.when(valid & (f == 0))
    def _():
        h_ref[...] = xs_ref[...].astype(h_ref.dtype)
        acc_ref[...] = jnp.zeros(acc_ref.shape, F32)

    @pl.when(valid)
    def _():
        h = h_ref[...]
        a = jnp.dot(h, wg_ref[...], preferred_element_type=F32)
        b = jnp.dot(h, wu_ref[...], preferred_element_type=F32)
        t = (a * _sigmoid(a)) * b
        acc_ref[...] += jnp.dot(t.astype(BF16), wd_ref[...], preferred_element_type=F32)

    @pl.when(valid & (f == last_f))
    def _():
        ys_ref[...] = acc_ref[...]

    @pl.when(jnp.logical_not(valid) & (f == last_f))
    def _():
        ys_ref[...] = jnp.zeros(ys_ref.shape, F32)


def _expert_mlp(tile_expert, n_used, xs, wg, wu, wd, tg, tf):
    n_rows, d = xs.shape
    ff = wg.shape[-1]
    nf = ff // tf
    fblk = lambda i, f, te, nu: jnp.where(i < nu[0], f, nf - 1)
    return pl.pallas_call(
        _expert_kernel,
        grid_spec=pltpu.PrefetchScalarGridSpec(
            num_scalar_prefetch=2,
            grid=(n_rows // tg, nf),
            in_specs=[pl.BlockSpec((tg, d), lambda i, f, te, nu: (i, 0)),
                      pl.BlockSpec((None, d, tf), lambda i, f, te, nu: (te[i], 0, fblk(i, f, te, nu))),
                      pl.BlockSpec((None, d, tf), lambda i, f, te, nu: (te[i], 0, fblk(i, f, te, nu))),
                      pl.BlockSpec((None, tf, d), lambda i, f, te, nu: (te[i], fblk(i, f, te, nu), 0))],
            out_specs=pl.BlockSpec((tg, d), lambda i, f, te, nu: (i, 0)),
            scratch_shapes=[pltpu.VMEM((tg, d), BF16), pltpu.VMEM((tg, d), F32)]),
        out_shape=jax.ShapeDtypeStruct((n_rows, d), F32),
        compiler_params=_cparams(2),
        name="moe_experts",
    )(tile_expert, n_used, xs, wg, wu, wd)


def _combine_kernel(dest_ref, x_ref, mp_ref, fin_ref, ys_ref, o_ref, buf_ref, sem, *, n_tokens, final_norm):
    tc = x_ref.shape[0]
    base = pl.program_id(0) * tc

    def issue(r, carry):
        for k in range(TOP_K):
            d = dest_ref[k * n_tokens + base + r]
            _row_copy(ys_ref.at[pl.ds(d, 1), :], buf_ref.at[k, pl.ds(r, 1), :], sem).start(priority=k % 2)
        return carry

    def drain(r, carry):
        for k in range(TOP_K):
            _row_copy(ys_ref.at[pl.ds(0, 1), :], buf_ref.at[k, pl.ds(0, 1), :], sem).wait()
        return carry

    lax.fori_loop(0, tc, issue, 0, unroll=DMA_UNROLL)
    lax.fori_loop(0, tc, drain, 0, unroll=DMA_UNROLL)
    y = x_ref[...]
    for k in range(TOP_K):
        y = y + mp_ref[:, k:k + 1] * buf_ref[k]
    o_ref[...] = _rms_rows(y, fin_ref[...]) if final_norm else y


def _combine(dest, x, mp, fin, ys, final_norm, tc):
    t, d = x.shape
    return pl.pallas_call(
        functools.partial(_combine_kernel, n_tokens=t, final_norm=final_norm),
        grid_spec=pltpu.PrefetchScalarGridSpec(
            num_scalar_prefetch=1,
            grid=(t // tc,),
            in_specs=[pl.BlockSpec((tc, d), lambda i, dest: (i, 0)),
                      pl.BlockSpec((tc, LANES), lambda i, dest: (i, 0)),
                      pl.BlockSpec((1, d), lambda i, dest: (0, 0)),
                      pl.BlockSpec(memory_space=pl.ANY)],
            out_specs=pl.BlockSpec((tc, d), lambda i, dest: (i, 0)),
            scratch_shapes=[pltpu.VMEM((TOP_K, tc, d), F32), pltpu.SemaphoreType.DMA(())]),
        out_shape=jax.ShapeDtypeStruct((t, d), F32),
        compiler_params=_cparams(1),
        name="moe_combine",
    )(dest, x, mp, fin, ys)


def _moe(x, g, router, wg, wu, wd, fin, final_norm, tm, tg, tf):
    t, d = x.shape
    n_experts = router.shape[1]
    h, mi, mp, cnt = _router(x, g, router, tm)
    counts = cnt[0, :n_experts].astype(jnp.int32)
    padded = ((counts + tg - 1) // tg) * tg
    ends = jnp.cumsum(padded)
    offs = ends - padded
    dest = jnp.concatenate([offs[mi[:, k]] + mi[:, TOP_K + k] for k in range(TOP_K)]).astype(jnp.int32)
    n_tiles = (TOP_K * t) // tg + n_experts
    n_used = (ends[-1] // tg).astype(jnp.int32)
    tidx = jnp.minimum(jnp.arange(n_tiles, dtype=jnp.int32), n_used - 1)
    tile_expert = jnp.sum((tidx[:, None] * tg >= ends[None, :]).astype(jnp.int32), axis=1)
    xs = _dispatch(dest, h, n_tiles * tg, _pick_tile(t, 512))
    ys = _expert_mlp(tile_expert, n_used.reshape(1), xs, wg, wu, wd, tg, tf)
    return _combine(dest, x, mp, fin, ys, final_norm, _pick_tile(t, 512))


def _pack_w_in(w_in, d_model, mixw):
    qw = ATT_HEADS * HEAD_DIM
    kvw = 2 * ATT_KV_HEADS * HEAD_DIM
    qiw = IDX_HEADS * IDX_DIM
    bcw = 2 * SSM_GROUPS * SSM_STATE
    sizes = (mixw, qw, kvw, qiw, IDX_DIM, IDX_HEADS, 2 * mixw, mixw, mixw + bcw, SSM_HEADS, 4 * d_model)
    offs = np.concatenate([[0], np.cumsum(sizes)])
    assert offs[-1] == w_in.shape[1]
    seg = lambda i: w_in[:, offs[i]:offs[i + 1]]
    u_pool, q, kv, qi, ki, wi, u_conv, z, xbc, dt, gates = (seg(i) for i in range(len(sizes)))
    d = w_in.shape[0]
    zeros = lambda n: jnp.zeros((d, n), w_in.dtype)
    idx_blk = jnp.concatenate([qi, ki, wi, zeros(3 * LANES - qiw - IDX_DIM - IDX_HEADS),
                               dt, zeros(LANES - SSM_HEADS)], axis=1)
    packed = jnp.concatenate([gates, u_conv, u_pool, q, z, xbc[:, :mixw], idx_blk, kv, xbc[:, mixw:]], axis=1)
    widths = dict(gates=4 * d_model, conv=2 * mixw, pool=mixw, q=qw, z=mixw, xs=mixw, idx=4 * LANES,
                  kv=kvw, bc=bcw)
    blocks, off = {}, 0
    for name in ("gates", "conv", "pool", "q", "z", "xs", "idx", "kv", "bc"):
        assert off % widths[name] == 0
        blocks[name] = off // widths[name]
        off += widths[name]
    blocks["dt"] = 3
    return packed.astype(BF16), blocks


def _pad_lanes(v):
    return jnp.zeros((1, LANES), F32).at[0, :v.shape[0]].set(v.astype(F32))


def _pick_tile(n, target):
    t = min(n, target)
    while n % t:
        t //= 2
    return t


def _pick_lane_tile(n, target):
    best = None
    for m in range(LANES, min(n, target) + 1, LANES):
        if n % m == 0:
            best = m
    return n if best is None else best


def kernel(x, norm_mix, w_in, pool_w, pool_scale, conv_dw, conv_b, conv_ln_g, conv_ln_b, ssm_conv_w, ssm_conv_b, ssm_dt_bias, ssm_a_log, ssm_d, ssm_norm, w_br, w_out, norm_ffn, ffn_w_gate, ffn_w_up, ffn_w_down, moe_router, moe_w_gate, moe_w_up, moe_w_down, final_norm):
    bsz, seq, d = x.shape
    depth = norm_mix.shape[0]
    mixw = pool_scale.shape[-1]
    t = bsz * seq
    row = lambda v: v.reshape(1, -1).astype(F32)

    tab_att = _rope_tables(seq, HEAD_DIM, ROPE_DIM, LANES // HEAD_DIM)
    tab_idx = _rope_tables(seq, IDX_DIM, IDX_ROPE_DIM, LANES // IDX_DIM)
    tab_tail = _rope_tables(seq, IDX_DIM, IDX_ROPE_DIM, 1)

    tm = _pick_tile(t, 1024)
    ts_seq = _pick_tile(seq, 512)
    fin = row(final_norm)

    xf = x.reshape(t, d)
    for layer in range(depth):
        w_packed, blk = _pack_w_in(w_in[layer], d, mixw)
        idx_w = 4 * LANES
        proj, aux = _norm_matmul(xf, row(norm_mix[layer]), w_packed, _pick_tile(t, 2048),
                                 _pick_tile(w_packed.shape[1], 1024), blk["idx"] * idx_w, idx_w)

        y_a = _pool_mixer(proj, blk["pool"], pool_w[layer].astype(BF16), row(pool_scale[layer]), bsz, seq, ts_seq)
        qt, k_r, vt, qit, ki_r, wt = _rope_prep(proj, aux, blk["q"], blk["kv"], tab_att, tab_idx,
                                                tab_tail, bsz, seq, ts_seq)
        y_b = _dsa_attention(qt, qit, wt, k_r, ki_r, vt, bsz, seq)
        y_c = _conformer(proj, blk["conv"], conv_dw[layer], row(conv_b[layer]), row(conv_ln_g[layer]),
                         row(conv_ln_b[layer]), bsz, seq, _pick_tile(seq, 256))
        y_d = _ssd(proj, aux, blk["z"], blk["xs"], blk["bc"], blk["dt"], ssm_conv_w[layer], row(ssm_conv_b[layer]),
                   _pad_lanes(ssm_dt_bias[layer]), _pad_lanes(-jnp.exp(ssm_a_log[layer].astype(F32))),
                   _pad_lanes(ssm_d[layer]), row(ssm_norm[layer]), bsz, seq)
        xf = _merge(xf, y_a, y_b, y_c, y_d, proj, blk["gates"], w_br[layer].astype(BF16),
                    w_out[layer].astype(BF16), _pick_tile(t, 512))

        last = layer == depth - 1
        jj = layer // 2
        if layer % 2 == 0:
            ff = ffn_w_gate.shape[-1]
            xf = _ffn(xf, row(norm_ffn[layer]), ffn_w_gate[jj].astype(BF16), ffn_w_up[jj].astype(BF16),
                      ffn_w_down[jj].astype(BF16), fin, last, _pick_tile(t, 512), _pick_lane_tile(ff, 1408))
        else:
            ff = moe_w_gate.shape[-1]
            xf = _moe(xf, row(norm_ffn[layer]), moe_router[jj], moe_w_gate[jj].astype(BF16),
                      moe_w_up[jj].astype(BF16), moe_w_down[jj].astype(BF16), fin, last, tm,
                      _pick_tile(t, 1024), _pick_lane_tile(ff, 896))
    if depth == 0:
        raise ValueError("depth must be positive")
    return xf.reshape(bsz, seq, d)
```

```python
import functools

import numpy as np
import jax
import jax.numpy as jnp
from jax import lax
from jax.experimental import pallas as pl
from jax.experimental.pallas import tpu as pltpu

POOL_WINDOWS = (2, 4, 8, 16)
ATT_HEADS = 8
ATT_KV_HEADS = 2
HEAD_DIM = 64
ROPE_DIM = HEAD_DIM // 4
ROPE_THETA = 500000.0
IDX_HEADS = 8
IDX_DIM = 32
IDX_ROPE_DIM = IDX_DIM // 4
TOPK_MAX = 256
SSM_HEADS = 8
SSM_HEAD_DIM = 64
SSM_GROUPS = 2
SSM_STATE = 64
SSM_CHUNK = 128
TOP_K = 2
NORM_EPS = 1e-6

LANES = 128
SUBLANES = 8
VMEM_LIMIT_BYTES = 56 * 1024 * 1024

F32 = jnp.float32
BF16 = jnp.bfloat16
INT_MIN = -(2 ** 31)
NEG_BIG = -1e30


def _cparams(n_axes):
    return pltpu.CompilerParams(dimension_semantics=("arbitrary",) * n_axes,
                                vmem_limit_bytes=VMEM_LIMIT_BYTES)


def _sigmoid(x):
    return 1.0 / (1.0 + jnp.exp(-x))


def _rms_rows(x, g):
    return x * lax.rsqrt(jnp.mean(x * x, axis=-1, keepdims=True) + NORM_EPS) * g


def _norm_matmul_kernel(x_ref, g_ref, w_ref, o_ref, aux_ref, h_ref, *, aux_j, aux_off):
    @pl.when(pl.program_id(1) == 0)
    def _():
        h_ref[...] = _rms_rows(x_ref[...], g_ref[...]).astype(h_ref.dtype)

    res = jnp.dot(h_ref[...], w_ref[...], preferred_element_type=F32)
    o_ref[...] = res.astype(o_ref.dtype)

    @pl.when(pl.program_id(1) == aux_j)
    def _():
        aux_ref[...] = res[:, aux_off:aux_off + aux_ref.shape[1]]


def _norm_matmul(x, g, w, tm, tn, aux_col, aux_w):
    t, d = x.shape
    n = w.shape[1]
    assert aux_col // tn == (aux_col + aux_w - 1) // tn
    kern = functools.partial(_norm_matmul_kernel, aux_j=aux_col // tn, aux_off=aux_col % tn)
    return pl.pallas_call(
        kern,
        grid=(t // tm, n // tn),
        in_specs=[pl.BlockSpec((tm, d), lambda i, j: (i, 0)),
                  pl.BlockSpec((1, d), lambda i, j: (0, 0)),
                  pl.BlockSpec((d, tn), lambda i, j: (0, j))],
        out_specs=[pl.BlockSpec((tm, tn), lambda i, j: (i, j)),
                   pl.BlockSpec((tm, aux_w), lambda i, j: (i, 0))],
        out_shape=[jax.ShapeDtypeStruct((t, n), BF16), jax.ShapeDtypeStruct((t, aux_w), F32)],
        scratch_shapes=[pltpu.VMEM((tm, d), BF16)],
        compiler_params=_cparams(2),
        name="norm_in_proj",
    )(x, g, w)


POOL_HIST = 16


def _pool_kernel(u_ref, w_ref, sc_ref, o_ref, buf_ref):
    s = pl.program_id(1)
    ts = u_ref.shape[0]
    gw = w_ref.shape[1]

    @pl.when(s == 0)
    def _():
        buf_ref[0:POOL_HIST, :] = jnp.zeros((POOL_HIST, buf_ref.shape[1]), F32)

    @pl.when(s > 0)
    def _():
        buf_ref[0:POOL_HIST, :] = buf_ref[ts:ts + POOL_HIST, :]

    buf_ref[POOL_HIST:POOL_HIST + ts, :] = u_ref[...].astype(F32)
    pos = s * ts + lax.broadcasted_iota(jnp.int32, (ts, 1), 0)
    for g, win in enumerate(POOL_WINDOWS):
        cols = slice(g * gw, (g + 1) * gw)
        acc = buf_ref[POOL_HIST:POOL_HIST + ts, cols]
        cur = acc
        for k in range(1, win):
            acc = acc + buf_ref[POOL_HIST - k:POOL_HIST - k + ts, cols]
        cnt = jnp.minimum(pos + 1, win).astype(F32)
        p = acc / cnt - cur
        y = jnp.dot(p.astype(BF16), w_ref[g], preferred_element_type=F32)
        o_ref[:, cols] = (y * sc_ref[:, cols]).astype(o_ref.dtype)


def _pool_mixer(proj, col_block, w, scale, bsz, seq, ts):
    mixw = scale.shape[-1]
    nblk = seq // ts
    return pl.pallas_call(
        _pool_kernel,
        grid=(bsz, nblk),
        in_specs=[pl.BlockSpec((ts, mixw), lambda b, s: (b * nblk + s, col_block)),
                  pl.BlockSpec(w.shape, lambda b, s: (0, 0, 0)),
                  pl.BlockSpec((1, mixw), lambda b, s: (0, 0))],
        out_specs=pl.BlockSpec((ts, mixw), lambda b, s: (b * nblk + s, 0)),
        out_shape=jax.ShapeDtypeStruct((bsz * seq, mixw), BF16),
        scratch_shapes=[pltpu.VMEM((POOL_HIST + ts, mixw), F32)],
        compiler_params=_cparams(2),
        name="pool_mixer",
    )(proj, w, scale)


CONV_HIST = 32


def _conformer_kernel(u_ref, dw_ref, db_ref, lg_ref, lb_ref, o_ref, buf_ref, sh_ref, tmp_ref):
    s = pl.program_id(1)
    ts = u_ref.shape[0]
    c = o_ref.shape[1]
    width = dw_ref.shape[0]

    @pl.when(s == 0)
    def _():
        buf_ref[0:CONV_HIST, :] = jnp.zeros((CONV_HIST, c), F32)

    @pl.when(s > 0)
    def _():
        buf_ref[0:CONV_HIST, :] = buf_ref[ts:ts + CONV_HIST, :]

    a = u_ref[:, 0:c].astype(F32)
    gt = u_ref[:, c:2 * c].astype(F32)
    buf_ref[CONV_HIST:CONV_HIST + ts, :] = a * _sigmoid(gt)
    nsh = CONV_HIST + ts - SUBLANES
    for sft in range(1, SUBLANES):
        sh_ref[sft - 1, 0:nsh, :] = buf_ref[sft:sft + nsh, :]
    rc = min(ts, LANES)
    for r0 in range(0, ts, rc):
        for c0 in range(0, c, LANES):
            acc = jnp.zeros((rc, LANES), F32) + db_ref[:, c0:c0 + LANES]
            for k in range(width):
                off = CONV_HIST - (width - 1) + k + r0
                sft = off % SUBLANES
                if sft == 0:
                    tap = buf_ref[off:off + rc, c0:c0 + LANES]
                else:
                    tap = sh_ref[sft - 1, off - sft:off - sft + rc, c0:c0 + LANES]
                acc = acc + tap * dw_ref[k:k + 1, c0:c0 + LANES]
            tmp_ref[r0:r0 + rc, c0:c0 + LANES] = acc
    acc = tmp_ref[...]
    mu = jnp.mean(acc, axis=-1, keepdims=True)
    xc = acc - mu
    y = xc * lax.rsqrt(jnp.mean(xc * xc, axis=-1, keepdims=True) + NORM_EPS)
    y = y * lg_ref[...] + lb_ref[...]
    o_ref[...] = (y * _sigmoid(y)).astype(o_ref.dtype)


def _conformer(proj, col_block, dw, db, lg, lb, bsz, seq, ts):
    c = dw.shape[1]
    nblk = seq // ts
    return pl.pallas_call(
        _conformer_kernel,
        grid=(bsz, nblk),
        in_specs=[pl.BlockSpec((ts, 2 * c), lambda b, s: (b * nblk + s, col_block)),
                  pl.BlockSpec(dw.shape, lambda b, s: (0, 0)),
                  pl.BlockSpec((1, c), lambda b, s: (0, 0)),
                  pl.BlockSpec((1, c), lambda b, s: (0, 0)),
                  pl.BlockSpec((1, c), lambda b, s: (0, 0))],
        out_specs=pl.BlockSpec((ts, c), lambda b, s: (b * nblk + s, 0)),
        out_shape=jax.ShapeDtypeStruct((bsz * seq, c), BF16),
        scratch_shapes=[pltpu.VMEM((CONV_HIST + ts, c), F32),
                        pltpu.VMEM((SUBLANES - 1, CONV_HIST + ts, c), F32),
                        pltpu.VMEM((ts, c), F32)],
        compiler_params=_cparams(2),
        name="conformer_conv",
    )(proj, dw, db, lg, lb)


SSM_HIST = 8


def _ssd_kernel(z_ref, xs_ref, bc_ref, dt_ref, cw_ref, cb_ref, dtb_ref, a_ref, dsk_ref, ng_ref,
                o_ref, y_ref, xbuf_ref, bcbuf_ref, state_ref):
    c_idx = pl.program_id(1)
    L = xs_ref.shape[0]
    mixw = xs_ref.shape[1]
    bcw = bc_ref.shape[1]
    width = cw_ref.shape[0]
    hp = SSM_HEAD_DIM
    ns = SSM_STATE
    rep = SSM_HEADS // SSM_GROUPS

    @pl.when(c_idx == 0)
    def _():
        xbuf_ref[0:SSM_HIST, :] = jnp.zeros((SSM_HIST, mixw), F32)
        bcbuf_ref[0:SSM_HIST, :] = jnp.zeros((SSM_HIST, bcw), F32)
        state_ref[...] = jnp.zeros(state_ref.shape, F32)

    @pl.when(c_idx > 0)
    def _():
        xbuf_ref[0:SSM_HIST, :] = xbuf_ref[L:L + SSM_HIST, :]
        bcbuf_ref[0:SSM_HIST, :] = bcbuf_ref[L:L + SSM_HIST, :]

    xbuf_ref[SSM_HIST:SSM_HIST + L, :] = xs_ref[...].astype(F32)
    bcbuf_ref[SSM_HIST:SSM_HIST + L, :] = bc_ref[...].astype(F32)

    xc = jnp.zeros((L, mixw), F32) + cb_ref[:, 0:mixw]
    bcc = jnp.zeros((L, bcw), F32) + cb_ref[:, mixw:mixw + bcw]
    for k in range(width):
        off = SSM_HIST - (width - 1) + k
        xc = xc + xbuf_ref[off:off + L, :] * cw_ref[k:k + 1, 0:mixw]
        bcc = bcc + bcbuf_ref[off:off + L, :] * cw_ref[k:k + 1, mixw:mixw + bcw]
    xc = xc * _sigmoid(xc)
    bcc = bcc * _sigmoid(bcc)

    dtr = dt_ref[...] + dtb_ref[...]
    dt = jnp.maximum(dtr, 0.0) + jnp.log(1.0 + jnp.exp(-jnp.abs(dtr)))
    da = dt * a_ref[...]
    row_i = lax.broadcasted_iota(jnp.int32, (L, L), 0)
    col_i = lax.broadcasted_iota(jnp.int32, (L, L), 1)
    tri = col_i <= row_i
    a_cum = jnp.dot(tri.astype(F32), da, preferred_element_type=F32,
                    precision=lax.Precision.HIGHEST)
    a_cum_t = a_cum.T
    a_last = a_cum[L - 1:L, :]
    dec_last = jnp.exp(a_last)
    dec_out = jnp.exp(a_cum)
    dec_st = jnp.exp(a_last - a_cum)

    cbs = []
    for g in range(SSM_GROUPS):
        bg = bcc[:, g * ns:(g + 1) * ns].astype(BF16)
        cg = bcc[:, SSM_GROUPS * ns + g * ns:SSM_GROUPS * ns + (g + 1) * ns].astype(BF16)
        cbs.append(lax.dot_general(cg, bg, (((1,), (1,)), ((), ())), preferred_element_type=F32))

    for h in range(SSM_HEADS):
        g = h // rep
        x_h = xc[:, h * hp:(h + 1) * hp]
        b_g = bcc[:, g * ns:(g + 1) * ns]
        c_g = bcc[:, SSM_GROUPS * ns + g * ns:SSM_GROUPS * ns + (g + 1) * ns]
        xd = x_h * dt[:, h:h + 1]
        seg = a_cum[:, h:h + 1] - a_cum_t[h:h + 1, :]
        lmat = jnp.exp(jnp.where(tri, seg, -jnp.inf))
        y_diag = jnp.dot((cbs[g] * lmat).astype(BF16), xd.astype(BF16), preferred_element_type=F32)
        st = state_ref[h]
        y_off = jnp.dot(c_g.astype(BF16), st.astype(BF16), preferred_element_type=F32) * dec_out[:, h:h + 1]
        bd = (b_g * dec_st[:, h:h + 1]).astype(BF16)
        st_new = lax.dot_general(bd, xd.astype(BF16), (((0,), (0,)), ((), ())),
                                 preferred_element_type=F32)
        state_ref[h] = st * dec_last[:, h:h + 1] + st_new
        y_h = y_diag + y_off + dsk_ref[:, h:h + 1] * x_h
        zz = z_ref[:, h * hp:(h + 1) * hp].astype(F32)
        y_ref[:, h * hp:(h + 1) * hp] = y_h * (zz * _sigmoid(zz))

    o_ref[...] = _rms_rows(y_ref[...], ng_ref[...]).astype(o_ref.dtype)


def _ssd(proj, aux, z_block, xs_block, bc_block, dt_block, cw, cb, dtb, a_neg, dsk, ng, bsz, seq):
    mixw = ng.shape[-1]
    bcw = cw.shape[1] - mixw
    L = SSM_CHUNK
    nblk = seq // L
    row = lambda b, s: b * nblk + s
    return pl.pallas_call(
        _ssd_kernel,
        grid=(bsz, nblk),
        in_specs=[pl.BlockSpec((L, mixw), lambda b, s: (row(b, s), z_block)),
                  pl.BlockSpec((L, mixw), lambda b, s: (row(b, s), xs_block)),
                  pl.BlockSpec((L, bcw), lambda b, s: (row(b, s), bc_block)),
                  pl.BlockSpec((L, LANES), lambda b, s: (row(b, s), dt_block)),
                  pl.BlockSpec(cw.shape, lambda b, s: (0, 0)),
                  pl.BlockSpec(cb.shape, lambda b, s: (0, 0)),
                  pl.BlockSpec((1, LANES), lambda b, s: (0, 0)),
                  pl.BlockSpec((1, LANES), lambda b, s: (0, 0)),
                  pl.BlockSpec((1, LANES), lambda b, s: (0, 0)),
                  pl.BlockSpec((1, mixw), lambda b, s: (0, 0))],
        out_specs=pl.BlockSpec((L, mixw), lambda b, s: (row(b, s), 0)),
        out_shape=jax.ShapeDtypeStruct((bsz * seq, mixw), BF16),
        scratch_shapes=[pltpu.VMEM((L, mixw), F32),
                        pltpu.VMEM((SSM_HIST + L, mixw), F32),
                        pltpu.VMEM((SSM_HIST + L, bcw), F32),
                        pltpu.VMEM((SSM_HEADS, SSM_STATE, SSM_HEAD_DIM), F32)],
        compiler_params=_cparams(2),
        name="ssd_mixer",
    )(proj, proj, proj, aux, cw, cb, dtb, a_neg, dsk, ng)


def _rope_cols(x, c, s1, s2, half):
    return (x * c + pltpu.roll(x, LANES - half, axis=1) * s1 + pltpu.roll(x, half, axis=1) * s2)


LOG2E = 1.4426950408889634


def _rope_prep_kernel(q_ref, kv_ref, idx_ref, ta_ref, ti_ref, tt_ref,
                      qg_ref, k_ref, vt_ref, qig_ref, ki_ref, wg_ref):
    hd = HEAD_DIM
    ts = q_ref.shape[0]
    rep = ATT_HEADS // ATT_KV_HEADS
    scale = (hd ** -0.5) * LOG2E
    ca, s1a, s2a = ta_ref[0], ta_ref[1], ta_ref[2]
    heads_per_col = LANES // hd
    for c in range(q_ref.shape[1] // LANES):
        qc = _rope_cols(q_ref[:, c * LANES:(c + 1) * LANES].astype(F32), ca, s1a, s2a, ROPE_DIM // 2) * scale
        qct = qc.T.astype(qg_ref.dtype)
        for r in range(heads_per_col):
            h = c * heads_per_col + r
            g, hh = h // rep, h % rep
            for jb in range(ts // LANES):
                dst = jb * rep * LANES + hh * LANES
                qg_ref[g, :, dst:dst + LANES] = qct[r * hd:(r + 1) * hd, jb * LANES:(jb + 1) * LANES]
    kw = ATT_KV_HEADS * hd
    kc = _rope_cols(kv_ref[:, 0:kw].astype(F32), ca, s1a, s2a, ROPE_DIM // 2)
    for g in range(ATT_KV_HEADS):
        k_ref[g] = kc[:, g * hd:(g + 1) * hd].astype(k_ref.dtype)
    vt_ref[...] = kv_ref[:, kw:2 * kw].astype(F32).T.astype(vt_ref.dtype)
    ci, s1i, s2i = ti_ref[0], ti_ref[1], ti_ref[2]
    nq = IDX_HEADS * IDX_DIM
    iheads_per_col = LANES // IDX_DIM
    for c in range(nq // LANES):
        qc = _rope_cols(idx_ref[:, c * LANES:(c + 1) * LANES], ci, s1i, s2i, IDX_ROPE_DIM // 2)
        qct = qc.T.astype(qig_ref.dtype)
        for r in range(iheads_per_col):
            h = c * iheads_per_col + r
            for jb in range(ts // LANES):
                dst = jb * IDX_HEADS * LANES + h * LANES
                qig_ref[:, dst:dst + LANES] = qct[r * IDX_DIM:(r + 1) * IDX_DIM, jb * LANES:(jb + 1) * LANES]
    tail = _rope_cols(idx_ref[:, nq:nq + LANES], tt_ref[0], tt_ref[1], tt_ref[2], IDX_ROPE_DIM // 2)
    ki_ref[...] = tail[:, 0:IDX_DIM].astype(ki_ref.dtype)
    tail_t = tail.T
    for h in range(IDX_HEADS):
        for jb in range(ts // LANES):
            dst = jb * IDX_HEADS * LANES + h * LANES
            wg_ref[:, dst:dst + LANES] = tail_t[IDX_DIM + h:IDX_DIM + h + 1, jb * LANES:(jb + 1) * LANES]


def _rope_prep(proj, aux, q_block, kv_block, tab_att, tab_idx, tab_tail, bsz, seq, ts):
    nblk = seq // ts
    qw = ATT_HEADS * HEAD_DIM
    kw = ATT_KV_HEADS * HEAD_DIM
    idxw = 4 * LANES
    rep = ATT_HEADS // ATT_KV_HEADS
    row = lambda b, s: b * nblk + s
    tab_spec = pl.BlockSpec((3, ts, LANES), lambda b, s: (0, s, 0))
    return pl.pallas_call(
        _rope_prep_kernel,
        grid=(bsz, nblk),
        in_specs=[pl.BlockSpec((ts, qw), lambda b, s: (row(b, s), q_block)),
                  pl.BlockSpec((ts, 2 * kw), lambda b, s: (row(b, s), kv_block)),
                  pl.BlockSpec((ts, idxw), lambda b, s: (row(b, s), 0)),
                  tab_spec, tab_spec, tab_spec],
        out_specs=[pl.BlockSpec((None, ATT_KV_HEADS, HEAD_DIM, rep * ts), lambda b, s: (b, 0, 0, s)),
                   pl.BlockSpec((None, ATT_KV_HEADS, ts, HEAD_DIM), lambda b, s: (b, 0, s, 0)),
                   pl.BlockSpec((None, kw, ts), lambda b, s: (b, 0, s)),
                   pl.BlockSpec((None, IDX_DIM, IDX_HEADS * ts), lambda b, s: (b, 0, s)),
                   pl.BlockSpec((None, ts, IDX_DIM), lambda b, s: (b, s, 0)),
                   pl.BlockSpec((None, 1, IDX_HEADS * ts), lambda b, s: (b, 0, s))],
        out_shape=[jax.ShapeDtypeStruct((bsz, ATT_KV_HEADS, HEAD_DIM, rep * seq), BF16),
                   jax.ShapeDtypeStruct((bsz, ATT_KV_HEADS, seq, HEAD_DIM), BF16),
                   jax.ShapeDtypeStruct((bsz, kw, seq), BF16),
                   jax.ShapeDtypeStruct((bsz, IDX_DIM, IDX_HEADS * seq), BF16),
                   jax.ShapeDtypeStruct((bsz, seq, IDX_DIM), BF16),
                   jax.ShapeDtypeStruct((bsz, 1, IDX_HEADS * seq), F32)],
        compiler_params=_cparams(2),
        name="rope_prep",
    )(proj, proj, aux, tab_att, tab_idx, tab_tail)


def _rope_tables(seq, head_dim, rot_dim, n_rot_heads):
    half = rot_dim // 2
    inv_freq = jnp.power(jnp.float32(ROPE_THETA), -jnp.arange(half, dtype=F32) * (2.0 / rot_dim))
    ang = jnp.arange(seq, dtype=F32)[:, None] * inv_freq[None, :]
    cos, sin = jnp.cos(ang), jnp.sin(ang)
    lane = np.arange(LANES)
    j = lane % head_dim
    rot_head = lane < n_rot_heads * head_dim
    first = rot_head & (j < half)
    second = rot_head & (j >= half) & (j < rot_dim)
    fi = np.where(first, j, 0)
    si = np.where(second, j - half, 0)
    c = jnp.where(first[None, :], cos[:, fi], jnp.where(second[None, :], cos[:, si], 1.0))
    s1 = jnp.where(first[None, :], -sin[:, fi], 0.0)
    s2 = jnp.where(second[None, :], sin[:, si], 0.0)
    return jnp.stack([c, s1, s2]).astype(F32)


def _sortable_key(score):
    score = jnp.where(score == 0.0, 0.0, score)
    bits = pltpu.bitcast(score, jnp.int32)
    return bits ^ (lax.shift_right_arithmetic(bits, 31) & jnp.int32(0x7FFFFFFF))


KEY_CHUNK = 256
COUNT_ROWS = 4 * SUBLANES


def _dsa_kernel(qg_ref, qig_ref, wg_ref, k_ref, ki_ref, vt_ref, o_ref, keys_ref, acc_ref, m_ref, s_ref,
                sa_ref, la_ref,
                *, top_k, seq_bits):
    j = pl.program_id(1)
    qb = LANES
    kc = KEY_CHUNK
    hd = HEAD_DIM
    rep = ATT_HEADS // ATT_KV_HEADS
    n_kc = lax.div(j * qb + (qb + kc - 1), kc)
    sub_i = lax.broadcasted_iota(jnp.int32, (kc, qb), 0)
    q_pos = j * qb + lax.broadcasted_iota(jnp.int32, (kc, qb), 1)

    n_pair = lax.div(j * qb + (qb + 2 * kc - 1), 2 * kc)
    last_chunk = 2 * n_pair - 1

    def idx_matmul(c):
        start = pl.multiple_of(c * kc, kc)
        return jnp.dot(ki_ref[pl.ds(start, kc), :], qig_ref[...], preferred_element_type=F32)

    def idx_keys(c, s_all):
        start = pl.multiple_of(c * kc, kc)
        score = jnp.zeros((kc, qb), F32)
        for h in range(IDX_HEADS):
            score = score + jnp.maximum(s_all[:, h * qb:(h + 1) * qb], 0.0) * wg_ref[:, h * qb:(h + 1) * qb]
        causal = (start + sub_i) <= q_pos
        keys_ref[pl.ds(start, kc), :] = jnp.where(causal, _sortable_key(score), jnp.int32(INT_MIN))

    sa_ref[...] = idx_matmul(0)

    def score_pair(p, carry):
        s_b = idx_matmul(2 * p + 1)
        idx_keys(2 * p, sa_ref[...])
        sa_ref[...] = idx_matmul(jnp.minimum(2 * p + 2, last_chunk))
        idx_keys(2 * p + 1, s_b)
        return carry

    lax.fori_loop(0, n_pair, score_pair, 0)

    def count_where(pred_fn):
        def body(c, acc):
            start = pl.multiple_of(c * kc, kc)
            ones = jnp.where(pred_fn(keys_ref[pl.ds(start, kc), :], start), 1.0, 0.0)
            return acc + jnp.sum(ones.reshape(kc // COUNT_ROWS, COUNT_ROWS, qb), axis=0)
        acc = lax.fori_loop(0, n_kc, body, jnp.zeros((COUNT_ROWS, qb), F32))
        return jnp.sum(acc, axis=0, keepdims=True)

    def count_ge(cand):
        def body(p, acc):
            start = pl.multiple_of(p * (2 * kc), 2 * kc)
            ones = jnp.where(keys_ref[pl.ds(start, 2 * kc), :] >= cand, 1.0, 0.0)
            return acc + jnp.sum(ones.reshape(2 * kc // COUNT_ROWS, COUNT_ROWS, qb), axis=0)
        acc = lax.fori_loop(0, n_pair, body, jnp.zeros((COUNT_ROWS, qb), F32))
        return jnp.sum(acc, axis=0, keepdims=True)

    def bit_body(i, carry):
        tau, n_ge = carry
        cand = tau ^ lax.shift_left(jnp.int32(1), 31 - i)
        cnt = count_ge(cand)
        ok = cnt >= top_k
        return jnp.where(ok, cand, tau), jnp.where(ok, cnt, n_ge)

    tau, n_ge = lax.fori_loop(0, 32, bit_body,
                              (jnp.full((1, qb), INT_MIN, jnp.int32), jnp.zeros((1, qb), F32)))
    tau = jnp.maximum(tau, jnp.int32(INT_MIN + 1))

    has_excess = jnp.max(jnp.where(n_ge > top_k, 1.0, 0.0)) > 0.5

    @pl.when(has_excess)
    def _():
        n_gt = count_where(lambda kk, _: kk > tau)
        need = top_k - n_gt

        def pos_body(i, xcut):
            cand = xcut | lax.shift_left(jnp.int32(1), seq_bits - 1 - i)
            cnt = count_where(lambda kk, st: (kk == tau) & ((st + sub_i) < cand))
            return jnp.where(cnt < need, cand, xcut)

        xcut = lax.fori_loop(0, seq_bits, pos_body, jnp.zeros((1, qb), jnp.int32))
        def fix_body(c, carry):
            start = pl.multiple_of(c * kc, kc)
            kk = keys_ref[pl.ds(start, kc), :]
            drop = (kk == tau) & (((start + sub_i) > xcut) | (need < 1))
            keys_ref[pl.ds(start, kc), :] = jnp.where(drop, jnp.int32(INT_MIN), kk)
            return carry

        lax.fori_loop(0, n_kc, fix_body, 0)

    m_ref[...] = jnp.full(m_ref.shape, NEG_BIG, F32)
    s_ref[...] = jnp.zeros(s_ref.shape, F32)
    acc_ref[...] = jnp.zeros(acc_ref.shape, F32)

    def qk_logits(c):
        start = pl.multiple_of(c * kc, kc)
        return [jnp.dot(k_ref[g, pl.ds(start, kc), :], qg_ref[g], preferred_element_type=F32)
                for g in range(ATT_KV_HEADS)]

    def softmax_pv(c, lgs):
        start = pl.multiple_of(c * kc, kc)
        bias = jnp.where(keys_ref[pl.ds(start, kc), :] >= tau, 0.0, NEG_BIG)
        for g in range(ATT_KV_HEADS):
            ps, alphas = [], []
            for hh in range(rep):
                h = g * rep + hh
                logit = lgs[g][:, hh * qb:(hh + 1) * qb] + bias
                m_old = m_ref[h:h + 1, :]
                m_new = jnp.maximum(m_old, jnp.max(logit, axis=0, keepdims=True))
                p = jnp.exp2(logit - m_new)
                alpha = jnp.exp2(m_old - m_new)
                s_ref[h:h + 1, :] = s_ref[h:h + 1, :] * alpha + jnp.sum(p, axis=0, keepdims=True)
                m_ref[h:h + 1, :] = m_new
                ps.append(p.astype(BF16))
                alphas.append(alpha)
            pv = jnp.dot(vt_ref[g * hd:(g + 1) * hd, pl.ds(start, kc)], jnp.concatenate(ps, axis=1),
                         preferred_element_type=F32)
            for hh in range(rep):
                rows = slice((g * rep + hh) * hd, (g * rep + hh + 1) * hd)
                acc_ref[rows, :] = acc_ref[rows, :] * alphas[hh] + pv[:, hh * qb:(hh + 1) * qb]

    for g, lg in enumerate(qk_logits(0)):
        la_ref[g] = lg

    def att_pair(p, carry):
        l_b = qk_logits(2 * p + 1)
        softmax_pv(2 * p, [la_ref[g] for g in range(ATT_KV_HEADS)])
        for g, lg in enumerate(qk_logits(jnp.minimum(2 * p + 2, last_chunk))):
            la_ref[g] = lg
        softmax_pv(2 * p + 1, l_b)
        return carry

    lax.fori_loop(0, n_pair, att_pair, 0)
    for h in range(ATT_HEADS):
        rows = slice(h * hd, (h + 1) * hd)
        acc_ref[rows, :] = acc_ref[rows, :] / s_ref[h:h + 1, :]
    for c0 in range(0, ATT_HEADS * hd, LANES):
        o_ref[:, c0:c0 + LANES] = acc_ref[c0:c0 + LANES, :].T.astype(o_ref.dtype)


def _dsa_attention(qg, qig, wg, k, ki, vt, bsz, seq):
    qb = LANES
    nblk = seq // qb
    qw = ATT_HEADS * HEAD_DIM
    kw = ATT_KV_HEADS * HEAD_DIM
    rep = ATT_HEADS // ATT_KV_HEADS
    top_k = min(TOPK_MAX, seq // 4)
    seq_bits = int(np.log2(seq))
    assert 2 ** seq_bits == seq and seq % (2 * KEY_CHUNK) == 0
    kern = functools.partial(_dsa_kernel, top_k=top_k, seq_bits=seq_bits)
    return pl.pallas_call(
        kern,
        grid=(bsz, nblk),
        in_specs=[pl.BlockSpec((None, ATT_KV_HEADS, HEAD_DIM, rep * qb), lambda b, j: (b, 0, 0, j)),
                  pl.BlockSpec((None, IDX_DIM, IDX_HEADS * qb), lambda b, j: (b, 0, j)),
                  pl.BlockSpec((None, 1, IDX_HEADS * qb), lambda b, j: (b, 0, j)),
                  pl.BlockSpec((None, ATT_KV_HEADS, seq, HEAD_DIM), lambda b, j: (b, 0, 0, 0)),
                  pl.BlockSpec((None, seq, IDX_DIM), lambda b, j: (b, 0, 0)),
                  pl.BlockSpec((None, kw, seq), lambda b, j: (b, 0, 0))],
        out_specs=pl.BlockSpec((qb, qw), lambda b, j: (b * nblk + j, 0)),
        out_shape=jax.ShapeDtypeStruct((bsz * seq, qw), BF16),
        scratch_shapes=[pltpu.VMEM((seq, qb), jnp.int32), pltpu.VMEM((qw, qb), F32),
                        pltpu.VMEM((ATT_HEADS, qb), F32), pltpu.VMEM((ATT_HEADS, qb), F32),
                        pltpu.VMEM((KEY_CHUNK, IDX_HEADS * qb), F32),
                        pltpu.VMEM((ATT_KV_HEADS, KEY_CHUNK, rep * qb), F32)],
        compiler_params=_cparams(2),
        name="dsa_attention",
    )(qg, qig, wg, k, ki, vt)


def _merge_kernel(x_ref, ya_ref, yb_ref, yc_ref, yd_ref, gl_ref, wbr_ref, wout_ref, o_ref):
    d = x_ref.shape[1]
    merged = None
    for n, y_ref in enumerate((ya_ref, yb_ref, yc_ref, yd_ref)):
        u = jnp.dot(y_ref[...], wbr_ref[n], preferred_element_type=F32)
        term = _sigmoid(gl_ref[:, n * d:(n + 1) * d].astype(F32)) * u
        merged = term if merged is None else merged + term
    o_ref[...] = x_ref[...] + jnp.dot(merged.astype(BF16), wout_ref[...], preferred_element_type=F32)


def _merge(x, ya, yb, yc, yd, proj, gate_block, wbr, wout, tm):
    t, d = x.shape
    mixw = ya.shape[1]
    nbr = wbr.shape[0]
    y_spec = pl.BlockSpec((tm, mixw), lambda i: (i, 0))
    return pl.pallas_call(
        _merge_kernel,
        grid=(t // tm,),
        in_specs=[pl.BlockSpec((tm, d), lambda i: (i, 0)), y_spec, y_spec, y_spec, y_spec,
                  pl.BlockSpec((tm, nbr * d), lambda i: (i, gate_block)),
                  pl.BlockSpec(wbr.shape, lambda i: (0, 0, 0)),
                  pl.BlockSpec(wout.shape, lambda i: (0, 0))],
        out_specs=pl.BlockSpec((tm, d), lambda i: (i, 0)),
        out_shape=jax.ShapeDtypeStruct((t, d), F32),
        compiler_params=_cparams(1),
        name="merge_out_proj",
    )(x, ya, yb, yc, yd, proj, wbr, wout)


def _ffn_kernel(x_ref, g_ref, wg_ref, wu_ref, wd_ref, fin_ref, o_ref, h_ref, acc_ref, *, final_norm):
    f = pl.program_id(1)

    @pl.when(f == 0)
    def _():
        h_ref[...] = _rms_rows(x_ref[...], g_ref[...]).astype(h_ref.dtype)
        acc_ref[...] = jnp.zeros(acc_ref.shape, F32)

    h = h_ref[...]
    a = jnp.dot(h, wg_ref[...], preferred_element_type=F32)
    b = jnp.dot(h, wu_ref[...], preferred_element_type=F32)
    t = (a * _sigmoid(a)) * b
    acc_ref[...] += jnp.dot(t.astype(BF16), wd_ref[...], preferred_element_type=F32)

    @pl.when(f == pl.num_programs(1) - 1)
    def _():
        y = x_ref[...] + acc_ref[...]
        o_ref[...] = _rms_rows(y, fin_ref[...]) if final_norm else y


def _ffn(x, g, wg, wu, wd, fin, final_norm, tm, tf):
    t, d = x.shape
    ff = wg.shape[1]
    return pl.pallas_call(
        functools.partial(_ffn_kernel, final_norm=final_norm),
        grid=(t // tm, ff // tf),
        in_specs=[pl.BlockSpec((tm, d), lambda i, f: (i, 0)),
                  pl.BlockSpec((1, d), lambda i, f: (0, 0)),
                  pl.BlockSpec((d, tf), lambda i, f: (0, f)),
                  pl.BlockSpec((d, tf), lambda i, f: (0, f)),
                  pl.BlockSpec((tf, d), lambda i, f: (f, 0)),
                  pl.BlockSpec((1, d), lambda i, f: (0, 0))],
        out_specs=pl.BlockSpec((tm, d), lambda i, f: (i, 0)),
        out_shape=jax.ShapeDtypeStruct((t, d), F32),
        scratch_shapes=[pltpu.VMEM((tm, d), BF16), pltpu.VMEM((tm, d), F32)],
        compiler_params=_cparams(2),
        name="dense_swiglu",
    )(x, g, wg, wu, wd, fin)


def _router_kernel(x_ref, g_ref, r_ref, h_ref, mi_ref, mp_ref, cnt_ref, tri_ref, carry_ref, *, n_experts):
    i = pl.program_id(0)
    tm = x_ref.shape[0]
    lane = lax.broadcasted_iota(jnp.int32, (tm, LANES), 1)

    @pl.when(i == 0)
    def _():
        carry_ref[...] = jnp.zeros(carry_ref.shape, F32)
        r_i = lax.broadcasted_iota(jnp.int32, (tm, tm), 0)
        c_i = lax.broadcasted_iota(jnp.int32, (tm, tm), 1)
        tri_ref[...] = jnp.where(c_i < r_i, 1.0, 0.0).astype(tri_ref.dtype)

    hf = _rms_rows(x_ref[...], g_ref[...])
    h_ref[...] = hf
    logits = jnp.dot(hf, r_ref[...], preferred_element_type=F32, precision=lax.Precision.HIGHEST)
    logits = jnp.where(lane < n_experts, logits, -jnp.inf)
    m1 = jnp.max(logits, axis=-1, keepdims=True)
    i1 = jnp.min(jnp.where(logits == m1, lane, LANES), axis=-1, keepdims=True)
    rest = jnp.where(lane == i1, -jnp.inf, logits)
    m2 = jnp.max(rest, axis=-1, keepdims=True)
    i2 = jnp.min(jnp.where(rest == m2, lane, LANES), axis=-1, keepdims=True)
    e2 = jnp.exp(m2 - m1)
    p1 = 1.0 / (1.0 + e2)
    p2 = e2 / (1.0 + e2)
    oh1 = lane == i1
    oh2 = lane == i2
    ohs = jnp.where(oh1, 1.0, jnp.where(oh2, 1.0, 0.0))
    pref = jnp.dot(tri_ref[...], ohs.astype(tri_ref.dtype), preferred_element_type=F32) + carry_ref[...]
    rank1 = jnp.sum(jnp.where(oh1, pref, 0.0), axis=-1, keepdims=True).astype(jnp.int32)
    rank2 = jnp.sum(jnp.where(oh2, pref, 0.0), axis=-1, keepdims=True).astype(jnp.int32)
    carry_ref[...] = carry_ref[...] + jnp.sum(ohs, axis=0, keepdims=True)
    cnt_ref[...] = carry_ref[...]
    mi_ref[...] = jnp.where(lane == 0, i1, jnp.where(lane == 1, i2, jnp.where(lane == 2, rank1,
                                                                          jnp.where(lane == 3, rank2, 0))))
    mp_ref[...] = jnp.where(lane == 0, p1, jnp.where(lane == 1, p2, 0.0))


def _router(x, g, router, tm):
    t, d = x.shape
    n_experts = router.shape[1]
    rpad = jnp.zeros((d, LANES), F32).at[:, :n_experts].set(router.astype(F32))
    return pl.pallas_call(
        functools.partial(_router_kernel, n_experts=n_experts),
        grid=(t // tm,),
        in_specs=[pl.BlockSpec((tm, d), lambda i: (i, 0)),
                  pl.BlockSpec((1, d), lambda i: (0, 0)),
                  pl.BlockSpec((d, LANES), lambda i: (0, 0))],
        out_specs=[pl.BlockSpec((tm, d), lambda i: (i, 0)),
                   pl.BlockSpec((tm, LANES), lambda i: (i, 0)),
                   pl.BlockSpec((tm, LANES), lambda i: (i, 0)),
                   pl.BlockSpec((1, LANES), lambda i: (0, 0))],
        out_shape=[jax.ShapeDtypeStruct((t, d), F32),
                   jax.ShapeDtypeStruct((t, LANES), jnp.int32),
                   jax.ShapeDtypeStruct((t, LANES), F32),
                   jax.ShapeDtypeStruct((1, LANES), F32)],
        scratch_shapes=[pltpu.VMEM((tm, tm), BF16), pltpu.VMEM((1, LANES), F32)],
        compiler_params=_cparams(1),
        name="moe_router",
    )(x, g, rpad)


DMA_UNROLL = 8


def _row_copy(src, dst, sem):
    return pltpu.make_async_copy(src, dst, sem)


def _dispatch_kernel(dest_ref, h_ref, xs_in_ref, xs_ref, sem, *, n_tokens):
    del xs_in_ref
    td = h_ref.shape[0]
    base = pl.program_id(0) * td

    def issue(r, carry):
        for k in range(TOP_K):
            d = dest_ref[k * n_tokens + base + r]
            _row_copy(h_ref.at[pl.ds(r, 1), :], xs_ref.at[pl.ds(d, 1), :], sem).start(priority=k % 2)
        return carry

    def drain(r, carry):
        for k in range(TOP_K):
            _row_copy(h_ref.at[pl.ds(0, 1), :], xs_ref.at[pl.ds(0, 1), :], sem).wait()
        return carry

    lax.fori_loop(0, td, issue, 0, unroll=DMA_UNROLL)
    lax.fori_loop(0, td, drain, 0, unroll=DMA_UNROLL)


def _dispatch(dest, h, n_rows, td):
    t, d = h.shape
    xs0 = jnp.zeros((n_rows, d), F32)
    return pl.pallas_call(
        functools.partial(_dispatch_kernel, n_tokens=t),
        grid_spec=pltpu.PrefetchScalarGridSpec(
            num_scalar_prefetch=1,
            grid=(t // td,),
            in_specs=[pl.BlockSpec((td, d), lambda i, dest: (i, 0)),
                      pl.BlockSpec(memory_space=pl.ANY)],
            out_specs=pl.BlockSpec(memory_space=pl.ANY),
            scratch_shapes=[pltpu.SemaphoreType.DMA(())]),
        out_shape=jax.ShapeDtypeStruct((n_rows, d), F32),
        input_output_aliases={2: 0},
        compiler_params=_cparams(1),
        name="moe_dispatch",
    )(dest, h, xs0)


def _expert_kernel(te_ref, nu_ref, xs_ref, wg_ref, wu_ref, wd_ref, ys_ref, h_ref, acc_ref):
    i = pl.program_id(0)
    f = pl.program_id(1)
    last_f = pl.num_programs(1) - 1
    valid = i < nu_ref[0]

    @pl.when(valid & (f == 0))
    def _():
        h_ref[...] = xs_ref[...].astype(h_ref.dtype)
        acc_ref[...] = jnp.zeros(acc_ref.shape, F32)

    @pl.when(valid)
    def _():
        h = h_ref[...]
        a = jnp.dot(h, wg_ref[...], preferred_element_type=F32)
        b = jnp.dot(h, wu_ref[...], preferred_element_type=F32)
        t = (a * _sigmoid(a)) * b
        acc_ref[...] += jnp.dot(t.astype(BF16), wd_ref[...], preferred_element_type=F32)

    @pl.when(valid & (f == last_f))
    def _():
        ys_ref[...] = acc_ref[...]

    @pl.when(jnp.logical_not(valid) & (f == last_f))
    def _():
        ys_ref[...] = jnp.zeros(ys_ref.shape, F32)


def _expert_mlp(tile_expert, n_used, xs, wg, wu, wd, tg, tf):
    n_rows, d = xs.shape
    ff = wg.shape[-1]
    nf = ff // tf
    fblk = lambda i, f, te, nu: jnp.where(i < nu[0], f, nf - 1)
    return pl.pallas_call(
        _expert_kernel,
        grid_spec=pltpu.PrefetchScalarGridSpec(
            num_scalar_prefetch=2,
            grid=(n_rows // tg, nf),
            in_specs=[pl.BlockSpec((tg, d), lambda i, f, te, nu: (i, 0)),
                      pl.BlockSpec((None, d, tf), lambda i, f, te, nu: (te[i], 0, fblk(i, f, te, nu))),
                      pl.BlockSpec((None, d, tf), lambda i, f, te, nu: (te[i], 0, fblk(i, f, te, nu))),
                      pl.BlockSpec((None, tf, d), lambda i, f, te, nu: (te[i], fblk(i, f, te, nu), 0))],
            out_specs=pl.BlockSpec((tg, d), lambda i, f, te, nu: (i, 0)),
            scratch_shapes=[pltpu.VMEM((tg, d), BF16), pltpu.VMEM((tg, d), F32)]),
        out_shape=jax.ShapeDtypeStruct((n_rows, d), F32),
        compiler_params=_cparams(2),
        name="moe_experts",
    )(tile_expert, n_used, xs, wg, wu, wd)


def _combine_kernel(dest_ref, x_ref, mp_ref, fin_ref, ys_ref, o_ref, buf_ref, sem, *, n_tokens, final_norm):
    tc = x_ref.shape[0]
    base = pl.program_id(0) * tc

    def issue(r, carry):
        for k in range(TOP_K):
            d = dest_ref[k * n_tokens + base + r]
            _row_copy(ys_ref.at[pl.ds(d, 1), :], buf_ref.at[k, pl.ds(r, 1), :], sem).start(priority=k % 2)
        return carry

    def drain(r, carry):
        for k in range(TOP_K):
            _row_copy(ys_ref.at[pl.ds(0, 1), :], buf_ref.at[k, pl.ds(0, 1), :], sem).wait()
        return carry

    lax.fori_loop(0, tc, issue, 0, unroll=DMA_UNROLL)
    lax.fori_loop(0, tc, drain, 0, unroll=DMA_UNROLL)
    y = x_ref[...]
    for k in range(TOP_K):
        y = y + mp_ref[:, k:k + 1] * buf_ref[k]
    o_ref[...] = _rms_rows(y, fin_ref[...]) if final_norm else y


def _combine(dest, x, mp, fin, ys, final_norm, tc):
    t, d = x.shape
    return pl.pallas_call(
        functools.partial(_combine_kernel, n_tokens=t, final_norm=final_norm),
        grid_spec=pltpu.PrefetchScalarGridSpec(
            num_scalar_prefetch=1,
            grid=(t // tc,),
            in_specs=[pl.BlockSpec((tc, d), lambda i, dest: (i, 0)),
                      pl.BlockSpec((tc, LANES), lambda i, dest: (i, 0)),
                      pl.BlockSpec((1, d), lambda i, dest: (0, 0)),
                      pl.BlockSpec(memory_space=pl.ANY)],
            out_specs=pl.BlockSpec((tc, d), lambda i, dest: (i, 0)),
            scratch_shapes=[pltpu.VMEM((TOP_K, tc, d), F32), pltpu.SemaphoreType.DMA(())]),
        out_shape=jax.ShapeDtypeStruct((t, d), F32),
        compiler_params=_cparams(1),
        name="moe_combine",
    )(dest, x, mp, fin, ys)


def _moe(x, g, router, wg, wu, wd, fin, final_norm, tm, tg, tf):
    t, d = x.shape
    n_experts = router.shape[1]
    h, mi, mp, cnt = _router(x, g, router, tm)
    counts = cnt[0, :n_experts].astype(jnp.int32)
    padded = ((counts + tg - 1) // tg) * tg
    ends = jnp.cumsum(padded)
    offs = ends - padded
    dest = jnp.concatenate([offs[mi[:, k]] + mi[:, TOP_K + k] for k in range(TOP_K)]).astype(jnp.int32)
    n_tiles = (TOP_K * t) // tg + n_experts
    n_used = (ends[-1] // tg).astype(jnp.int32)
    tidx = jnp.minimum(jnp.arange(n_tiles, dtype=jnp.int32), n_used - 1)
    tile_expert = jnp.sum((tidx[:, None] * tg >= ends[None, :]).astype(jnp.int32), axis=1)
    xs = _dispatch(dest, h, n_tiles * tg, _pick_tile(t, 512))
    ys = _expert_mlp(tile_expert, n_used.reshape(1), xs, wg, wu, wd, tg, tf)
    return _combine(dest, x, mp, fin, ys, final_norm, _pick_tile(t, 512))


def _pack_w_in(w_in, d_model, mixw):
    qw = ATT_HEADS * HEAD_DIM
    kvw = 2 * ATT_KV_HEADS * HEAD_DIM
    qiw = IDX_HEADS * IDX_DIM
    bcw = 2 * SSM_GROUPS * SSM_STATE
    sizes = (mixw, qw, kvw, qiw, IDX_DIM, IDX_HEADS, 2 * mixw, mixw, mixw + bcw, SSM_HEADS, 4 * d_model)
    offs = np.concatenate([[0], np.cumsum(sizes)])
    assert offs[-1] == w_in.shape[1]
    seg = lambda i: w_in[:, offs[i]:offs[i + 1]]
    u_pool, q, kv, qi, ki, wi, u_conv, z, xbc, dt, gates = (seg(i) for i in range(len(sizes)))
    d = w_in.shape[0]
    zeros = lambda n: jnp.zeros((d, n), w_in.dtype)
    idx_blk = jnp.concatenate([qi, ki, wi, zeros(3 * LANES - qiw - IDX_DIM - IDX_HEADS),
                               dt, zeros(LANES - SSM_HEADS)], axis=1)
    packed = jnp.concatenate([gates, u_conv, u_pool, q, z, xbc[:, :mixw], idx_blk, kv, xbc[:, mixw:]], axis=1)
    widths = dict(gates=4 * d_model, conv=2 * mixw, pool=mixw, q=qw, z=mixw, xs=mixw, idx=4 * LANES,
                  kv=kvw, bc=bcw)
    blocks, off = {}, 0
    for name in ("gates", "conv", "pool", "q", "z", "xs", "idx", "kv", "bc"):
        assert off % widths[name] == 0
        blocks[name] = off // widths[name]
        off += widths[name]
    blocks["dt"] = 3
    return packed.astype(BF16), blocks


def _pad_lanes(v):
    return jnp.zeros((1, LANES), F32).at[0, :v.shape[0]].set(v.astype(F32))


def _pick_tile(n, target):
    t = min(n, target)
    while n % t:
        t //= 2
    return t


def _pick_lane_tile(n, target):
    best = None
    for m in range(LANES, min(n, target) + 1, LANES):
        if n % m == 0:
            best = m
    return n if best is None else best


def kernel(x, norm_mix, w_in, pool_w, pool_scale, conv_dw, conv_b, conv_ln_g, conv_ln_b, ssm_conv_w, ssm_conv_b, ssm_dt_bias, ssm_a_log, ssm_d, ssm_norm, w_br, w_out, norm_ffn, ffn_w_gate, ffn_w_up, ffn_w_down, moe_router, moe_w_gate, moe_w_up, moe_w_down, final_norm):
    bsz, seq, d = x.shape
    depth = norm_mix.shape[0]
    mixw = pool_scale.shape[-1]
    t = bsz * seq
    row = lambda v: v.reshape(1, -1).astype(F32)

    tab_att = _rope_tables(seq, HEAD_DIM, ROPE_DIM, LANES // HEAD_DIM)
    tab_idx = _rope_tables(seq, IDX_DIM, IDX_ROPE_DIM, LANES // IDX_DIM)
    tab_tail = _rope_tables(seq, IDX_DIM, IDX_ROPE_DIM, 1)

    tm = _pick_tile(t, 1024)
    ts_seq = _pick_tile(seq, 512)
    fin = row(final_norm)

    xf = x.reshape(t, d)
    for layer in range(depth):
        w_packed, blk = _pack_w_in(w_in[layer], d, mixw)
        idx_w = 4 * LANES
        proj, aux = _norm_matmul(xf, row(norm_mix[layer]), w_packed, _pick_tile(t, 2048),
                                 _pick_tile(w_packed.shape[1], 1024), blk["idx"] * idx_w, idx_w)

        y_a = _pool_mixer(proj, blk["pool"], pool_w[layer].astype(BF16), row(pool_scale[layer]), bsz, seq, ts_seq)
        qt, k_r, vt, qit, ki_r, wt = _rope_prep(proj, aux, blk["q"], blk["kv"], tab_att, tab_idx,
                                                tab_tail, bsz, seq, ts_seq)
        y_b = _dsa_attention(qt, qit, wt, k_r, ki_r, vt, bsz, seq)
        y_c = _conformer(proj, blk["conv"], conv_dw[layer], row(conv_b[layer]), row(conv_ln_g[layer]),
                         row(conv_ln_b[layer]), bsz, seq, _pick_tile(seq, 256))
        y_d = _ssd(proj, aux, blk["z"], blk["xs"], blk["bc"], blk["dt"], ssm_conv_w[layer], row(ssm_conv_b[layer]),
                   _pad_lanes(ssm_dt_bias[layer]), _pad_lanes(-jnp.exp(ssm_a_log[layer].astype(F32))),
                   _pad_lanes(ssm_d[layer]), row(ssm_norm[layer]), bsz, seq)
        xf = _merge(xf, y_a, y_b, y_c, y_d, proj, blk["gates"], w_br[layer].astype(BF16),
                    w_out[layer].astype(BF16), _pick_tile(t, 512))

        last = layer == depth - 1
        jj = layer // 2
        if layer % 2 == 0:
            ff = ffn_w_gate.shape[-1]
            xf = _ffn(xf, row(norm_ffn[layer]), ffn_w_gate[jj].astype(BF16), ffn_w_up[jj].astype(BF16),
                      ffn_w_down[jj].astype(BF16), fin, last, _pick_tile(t, 512), _pick_lane_tile(ff, 1408))
        else:
            ff = moe_w_gate.shape[-1]
            xf = _moe(xf, row(norm_ffn[layer]), moe_router[jj], moe_w_gate[jj].astype(BF16),
                      moe_w_up[jj].astype(BF16), moe_w_down[jj].astype(BF16), fin, last, tm,
                      _pick_tile(t, 1024), _pick_lane_tile(ff, 896))
    if depth == 0:
        raise ValueError("depth must be positive")
    return xf.reshape(bsz, seq, d)
```

```python
import functools

import numpy as np
import jax
import jax.numpy as jnp
from jax import lax
from jax.experimental import pallas as pl
from jax.experimental.pallas import tpu as pltpu

POOL_WINDOWS = (2, 4, 8, 16)
ATT_HEADS = 8
ATT_KV_HEADS = 2
HEAD_DIM = 64
ROPE_DIM = HEAD_DIM // 4
ROPE_THETA = 500000.0
IDX_HEADS = 8
IDX_DIM = 32
IDX_ROPE_DIM = IDX_DIM // 4
TOPK_MAX = 256
SSM_HEADS = 8
SSM_HEAD_DIM = 64
SSM_GROUPS = 2
SSM_STATE = 64
SSM_CHUNK = 128
TOP_K = 2
NORM_EPS = 1e-6

LANES = 128
SUBLANES = 8
VMEM_LIMIT_BYTES = 56 * 1024 * 1024

F32 = jnp.float32
BF16 = jnp.bfloat16
INT_MIN = -(2 ** 31)
NEG_BIG = -1e30


def _cparams(n_axes):
    return pltpu.CompilerParams(dimension_semantics=("arbitrary",) * n_axes,
                                vmem_limit_bytes=VMEM_LIMIT_BYTES)


def _sigmoid(x):
    return 1.0 / (1.0 + jnp.exp(-x))


def _rms_rows(x, g):
    return x * lax.rsqrt(jnp.mean(x * x, axis=-1, keepdims=True) + NORM_EPS) * g


def _norm_matmul_kernel(x_ref, g_ref, w_ref, o_ref, aux_ref, h_ref, *, aux_j, aux_off):
    @pl.when(pl.program_id(1) == 0)
    def _():
        h_ref[...] = _rms_rows(x_ref[...], g_ref[...]).astype(h_ref.dtype)

    res = jnp.dot(h_ref[...], w_ref[...], preferred_element_type=F32)
    o_ref[...] = res.astype(o_ref.dtype)

    @pl.when(pl.program_id(1) == aux_j)
    def _():
        aux_ref[...] = res[:, aux_off:aux_off + aux_ref.shape[1]]


def _norm_matmul(x, g, w, tm, tn, aux_col, aux_w):
    t, d = x.shape
    n = w.shape[1]
    assert aux_col // tn == (aux_col + aux_w - 1) // tn
    kern = functools.partial(_norm_matmul_kernel, aux_j=aux_col // tn, aux_off=aux_col % tn)
    return pl.pallas_call(
        kern,
        grid=(t // tm, n // tn),
        in_specs=[pl.BlockSpec((tm, d), lambda i, j: (i, 0)),
                  pl.BlockSpec((1, d), lambda i, j: (0, 0)),
                  pl.BlockSpec((d, tn), lambda i, j: (0, j))],
        out_specs=[pl.BlockSpec((tm, tn), lambda i, j: (i, j)),
                   pl.BlockSpec((tm, aux_w), lambda i, j: (i, 0))],
        out_shape=[jax.ShapeDtypeStruct((t, n), BF16), jax.ShapeDtypeStruct((t, aux_w), F32)],
        scratch_shapes=[pltpu.VMEM((tm, d), BF16)],
        compiler_params=_cparams(2),
        name="norm_in_proj",
    )(x, g, w)


POOL_HIST = 16


def _pool_kernel(u_ref, w_ref, sc_ref, o_ref, buf_ref):
    s = pl.program_id(1)
    ts = u_ref.shape[0]
    gw = w_ref.shape[1]

    @pl.when(s == 0)
    def _():
        buf_ref[0:POOL_HIST, :] = jnp.zeros((POOL_HIST, buf_ref.shape[1]), F32)

    @pl.when(s > 0)
    def _():
        buf_ref[0:POOL_HIST, :] = buf_ref[ts:ts + POOL_HIST, :]

    buf_ref[POOL_HIST:POOL_HIST + ts, :] = u_ref[...].astype(F32)
    pos = s * ts + lax.broadcasted_iota(jnp.int32, (ts, 1), 0)
    for g, win in enumerate(POOL_WINDOWS):
        cols = slice(g * gw, (g + 1) * gw)
        acc = buf_ref[POOL_HIST:POOL_HIST + ts, cols]
        cur = acc
        for k in range(1, win):
            acc = acc + buf_ref[POOL_HIST - k:POOL_HIST - k + ts, cols]
        cnt = jnp.minimum(pos + 1, win).astype(F32)
        p = acc / cnt - cur
        y = jnp.dot(p.astype(BF16), w_ref[g], preferred_element_type=F32)
        o_ref[:, cols] = (y * sc_ref[:, cols]).astype(o_ref.dtype)


def _pool_mixer(proj, col_block, w, scale, bsz, seq, ts):
    mixw = scale.shape[-1]
    nblk = seq // ts
    return pl.pallas_call(
        _pool_kernel,
        grid=(bsz, nblk),
        in_specs=[pl.BlockSpec((ts, mixw), lambda b, s: (b * nblk + s, col_block)),
                  pl.BlockSpec(w.shape, lambda b, s: (0, 0, 0)),
                  pl.BlockSpec((1, mixw), lambda b, s: (0, 0))],
        out_specs=pl.BlockSpec((ts, mixw), lambda b, s: (b * nblk + s, 0)),
        out_shape=jax.ShapeDtypeStruct((bsz * seq, mixw), BF16),
        scratch_shapes=[pltpu.VMEM((POOL_HIST + ts, mixw), F32)],
        compiler_params=_cparams(2),
        name="pool_mixer",
    )(proj, w, scale)


CONV_HIST = 32


def _conformer_kernel(u_ref, dw_ref, db_ref, lg_ref, lb_ref, o_ref, buf_ref, sh_ref, tmp_ref):
    s = pl.program_id(1)
    ts = u_ref.shape[0]
    c = o_ref.shape[1]
    width = dw_ref.shape[0]

    @pl.when(s == 0)
    def _():
        buf_ref[0:CONV_HIST, :] = jnp.zeros((CONV_HIST, c), F32)

    @pl.when(s > 0)
    def _():
        buf_ref[0:CONV_HIST, :] = buf_ref[ts:ts + CONV_HIST, :]

    a = u_ref[:, 0:c].astype(F32)
    gt = u_ref[:, c:2 * c].astype(F32)
    buf_ref[CONV_HIST:CONV_HIST + ts, :] = a * _sigmoid(gt)
    nsh = CONV_HIST + ts - SUBLANES
    for sft in range(1, SUBLANES):
        sh_ref[sft - 1, 0:nsh, :] = buf_ref[sft:sft + nsh, :]
    rc = min(ts, LANES)
    for r0 in range(0, ts, rc):
        for c0 in range(0, c, LANES):
            acc = jnp.zeros((rc, LANES), F32) + db_ref[:, c0:c0 + LANES]
            for k in range(width):
                off = CONV_HIST - (width - 1) + k + r0
                sft = off % SUBLANES
                if sft == 0:
                    tap = buf_ref[off:off + rc, c0:c0 + LANES]
                else:
                    tap = sh_ref[sft - 1, off - sft:off - sft + rc, c0:c0 + LANES]
                acc = acc + tap * dw_ref[k:k + 1, c0:c0 + LANES]
            tmp_ref[r0:r0 + rc, c0:c0 + LANES] = acc
    acc = tmp_ref[...]
    mu = jnp.mean(acc, axis=-1, keepdims=True)
    xc = acc - mu
    y = xc * lax.rsqrt(jnp.mean(xc * xc, axis=-1, keepdims=True) + NORM_EPS)
    y = y * lg_ref[...] + lb_ref[...]
    o_ref[...] = (y * _sigmoid(y)).astype(o_ref.dtype)


def _conformer(proj, col_block, dw, db, lg, lb, bsz, seq, ts):
    c = dw.shape[1]
    nblk = seq // ts
    return pl.pallas_call(
        _conformer_kernel,
        grid=(bsz, nblk),
        in_specs=[pl.BlockSpec((ts, 2 * c), lambda b, s: (b * nblk + s, col_block)),
                  pl.BlockSpec(dw.shape, lambda b, s: (0, 0)),
                  pl.BlockSpec((1, c), lambda b, s: (0, 0)),
                  pl.BlockSpec((1, c), lambda b, s: (0, 0)),
                  pl.BlockSpec((1, c), lambda b, s: (0, 0))],
        out_specs=pl.BlockSpec((ts, c), lambda b, s: (b * nblk + s, 0)),
        out_shape=jax.ShapeDtypeStruct((bsz * seq, c), BF16),
        scratch_shapes=[pltpu.VMEM((CONV_HIST + ts, c), F32),
                        pltpu.VMEM((SUBLANES - 1, CONV_HIST + ts, c), F32),
                        pltpu.VMEM((ts, c), F32)],
        compiler_params=_cparams(2),
        name="conformer_conv",
    )(proj, dw, db, lg, lb)


SSM_HIST = 8


def _ssd_kernel(z_ref, xs_ref, bc_ref, dt_ref, cw_ref, cb_ref, dtb_ref, a_ref, dsk_ref, ng_ref,
                o_ref, y_ref, xbuf_ref, bcbuf_ref, state_ref):
    c_idx = pl.program_id(1)
    L = xs_ref.shape[0]
    mixw = xs_ref.shape[1]
    bcw = bc_ref.shape[1]
    width = cw_ref.shape[0]
    hp = SSM_HEAD_DIM
    ns = SSM_STATE
    rep = SSM_HEADS // SSM_GROUPS

    @pl.when(c_idx == 0)
    def _():
        xbuf_ref[0:SSM_HIST, :] = jnp.zeros((SSM_HIST, mixw), F32)
        bcbuf_ref[0:SSM_HIST, :] = jnp.zeros((SSM_HIST, bcw), F32)
        state_ref[...] = jnp.zeros(state_ref.shape, F32)

    @pl.when(c_idx > 0)
    def _():
        xbuf_ref[0:SSM_HIST, :] = xbuf_ref[L:L + SSM_HIST, :]
        bcbuf_ref[0:SSM_HIST, :] = bcbuf_ref[L:L + SSM_HIST, :]

    xbuf_ref[SSM_HIST:SSM_HIST + L, :] = xs_ref[...].astype(F32)
    bcbuf_ref[SSM_HIST:SSM_HIST + L, :] = bc_ref[...].astype(F32)

    xc = jnp.zeros((L, mixw), F32) + cb_ref[:, 0:mixw]
    bcc = jnp.zeros((L, bcw), F32) + cb_ref[:, mixw:mixw + bcw]
    for k in range(width):
        off = SSM_HIST - (width - 1) + k
        xc = xc + xbuf_ref[off:off + L, :] * cw_ref[k:k + 1, 0:mixw]
        bcc = bcc + bcbuf_ref[off:off + L, :] * cw_ref[k:k + 1, mixw:mixw + bcw]
    xc = xc * _sigmoid(xc)
    bcc = bcc * _sigmoid(bcc)

    dtr = dt_ref[...] + dtb_ref[...]
    dt = jnp.maximum(dtr, 0.0) + jnp.log(1.0 + jnp.exp(-jnp.abs(dtr)))
    da = dt * a_ref[...]
    row_i = lax.broadcasted_iota(jnp.int32, (L, L), 0)
    col_i = lax.broadcasted_iota(jnp.int32, (L, L), 1)
    tri = col_i <= row_i
    a_cum = jnp.dot(tri.astype(F32), da, preferred_element_type=F32,
                    precision=lax.Precision.HIGHEST)
    a_cum_t = a_cum.T
    a_last = a_cum[L - 1:L, :]
    dec_last = jnp.exp(a_last)
    dec_out = jnp.exp(a_cum)
    dec_st = jnp.exp(a_last - a_cum)

    cbs = []
    for g in range(SSM_GROUPS):
        bg = bcc[:, g * ns:(g + 1) * ns].astype(BF16)
        cg = bcc[:, SSM_GROUPS * ns + g * ns:SSM_GROUPS * ns + (g + 1) * ns].astype(BF16)
        cbs.append(lax.dot_general(cg, bg, (((1,), (1,)), ((), ())), preferred_element_type=F32))

    for h in range(SSM_HEADS):
        g = h // rep
        x_h = xc[:, h * hp:(h + 1) * hp]
        b_g = bcc[:, g * ns:(g + 1) * ns]
        c_g = bcc[:, SSM_GROUPS * ns + g * ns:SSM_GROUPS * ns + (g + 1) * ns]
        xd = x_h * dt[:, h:h + 1]
        seg = a_cum[:, h:h + 1] - a_cum_t[h:h + 1, :]
        lmat = jnp.exp(jnp.where(tri, seg, -jnp.inf))
        y_diag = jnp.dot((cbs[g] * lmat).astype(BF16), xd.astype(BF16), preferred_element_type=F32)
        st = state_ref[h]
        y_off = jnp.dot(c_g.astype(BF16), st.astype(BF16), preferred_element_type=F32) * dec_out[:, h:h + 1]
        bd = (b_g * dec_st[:, h:h + 1]).astype(BF16)
        st_new = lax.dot_general(bd, xd.astype(BF16), (((0,), (0,)), ((), ())),
                                 preferred_element_type=F32)
        state_ref[h] = st * dec_last[:, h:h + 1] + st_new
        y_h = y_diag + y_off + dsk_ref[:, h:h + 1] * x_h
        zz = z_ref[:, h * hp:(h + 1) * hp].astype(F32)
        y_ref[:, h * hp:(h + 1) * hp] = y_h * (zz * _sigmoid(zz))

    o_ref[...] = _rms_rows(y_ref[...], ng_ref[...]).astype(o_ref.dtype)


def _ssd(proj, aux, z_block, xs_block, bc_block, dt_block, cw, cb, dtb, a_neg, dsk, ng, bsz, seq):
    mixw = ng.shape[-1]
    bcw = cw.shape[1] - mixw
    L = SSM_CHUNK
    nblk = seq // L
    row = lambda b, s: b * nblk + s
    return pl.pallas_call(
        _ssd_kernel,
        grid=(bsz, nblk),
        in_specs=[pl.BlockSpec((L, mixw), lambda b, s: (row(b, s), z_block)),
                  pl.BlockSpec((L, mixw), lambda b, s: (row(b, s), xs_block)),
                  pl.BlockSpec((L, bcw), lambda b, s: (row(b, s), bc_block)),
                  pl.BlockSpec((L, LANES), lambda b, s: (row(b, s), dt_block)),
                  pl.BlockSpec(cw.shape, lambda b, s: (0, 0)),
                  pl.BlockSpec(cb.shape, lambda b, s: (0, 0)),
                  pl.BlockSpec((1, LANES), lambda b, s: (0, 0)),
                  pl.BlockSpec((1, LANES), lambda b, s: (0, 0)),
                  pl.BlockSpec((1, LANES), lambda b, s: (0, 0)),
                  pl.BlockSpec((1, mixw), lambda b, s: (0, 0))],
        out_specs=pl.BlockSpec((L, mixw), lambda b, s: (row(b, s), 0)),
        out_shape=jax.ShapeDtypeStruct((bsz * seq, mixw), BF16),
        scratch_shapes=[pltpu.VMEM((L, mixw), F32),
                        pltpu.VMEM((SSM_HIST + L, mixw), F32),
                        pltpu.VMEM((SSM_HIST + L, bcw), F32),
                        pltpu.VMEM((SSM_HEADS, SSM_STATE, SSM_HEAD_DIM), F32)],
        compiler_params=_cparams(2),
        name="ssd_mixer",
    )(proj, proj, proj, aux, cw, cb, dtb, a_neg, dsk, ng)


def _rope_cols(x, c, s1, s2, half):
    return (x * c + pltpu.roll(x, LANES - half, axis=1) * s1 + pltpu.roll(x, half, axis=1) * s2)


LOG2E = 1.4426950408889634


def _rope_prep_kernel(q_ref, kv_ref, idx_ref, ta_ref, ti_ref, tt_ref,
                      qg_ref, k_ref, vt_ref, qig_ref, ki_ref, wg_ref):
    hd = HEAD_DIM
    ts = q_ref.shape[0]
    rep = ATT_HEADS // ATT_KV_HEADS
    scale = (hd ** -0.5) * LOG2E
    ca, s1a, s2a = ta_ref[0], ta_ref[1], ta_ref[2]
    heads_per_col = LANES // hd
    for c in range(q_ref.shape[1] // LANES):
        qc = _rope_cols(q_ref[:, c * LANES:(c + 1) * LANES].astype(F32), ca, s1a, s2a, ROPE_DIM // 2) * scale
        qct = qc.T.astype(qg_ref.dtype)
        for r in range(heads_per_col):
            h = c * heads_per_col + r
            g, hh = h // rep, h % rep
            for jb in range(ts // LANES):
                dst = jb * rep * LANES + hh * LANES
                qg_ref[g, :, dst:dst + LANES] = qct[r * hd:(r + 1) * hd, jb * LANES:(jb + 1) * LANES]
    kw = ATT_KV_HEADS * hd
    kc = _rope_cols(kv_ref[:, 0:kw].astype(F32), ca, s1a, s2a, ROPE_DIM // 2)
    for g in range(ATT_KV_HEADS):
        k_ref[g] = kc[:, g * hd:(g + 1) * hd].astype(k_ref.dtype)
    vt_ref[...] = kv_ref[:, kw:2 * kw].astype(F32).T.astype(vt_ref.dtype)
    ci, s1i, s2i = ti_ref[0], ti_ref[1], ti_ref[2]
    nq = IDX_HEADS * IDX_DIM
    iheads_per_col = LANES // IDX_DIM
    for c in range(nq // LANES):
        qc = _rope_cols(idx_ref[:, c * LANES:(c + 1) * LANES], ci, s1i, s2i, IDX_ROPE_DIM // 2)
        qct = qc.T.astype(qig_ref.dtype)
        for r in range(iheads_per_col):
            h = c * iheads_per_col + r
            for jb in range(ts // LANES):
                dst = jb * IDX_HEADS * LANES + h * LANES
                qig_ref[:, dst:dst + LANES] = qct[r * IDX_DIM:(r + 1) * IDX_DIM, jb * LANES:(jb + 1) * LANES]
    tail = _rope_cols(idx_ref[:, nq:nq + LANES], tt_ref[0], tt_ref[1], tt_ref[2], IDX_ROPE_DIM // 2)
    ki_ref[...] = tail[:, 0:IDX_DIM].astype(ki_ref.dtype)
    tail_t = tail.T
    for h in range(IDX_HEADS):
        for jb in range(ts // LANES):
            dst = jb * IDX_HEADS * LANES + h * LANES
            wg_ref[:, dst:dst + LANES] = tail_t[IDX_DIM + h:IDX_DIM + h + 1, jb * LANES:(jb + 1) * LANES]


def _rope_prep(proj, aux, q_block, kv_block, tab_att, tab_idx, tab_tail, bsz, seq, ts):
    nblk = seq // ts
    qw = ATT_HEADS * HEAD_DIM
    kw = ATT_KV_HEADS * HEAD_DIM
    idxw = 4 * LANES
    rep = ATT_HEADS // ATT_KV_HEADS
    row = lambda b, s: b * nblk + s
    tab_spec = pl.BlockSpec((3, ts, LANES), lambda b, s: (0, s, 0))
    return pl.pallas_call(
        _rope_prep_kernel,
        grid=(bsz, nblk),
        in_specs=[pl.BlockSpec((ts, qw), lambda b, s: (row(b, s), q_block)),
                  pl.BlockSpec((ts, 2 * kw), lambda b, s: (row(b, s), kv_block)),
                  pl.BlockSpec((ts, idxw), lambda b, s: (row(b, s), 0)),
                  tab_spec, tab_spec, tab_spec],
        out_specs=[pl.BlockSpec((None, ATT_KV_HEADS, HEAD_DIM, rep * ts), lambda b, s: (b, 0, 0, s)),
                   pl.BlockSpec((None, ATT_KV_HEADS, ts, HEAD_DIM), lambda b, s: (b, 0, s, 0)),
                   pl.BlockSpec((None, kw, ts), lambda b, s: (b, 0, s)),
                   pl.BlockSpec((None, IDX_DIM, IDX_HEADS * ts), lambda b, s: (b, 0, s)),
                   pl.BlockSpec((None, ts, IDX_DIM), lambda b, s: (b, s, 0)),
                   pl.BlockSpec((None, 1, IDX_HEADS * ts), lambda b, s: (b, 0, s))],
        out_shape=[jax.ShapeDtypeStruct((bsz, ATT_KV_HEADS, HEAD_DIM, rep * seq), BF16),
                   jax.ShapeDtypeStruct((bsz, ATT_KV_HEADS, seq, HEAD_DIM), BF16),
                   jax.ShapeDtypeStruct((bsz, kw, seq), BF16),
                   jax.ShapeDtypeStruct((bsz, IDX_DIM, IDX_HEADS * seq), BF16),
                   jax.ShapeDtypeStruct((bsz, seq, IDX_DIM), BF16),
                   jax.ShapeDtypeStruct((bsz, 1, IDX_HEADS * seq), F32)],
        compiler_params=_cparams(2),
        name="rope_prep",
    )(proj, proj, aux, tab_att, tab_idx, tab_tail)


def _rope_tables(seq, head_dim, rot_dim, n_rot_heads):
    half = rot_dim // 2
    inv_freq = jnp.power(jnp.float32(ROPE_THETA), -jnp.arange(half, dtype=F32) * (2.0 / rot_dim))
    ang = jnp.arange(seq, dtype=F32)[:, None] * inv_freq[None, :]
    cos, sin = jnp.cos(ang), jnp.sin(ang)
    lane = np.arange(LANES)
    j = lane % head_dim
    rot_head = lane < n_rot_heads * head_dim
    first = rot_head & (j < half)
    second = rot_head & (j >= half) & (j < rot_dim)
    fi = np.where(first, j, 0)
    si = np.where(second, j - half, 0)
    c = jnp.where(first[None, :], cos[:, fi], jnp.where(second[None, :], cos[:, si], 1.0))
    s1 = jnp.where(first[None, :], -sin[:, fi], 0.0)
    s2 = jnp.where(second[None, :], sin[:, si], 0.0)
    return jnp.stack([c, s1, s2]).astype(F32)


def _sortable_key(score):
    score = jnp.where(score == 0.0, 0.0, score)
    bits = pltpu.bitcast(score, jnp.int32)
    return bits ^ (lax.shift_right_arithmetic(bits, 31) & jnp.int32(0x7FFFFFFF))


KEY_CHUNK = 256
COUNT_ROWS = 4 * SUBLANES


def _dsa_kernel(qg_ref, qig_ref, wg_ref, k_ref, ki_ref, vt_ref, o_ref, keys_ref, acc_ref, m_ref, s_ref,
                sa_ref, la_ref,
                *, top_k, seq_bits):
    j = pl.program_id(1)
    qb = LANES
    kc = KEY_CHUNK
    hd = HEAD_DIM
    rep = ATT_HEADS // ATT_KV_HEADS
    n_kc = lax.div(j * qb + (qb + kc - 1), kc)
    sub_i = lax.broadcasted_iota(jnp.int32, (kc, qb), 0)
    q_pos = j * qb + lax.broadcasted_iota(jnp.int32, (kc, qb), 1)

    n_pair = lax.div(j * qb + (qb + 2 * kc - 1), 2 * kc)
    last_chunk = 2 * n_pair - 1

    def idx_matmul(c):
        start = pl.multiple_of(c * kc, kc)
        return jnp.dot(ki_ref[pl.ds(start, kc), :], qig_ref[...], preferred_element_type=F32)

    def idx_keys(c, s_all):
        start = pl.multiple_of(c * kc, kc)
        score = jnp.zeros((kc, qb), F32)
        for h in range(IDX_HEADS):
            score = score + jnp.maximum(s_all[:, h * qb:(h + 1) * qb], 0.0) * wg_ref[:, h * qb:(h + 1) * qb]
        causal = (start + sub_i) <= q_pos
        keys_ref[pl.ds(start, kc), :] = jnp.where(causal, _sortable_key(score), jnp.int32(INT_MIN))

    def qk_logits(c):
        start = pl.multiple_of(c * kc, kc)
        return [jnp.dot(k_ref[g, pl.ds(start, kc), :], qg_ref[g], preferred_element_type=F32)
                for g in range(ATT_KV_HEADS)]

    sa_ref[...] = idx_matmul(0)
    for g, lg in enumerate(qk_logits(0)):
        la_ref[g] = lg

    def score_pair(p, carry):
        s_b = idx_matmul(2 * p + 1)
        idx_keys(2 * p, sa_ref[...])
        sa_ref[...] = idx_matmul(jnp.minimum(2 * p + 2, last_chunk))
        idx_keys(2 * p + 1, s_b)
        return carry

    lax.fori_loop(0, n_pair, score_pair, 0)

    def count_where(pred_fn):
        def body(c, acc):
            start = pl.multiple_of(c * kc, kc)
            ones = jnp.where(pred_fn(keys_ref[pl.ds(start, kc), :], start), 1.0, 0.0)
            return acc + jnp.sum(ones.reshape(kc // COUNT_ROWS, COUNT_ROWS, qb), axis=0)
        acc = lax.fori_loop(0, n_kc, body, jnp.zeros((COUNT_ROWS, qb), F32))
        return jnp.sum(acc, axis=0, keepdims=True)

    def count_ge(cand):
        def body(p, acc):
            start = pl.multiple_of(p * (2 * kc), 2 * kc)
            ones = jnp.where(keys_ref[pl.ds(start, 2 * kc), :] >= cand, 1.0, 0.0)
            return acc + jnp.sum(ones.reshape(2 * kc // COUNT_ROWS, COUNT_ROWS, qb), axis=0)
        acc = lax.fori_loop(0, n_pair, body, jnp.zeros((COUNT_ROWS, qb), F32))
        return jnp.sum(acc, axis=0, keepdims=True)

    def bit_body(i, carry):
        tau, n_ge = carry
        cand = tau ^ lax.shift_left(jnp.int32(1), 31 - i)
        cnt = count_ge(cand)
        ok = cnt >= top_k
        return jnp.where(ok, cand, tau), jnp.where(ok, cnt, n_ge)

    tau, n_ge = lax.fori_loop(0, 32, bit_body,
                              (jnp.full((1, qb), INT_MIN, jnp.int32), jnp.zeros((1, qb), F32)))
    tau = jnp.maximum(tau, jnp.int32(INT_MIN + 1))

    has_excess = jnp.max(jnp.where(n_ge > top_k, 1.0, 0.0)) > 0.5

    @pl.when(has_excess)
    def _():
        n_gt = count_where(lambda kk, _: kk > tau)
        need = top_k - n_gt

        def pos_body(i, xcut):
            cand = xcut | lax.shift_left(jnp.int32(1), seq_bits - 1 - i)
            cnt = count_where(lambda kk, st: (kk == tau) & ((st + sub_i) < cand))
            return jnp.where(cnt < need, cand, xcut)

        xcut = lax.fori_loop(0, seq_bits, pos_body, jnp.zeros((1, qb), jnp.int32))
        def fix_body(c, carry):
            start = pl.multiple_of(c * kc, kc)
            kk = keys_ref[pl.ds(start, kc), :]
            drop = (kk == tau) & (((start + sub_i) > xcut) | (need < 1))
            keys_ref[pl.ds(start, kc), :] = jnp.where(drop, jnp.int32(INT_MIN), kk)
            return carry

        lax.fori_loop(0, n_kc, fix_body, 0)

    m_ref[...] = jnp.full(m_ref.shape, NEG_BIG, F32)
    s_ref[...] = jnp.zeros(s_ref.shape, F32)
    acc_ref[...] = jnp.zeros(acc_ref.shape, F32)

    def softmax_pv(c, lgs):
        start = pl.multiple_of(c * kc, kc)
        bias = jnp.where(keys_ref[pl.ds(start, kc), :] >= tau, 0.0, NEG_BIG)
        for g in range(ATT_KV_HEADS):
            ps, alphas = [], []
            for hh in range(rep):
                h = g * rep + hh
                logit = lgs[g][:, hh * qb:(hh + 1) * qb] + bias
                m_old = m_ref[h:h + 1, :]
                m_new = jnp.maximum(m_old, jnp.max(logit, axis=0, keepdims=True))
                p = jnp.exp2(logit - m_new)
                alpha = jnp.exp2(m_old - m_new)
                s_ref[h:h + 1, :] = s_ref[h:h + 1, :] * alpha + jnp.sum(p, axis=0, keepdims=True)
                m_ref[h:h + 1, :] = m_new
                ps.append(p.astype(BF16))
                alphas.append(alpha)
            pv = jnp.dot(vt_ref[g * hd:(g + 1) * hd, pl.ds(start, kc)], jnp.concatenate(ps, axis=1),
                         preferred_element_type=F32)
            for hh in range(rep):
                rows = slice((g * rep + hh) * hd, (g * rep + hh + 1) * hd)
                acc_ref[rows, :] = acc_ref[rows, :] * alphas[hh] + pv[:, hh * qb:(hh + 1) * qb]

    def att_pair(p, carry):
        l_b = qk_logits(2 * p + 1)
        softmax_pv(2 * p, [la_ref[g] for g in range(ATT_KV_HEADS)])
        for g, lg in enumerate(qk_logits(jnp.minimum(2 * p + 2, last_chunk))):
            la_ref[g] = lg
        softmax_pv(2 * p + 1, l_b)
        return carry

    lax.fori_loop(0, n_pair, att_pair, 0)
    for h in range(ATT_HEADS):
        rows = slice(h * hd, (h + 1) * hd)
        acc_ref[rows, :] = acc_ref[rows, :] / s_ref[h:h + 1, :]
    for c0 in range(0, ATT_HEADS * hd, LANES):
        o_ref[:, c0:c0 + LANES] = acc_ref[c0:c0 + LANES, :].T.astype(o_ref.dtype)


def _dsa_attention(qg, qig, wg, k, ki, vt, bsz, seq):
    qb = LANES
    nblk = seq // qb
    qw = ATT_HEADS * HEAD_DIM
    kw = ATT_KV_HEADS * HEAD_DIM
    rep = ATT_HEADS // ATT_KV_HEADS
    top_k = min(TOPK_MAX, seq // 4)
    seq_bits = int(np.log2(seq))
    assert 2 ** seq_bits == seq and seq % (2 * KEY_CHUNK) == 0
    kern = functools.partial(_dsa_kernel, top_k=top_k, seq_bits=seq_bits)
    return pl.pallas_call(
        kern,
        grid=(bsz, nblk),
        in_specs=[pl.BlockSpec((None, ATT_KV_HEADS, HEAD_DIM, rep * qb), lambda b, j: (b, 0, 0, j)),
                  pl.BlockSpec((None, IDX_DIM, IDX_HEADS * qb), lambda b, j: (b, 0, j)),
                  pl.BlockSpec((None, 1, IDX_HEADS * qb), lambda b, j: (b, 0, j)),
                  pl.BlockSpec((None, ATT_KV_HEADS, seq, HEAD_DIM), lambda b, j: (b, 0, 0, 0)),
                  pl.BlockSpec((None, seq, IDX_DIM), lambda b, j: (b, 0, 0)),
                  pl.BlockSpec((None, kw, seq), lambda b, j: (b, 0, 0))],
        out_specs=pl.BlockSpec((qb, qw), lambda b, j: (b * nblk + j, 0)),
        out_shape=jax.ShapeDtypeStruct((bsz * seq, qw), BF16),
        scratch_shapes=[pltpu.VMEM((seq, qb), jnp.int32), pltpu.VMEM((qw, qb), F32),
                        pltpu.VMEM((ATT_HEADS, qb), F32), pltpu.VMEM((ATT_HEADS, qb), F32),
                        pltpu.VMEM((KEY_CHUNK, IDX_HEADS * qb), F32),
                        pltpu.VMEM((ATT_KV_HEADS, KEY_CHUNK, rep * qb), F32)],
        compiler_params=_cparams(2),
        name="dsa_attention",
    )(qg, qig, wg, k, ki, vt)


def _merge_kernel(x_ref, ya_ref, yb_ref, yc_ref, yd_ref, gl_ref, wbr_ref, wout_ref, o_ref):
    d = x_ref.shape[1]
    merged = None
    for n, y_ref in enumerate((ya_ref, yb_ref, yc_ref, yd_ref)):
        u = jnp.dot(y_ref[...], wbr_ref[n], preferred_element_type=F32)
        term = _sigmoid(gl_ref[:, n * d:(n + 1) * d].astype(F32)) * u
        merged = term if merged is None else merged + term
    o_ref[...] = x_ref[...] + jnp.dot(merged.astype(BF16), wout_ref[...], preferred_element_type=F32)


def _merge(x, ya, yb, yc, yd, proj, gate_block, wbr, wout, tm):
    t, d = x.shape
    mixw = ya.shape[1]
    nbr = wbr.shape[0]
    y_spec = pl.BlockSpec((tm, mixw), lambda i: (i, 0))
    return pl.pallas_call(
        _merge_kernel,
        grid=(t // tm,),
        in_specs=[pl.BlockSpec((tm, d), lambda i: (i, 0)), y_spec, y_spec, y_spec, y_spec,
                  pl.BlockSpec((tm, nbr * d), lambda i: (i, gate_block)),
                  pl.BlockSpec(wbr.shape, lambda i: (0, 0, 0)),
                  pl.BlockSpec(wout.shape, lambda i: (0, 0))],
        out_specs=pl.BlockSpec((tm, d), lambda i: (i, 0)),
        out_shape=jax.ShapeDtypeStruct((t, d), F32),
        compiler_params=_cparams(1),
        name="merge_out_proj",
    )(x, ya, yb, yc, yd, proj, wbr, wout)


def _ffn_kernel(x_ref, g_ref, wg_ref, wu_ref, wd_ref, fin_ref, o_ref, h_ref, acc_ref, *, final_norm):
    f = pl.program_id(1)

    @pl.when(f == 0)
    def _():
        h_ref[...] = _rms_rows(x_ref[...], g_ref[...]).astype(h_ref.dtype)
        acc_ref[...] = jnp.zeros(acc_ref.shape, F32)

    h = h_ref[...]
    a = jnp.dot(h, wg_ref[...], preferred_element_type=F32)
    b = jnp.dot(h, wu_ref[...], preferred_element_type=F32)
    t = (a * _sigmoid(a)) * b
    acc_ref[...] += jnp.dot(t.astype(BF16), wd_ref[...], preferred_element_type=F32)

    @pl.when(f == pl.num_programs(1) - 1)
    def _():
        y = x_ref[...] + acc_ref[...]
        o_ref[...] = _rms_rows(y, fin_ref[...]) if final_norm else y


def _ffn(x, g, wg, wu, wd, fin, final_norm, tm, tf):
    t, d = x.shape
    ff = wg.shape[1]
    return pl.pallas_call(
        functools.partial(_ffn_kernel, final_norm=final_norm),
        grid=(t // tm, ff // tf),
        in_specs=[pl.BlockSpec((tm, d), lambda i, f: (i, 0)),
                  pl.BlockSpec((1, d), lambda i, f: (0, 0)),
                  pl.BlockSpec((d, tf), lambda i, f: (0, f)),
                  pl.BlockSpec((d, tf), lambda i, f: (0, f)),
                  pl.BlockSpec((tf, d), lambda i, f: (f, 0)),
                  pl.BlockSpec((1, d), lambda i, f: (0, 0))],
        out_specs=pl.BlockSpec((tm, d), lambda i, f: (i, 0)),
        out_shape=jax.ShapeDtypeStruct((t, d), F32),
        scratch_shapes=[pltpu.VMEM((tm, d), BF16), pltpu.VMEM((tm, d), F32)],
        compiler_params=_cparams(2),
        name="dense_swiglu",
    )(x, g, wg, wu, wd, fin)


def _router_kernel(x_ref, g_ref, r_ref, h_ref, mi_ref, mp_ref, cnt_ref, tri_ref, carry_ref, *, n_experts):
    i = pl.program_id(0)
    tm = x_ref.shape[0]
    lane = lax.broadcasted_iota(jnp.int32, (tm, LANES), 1)

    @pl.when(i == 0)
    def _():
        carry_ref[...] = jnp.zeros(carry_ref.shape, F32)
        r_i = lax.broadcasted_iota(jnp.int32, (tm, tm), 0)
        c_i = lax.broadcasted_iota(jnp.int32, (tm, tm), 1)
        tri_ref[...] = jnp.where(c_i < r_i, 1.0, 0.0).astype(tri_ref.dtype)

    hf = _rms_rows(x_ref[...], g_ref[...])
    h_ref[...] = hf
    logits = jnp.dot(hf, r_ref[...], preferred_element_type=F32, precision=lax.Precision.HIGHEST)
    logits = jnp.where(lane < n_experts, logits, -jnp.inf)
    m1 = jnp.max(logits, axis=-1, keepdims=True)
    i1 = jnp.min(jnp.where(logits == m1, lane, LANES), axis=-1, keepdims=True)
    rest = jnp.where(lane == i1, -jnp.inf, logits)
    m2 = jnp.max(rest, axis=-1, keepdims=True)
    i2 = jnp.min(jnp.where(rest == m2, lane, LANES), axis=-1, keepdims=True)
    e2 = jnp.exp(m2 - m1)
    p1 = 1.0 / (1.0 + e2)
    p2 = e2 / (1.0 + e2)
    oh1 = lane == i1
    oh2 = lane == i2
    ohs = jnp.where(oh1, 1.0, jnp.where(oh2, 1.0, 0.0))
    pref = jnp.dot(tri_ref[...], ohs.astype(tri_ref.dtype), preferred_element_type=F32) + carry_ref[...]
    rank1 = jnp.sum(jnp.where(oh1, pref, 0.0), axis=-1, keepdims=True).astype(jnp.int32)
    rank2 = jnp.sum(jnp.where(oh2, pref, 0.0), axis=-1, keepdims=True).astype(jnp.int32)
    carry_ref[...] = carry_ref[...] + jnp.sum(ohs, axis=0, keepdims=True)
    cnt_ref[...] = carry_ref[...]
    mi_ref[...] = jnp.where(lane == 0, i1, jnp.where(lane == 1, i2, jnp.where(lane == 2, rank1,
                                                                          jnp.where(lane == 3, rank2, 0))))
    mp_ref[...] = jnp.where(lane == 0, p1, jnp.where(lane == 1, p2, 0.0))


def _router(x, g, router, tm):
    t, d = x.shape
    n_experts = router.shape[1]
    rpad = jnp.zeros((d, LANES), F32).at[:, :n_experts].set(router.astype(F32))
    return pl.pallas_call(
        functools.partial(_router_kernel, n_experts=n_experts),
        grid=(t // tm,),
        in_specs=[pl.BlockSpec((tm, d), lambda i: (i, 0)),
                  pl.BlockSpec((1, d), lambda i: (0, 0)),
                  pl.BlockSpec((d, LANES), lambda i: (0, 0))],
        out_specs=[pl.BlockSpec((tm, d), lambda i: (i, 0)),
                   pl.BlockSpec((tm, LANES), lambda i: (i, 0)),
                   pl.BlockSpec((tm, LANES), lambda i: (i, 0)),
                   pl.BlockSpec((1, LANES), lambda i: (0, 0))],
        out_shape=[jax.ShapeDtypeStruct((t, d), F32),
                   jax.ShapeDtypeStruct((t, LANES), jnp.int32),
                   jax.ShapeDtypeStruct((t, LANES), F32),
                   jax.ShapeDtypeStruct((1, LANES), F32)],
        scratch_shapes=[pltpu.VMEM((tm, tm), BF16), pltpu.VMEM((1, LANES), F32)],
        compiler_params=_cparams(1),
        name="moe_router",
    )(x, g, rpad)


DMA_UNROLL = 8


def _row_copy(src, dst, sem):
    return pltpu.make_async_copy(src, dst, sem)


def _dispatch_kernel(dest_ref, h_ref, xs_in_ref, xs_ref, sem, *, n_tokens):
    del xs_in_ref
    td = h_ref.shape[0]
    base = pl.program_id(0) * td

    def issue(r, carry):
        for k in range(TOP_K):
            d = dest_ref[k * n_tokens + base + r]
            _row_copy(h_ref.at[pl.ds(r, 1), :], xs_ref.at[pl.ds(d, 1), :], sem).start(priority=k % 2)
        return carry

    def drain(r, carry):
        for k in range(TOP_K):
            _row_copy(h_ref.at[pl.ds(0, 1), :], xs_ref.at[pl.ds(0, 1), :], sem).wait()
        return carry

    lax.fori_loop(0, td, issue, 0, unroll=DMA_UNROLL)
    lax.fori_loop(0, td, drain, 0, unroll=DMA_UNROLL)


def _dispatch(dest, h, n_rows, td):
    t, d = h.shape
    xs0 = jnp.zeros((n_rows, d), F32)
    return pl.pallas_call(
        functools.partial(_dispatch_kernel, n_tokens=t),
        grid_spec=pltpu.PrefetchScalarGridSpec(
            num_scalar_prefetch=1,
            grid=(t // td,),
            in_specs=[pl.BlockSpec((td, d), lambda i, dest: (i, 0)),
                      pl.BlockSpec(memory_space=pl.ANY)],
            out_specs=pl.BlockSpec(memory_space=pl.ANY),
            scratch_shapes=[pltpu.SemaphoreType.DMA(())]),
        out_shape=jax.ShapeDtypeStruct((n_rows, d), F32),
        input_output_aliases={2: 0},
        compiler_params=_cparams(1),
        name="moe_dispatch",
    )(dest, h, xs0)


def _expert_kernel(te_ref, nu_ref, xs_ref, wg_ref, wu_ref, wd_ref, ys_ref, h_ref, acc_ref):
    i = pl.program_id(0)
    f = pl.program_id(1)
    last_f = pl.num_programs(1) - 1
    valid = i < nu_ref[0]

    @pl.when(valid & (f == 0))
    def _():
        h_ref[...] = xs_ref[...].astype(h_ref.dtype)
        acc_ref[...] = jnp.zeros(acc_ref.shape, F32)

    @pl.when(valid)
    def _():
        h = h_ref[...]
        a = jnp.dot(h, wg_ref[...], preferred_element_type=F32)
        b = jnp.dot(h, wu_ref[...], preferred_element_type=F32)
        t = (a * _sigmoid(a)) * b
        acc_ref[...] += jnp.dot(t.astype(BF16), wd_ref[...], preferred_element_type=F32)

    @pl.when(valid & (f == last_f))
    def _():
        ys_ref[...] = acc_ref[...]

    @pl.when(jnp.logical_not(valid) & (f == last_f))
    def _():
        ys_ref[...] = jnp.zeros(ys_ref.shape, F32)


def _expert_mlp(tile_expert, n_used, xs, wg, wu, wd, tg, tf):
    n_rows, d = xs.shape
    ff = wg.shape[-1]
    nf = ff // tf
    fblk = lambda i, f, te, nu: jnp.where(i < nu[0], f, nf - 1)
    return pl.pallas_call(
        _expert_kernel,
        grid_spec=pltpu.PrefetchScalarGridSpec(
            num_scalar_prefetch=2,
            grid=(n_rows // tg, nf),
            in_specs=[pl.BlockSpec((tg, d), lambda i, f, te, nu: (i, 0)),
                      pl.BlockSpec((None, d, tf), lambda i, f, te, nu: (te[i], 0, fblk(i, f, te, nu))),
                      pl.BlockSpec((None, d, tf), lambda i, f, te, nu: (te[i], 0, fblk(i, f, te, nu))),
                      pl.BlockSpec((None, tf, d), lambda i, f, te, nu: (te[i], fblk(i, f, te, nu), 0))],
            out_specs=pl.BlockSpec((tg, d), lambda i, f, te, nu: (i, 0)),
            scratch_shapes=[pltpu.VMEM((tg, d), BF16), pltpu.VMEM((tg, d), F32)]),
        out_shape=jax.ShapeDtypeStruct((n_rows, d), F32),
        compiler_params=_cparams(2),
        name="moe_experts",
    )(tile_expert, n_used, xs, wg, wu, wd)


def _combine_kernel(dest_ref, x_ref, mp_ref, fin_ref, ys_ref, o_ref, buf_ref, sem, *, n_tokens, final_norm):
    tc = x_ref.shape[0]
    base = pl.program_id(0) * tc

    def issue(r, carry):
        for k in range(TOP_K):
            d = dest_ref[k * n_tokens + base + r]
            _row_copy(ys_ref.at[pl.ds(d, 1), :], buf_ref.at[k, pl.ds(r, 1), :], sem).start(priority=k % 2)
        return carry

    def drain(r, carry):
        for k in range(TOP_K):
            _row_copy(ys_ref.at[pl.ds(0, 1), :], buf_ref.at[k, pl.ds(0, 1), :], sem).wait()
        return carry

    lax.fori_loop(0, tc, issue, 0, unroll=DMA_UNROLL)
    lax.fori_loop(0, tc, drain, 0, unroll=DMA_UNROLL)
    y = x_ref[...]
    for k in range(TOP_K):
        y = y + mp_ref[:, k:k + 1] * buf_ref[k]
    o_ref[...] = _rms_rows(y, fin_ref[...]) if final_norm else y


def _combine(dest, x, mp, fin, ys, final_norm, tc):
    t, d = x.shape
    return pl.pallas_call(
        functools.partial(_combine_kernel, n_tokens=t, final_norm=final_norm),
        grid_spec=pltpu.PrefetchScalarGridSpec(
            num_scalar_prefetch=1,
            grid=(t // tc,),
            in_specs=[pl.BlockSpec((tc, d), lambda i, dest: (i, 0)),
                      pl.BlockSpec((tc, LANES), lambda i, dest: (i, 0)),
                      pl.BlockSpec((1, d), lambda i, dest: (0, 0)),
                      pl.BlockSpec(memory_space=pl.ANY)],
            out_specs=pl.BlockSpec((tc, d), lambda i, dest: (i, 0)),
            scratch_shapes=[pltpu.VMEM((TOP_K, tc, d), F32), pltpu.SemaphoreType.DMA(())]),
        out_shape=jax.ShapeDtypeStruct((t, d), F32),
        compiler_params=_cparams(1),
        name="moe_combine",
    )(dest, x, mp, fin, ys)


def _moe(x, g, router, wg, wu, wd, fin, final_norm, tm, tg, tf):
    t, d = x.shape
    n_experts = router.shape[1]
    h, mi, mp, cnt = _router(x, g, router, tm)
    counts = cnt[0, :n_experts].astype(jnp.int32)
    padded = ((counts + tg - 1) // tg) * tg
    ends = jnp.cumsum(padded)
    offs = ends - padded
    dest = jnp.concatenate([offs[mi[:, k]] + mi[:, TOP_K + k] for k in range(TOP_K)]).astype(jnp.int32)
    n_tiles = (TOP_K * t) // tg + n_experts
    n_used = (ends[-1] // tg).astype(jnp.int32)
    tidx = jnp.minimum(jnp.arange(n_tiles, dtype=jnp.int32), n_used - 1)
    tile_expert = jnp.sum((tidx[:, None] * tg >= ends[None, :]).astype(jnp.int32), axis=1)
    xs = _dispatch(dest, h, n_tiles * tg, _pick_tile(t, 512))
    ys = _expert_mlp(tile_expert, n_used.reshape(1), xs, wg, wu, wd, tg, tf)
    return _combine(dest, x, mp, fin, ys, final_norm, _pick_tile(t, 512))


def _pack_w_in(w_in, d_model, mixw):
    qw = ATT_HEADS * HEAD_DIM
    kvw = 2 * ATT_KV_HEADS * HEAD_DIM
    qiw = IDX_HEADS * IDX_DIM
    bcw = 2 * SSM_GROUPS * SSM_STATE
    sizes = (mixw, qw, kvw, qiw, IDX_DIM, IDX_HEADS, 2 * mixw, mixw, mixw + bcw, SSM_HEADS, 4 * d_model)
    offs = np.concatenate([[0], np.cumsum(sizes)])
    assert offs[-1] == w_in.shape[1]
    seg = lambda i: w_in[:, offs[i]:offs[i + 1]]
    u_pool, q, kv, qi, ki, wi, u_conv, z, xbc, dt, gates = (seg(i) for i in range(len(sizes)))
    d = w_in.shape[0]
    zeros = lambda n: jnp.zeros((d, n), w_in.dtype)
    idx_blk = jnp.concatenate([qi, ki, wi, zeros(3 * LANES - qiw - IDX_DIM - IDX_HEADS),
                               dt, zeros(LANES - SSM_HEADS)], axis=1)
    packed = jnp.concatenate([gates, u_conv, u_pool, q, z, xbc[:, :mixw], idx_blk, kv, xbc[:, mixw:]], axis=1)
    widths = dict(gates=4 * d_model, conv=2 * mixw, pool=mixw, q=qw, z=mixw, xs=mixw, idx=4 * LANES,
                  kv=kvw, bc=bcw)
    blocks, off = {}, 0
    for name in ("gates", "conv", "pool", "q", "z", "xs", "idx", "kv", "bc"):
        assert off % widths[name] == 0
        blocks[name] = off // widths[name]
        off += widths[name]
    blocks["dt"] = 3
    return packed.astype(BF16), blocks


def _pad_lanes(v):
    return jnp.zeros((1, LANES), F32).at[0, :v.shape[0]].set(v.astype(F32))


def _pick_tile(n, target):
    t = min(n, target)
    while n % t:
        t //= 2
    return t


def _pick_lane_tile(n, target):
    best = None
    for m in range(LANES, min(n, target) + 1, LANES):
        if n % m == 0:
            best = m
    return n if best is None else best


def kernel(x, norm_mix, w_in, pool_w, pool_scale, conv_dw, conv_b, conv_ln_g, conv_ln_b, ssm_conv_w, ssm_conv_b, ssm_dt_bias, ssm_a_log, ssm_d, ssm_norm, w_br, w_out, norm_ffn, ffn_w_gate, ffn_w_up, ffn_w_down, moe_router, moe_w_gate, moe_w_up, moe_w_down, final_norm):
    bsz, seq, d = x.shape
    depth = norm_mix.shape[0]
    mixw = pool_scale.shape[-1]
    t = bsz * seq
    row = lambda v: v.reshape(1, -1).astype(F32)

    tab_att = _rope_tables(seq, HEAD_DIM, ROPE_DIM, LANES // HEAD_DIM)
    tab_idx = _rope_tables(seq, IDX_DIM, IDX_ROPE_DIM, LANES // IDX_DIM)
    tab_tail = _rope_tables(seq, IDX_DIM, IDX_ROPE_DIM, 1)

    tm = _pick_tile(t, 1024)
    ts_seq = _pick_tile(seq, 512)
    fin = row(final_norm)

    xf = x.reshape(t, d)
    for layer in range(depth):
        w_packed, blk = _pack_w_in(w_in[layer], d, mixw)
        idx_w = 4 * LANES
        proj, aux = _norm_matmul(xf, row(norm_mix[layer]), w_packed, _pick_tile(t, 2048),
                                 _pick_tile(w_packed.shape[1], 1024), blk["idx"] * idx_w, idx_w)

        y_a = _pool_mixer(proj, blk["pool"], pool_w[layer].astype(BF16), row(pool_scale[layer]), bsz, seq, ts_seq)
        qt, k_r, vt, qit, ki_r, wt = _rope_prep(proj, aux, blk["q"], blk["kv"], tab_att, tab_idx,
                                                tab_tail, bsz, seq, ts_seq)
        y_b = _dsa_attention(qt, qit, wt, k_r, ki_r, vt, bsz, seq)
        y_c = _conformer(proj, blk["conv"], conv_dw[layer], row(conv_b[layer]), row(conv_ln_g[layer]),
                         row(conv_ln_b[layer]), bsz, seq, _pick_tile(seq, 256))
        y_d = _ssd(proj, aux, blk["z"], blk["xs"], blk["bc"], blk["dt"], ssm_conv_w[layer], row(ssm_conv_b[layer]),
                   _pad_lanes(ssm_dt_bias[layer]), _pad_lanes(-jnp.exp(ssm_a_log[layer].astype(F32))),
                   _pad_lanes(ssm_d[layer]), row(ssm_norm[layer]), bsz, seq)
        xf = _merge(xf, y_a, y_b, y_c, y_d, proj, blk["gates"], w_br[layer].astype(BF16),
                    w_out[layer].astype(BF16), _pick_tile(t, 512))

        last = layer == depth - 1
        jj = layer // 2
        if layer % 2 == 0:
            ff = ffn_w_gate.shape[-1]
            xf = _ffn(xf, row(norm_ffn[layer]), ffn_w_gate[jj].astype(BF16), ffn_w_up[jj].astype(BF16),
                      ffn_w_down[jj].astype(BF16), fin, last, _pick_tile(t, 512), _pick_lane_tile(ff, 1408))
        else:
            ff = moe_w_gate.shape[-1]
            xf = _moe(xf, row(norm_ffn[layer]), moe_router[jj], moe_w_gate[jj].astype(BF16),
                      moe_w_up[jj].astype(BF16), moe_w_down[jj].astype(BF16), fin, last, tm,
                      _pick_tile(t, 1024), _pick_lane_tile(ff, 896))
    if depth == 0:
        raise ValueError("depth must be positive")
    return xf.reshape(bsz, seq, d)
```

```python
import functools

import numpy as np
import jax
import jax.numpy as jnp
from jax import lax
from jax.experimental import pallas as pl
from jax.experimental.pallas import tpu as pltpu

POOL_WINDOWS = (2, 4, 8, 16)
ATT_HEADS = 8
ATT_KV_HEADS = 2
HEAD_DIM = 64
ROPE_DIM = HEAD_DIM // 4
ROPE_THETA = 500000.0
IDX_HEADS = 8
IDX_DIM = 32
IDX_ROPE_DIM = IDX_DIM // 4
TOPK_MAX = 256
SSM_HEADS = 8
SSM_HEAD_DIM = 64
SSM_GROUPS = 2
SSM_STATE = 64
SSM_CHUNK = 128
TOP_K = 2
NORM_EPS = 1e-6

LANES = 128
SUBLANES = 8
VMEM_LIMIT_BYTES = 56 * 1024 * 1024

F32 = jnp.float32
BF16 = jnp.bfloat16
INT_MIN = -(2 ** 31)
NEG_BIG = -1e30


def _cparams(n_axes):
    return pltpu.CompilerParams(dimension_semantics=("arbitrary",) * n_axes,
                                vmem_limit_bytes=VMEM_LIMIT_BYTES)


def _sigmoid(x):
    return 1.0 / (1.0 + jnp.exp(-x))


def _rms_rows(x, g):
    return x * lax.rsqrt(jnp.mean(x * x, axis=-1, keepdims=True) + NORM_EPS) * g


def _norm_matmul_kernel(x_ref, g_ref, w_ref, o_ref, aux_ref, h_ref, *, aux_j, aux_off):
    @pl.when(pl.program_id(1) == 0)
    def _():
        h_ref[...] = _rms_rows(x_ref[...], g_ref[...]).astype(h_ref.dtype)

    res = jnp.dot(h_ref[...], w_ref[...], preferred_element_type=F32)
    o_ref[...] = res.astype(o_ref.dtype)

    @pl.when(pl.program_id(1) == aux_j)
    def _():
        aux_ref[...] = res[:, aux_off:aux_off + aux_ref.shape[1]]


def _norm_matmul(x, g, w, tm, tn, aux_col, aux_w):
    t, d = x.shape
    n = w.shape[1]
    assert aux_col // tn == (aux_col + aux_w - 1) // tn
    kern = functools.partial(_norm_matmul_kernel, aux_j=aux_col // tn, aux_off=aux_col % tn)
    return pl.pallas_call(
        kern,
        grid=(t // tm, n // tn),
        in_specs=[pl.BlockSpec((tm, d), lambda i, j: (i, 0)),
                  pl.BlockSpec((1, d), lambda i, j: (0, 0)),
                  pl.BlockSpec((d, tn), lambda i, j: (0, j))],
        out_specs=[pl.BlockSpec((tm, tn), lambda i, j: (i, j)),
                   pl.BlockSpec((tm, aux_w), lambda i, j: (i, 0))],
        out_shape=[jax.ShapeDtypeStruct((t, n), BF16), jax.ShapeDtypeStruct((t, aux_w), F32)],
        scratch_shapes=[pltpu.VMEM((tm, d), BF16)],
        compiler_params=_cparams(2),
        name="norm_in_proj",
    )(x, g, w)


POOL_HIST = 16


def _pool_kernel(u_ref, w_ref, sc_ref, o_ref, buf_ref):
    s = pl.program_id(1)
    ts = u_ref.shape[0]
    gw = w_ref.shape[1]

    @pl.when(s == 0)
    def _():
        buf_ref[0:POOL_HIST, :] = jnp.zeros((POOL_HIST, buf_ref.shape[1]), F32)

    @pl.when(s > 0)
    def _():
        buf_ref[0:POOL_HIST, :] = buf_ref[ts:ts + POOL_HIST, :]

    buf_ref[POOL_HIST:POOL_HIST + ts, :] = u_ref[...].astype(F32)
    pos = s * ts + lax.broadcasted_iota(jnp.int32, (ts, 1), 0)
    for g, win in enumerate(POOL_WINDOWS):
        cols = slice(g * gw, (g + 1) * gw)
        acc = buf_ref[POOL_HIST:POOL_HIST + ts, cols]
        cur = acc
        for k in range(1, win):
            acc = acc + buf_ref[POOL_HIST - k:POOL_HIST - k + ts, cols]
        cnt = jnp.minimum(pos + 1, win).astype(F32)
        p = acc / cnt - cur
        y = jnp.dot(p.astype(BF16), w_ref[g], preferred_element_type=F32)
        o_ref[:, cols] = (y * sc_ref[:, cols]).astype(o_ref.dtype)


def _pool_mixer(proj, col_block, w, scale, bsz, seq, ts):
    mixw = scale.shape[-1]
    nblk = seq // ts
    return pl.pallas_call(
        _pool_kernel,
        grid=(bsz, nblk),
        in_specs=[pl.BlockSpec((ts, mixw), lambda b, s: (b * nblk + s, col_block)),
                  pl.BlockSpec(w.shape, lambda b, s: (0, 0, 0)),
                  pl.BlockSpec((1, mixw), lambda b, s: (0, 0))],
        out_specs=pl.BlockSpec((ts, mixw), lambda b, s: (b * nblk + s, 0)),
        out_shape=jax.ShapeDtypeStruct((bsz * seq, mixw), BF16),
        scratch_shapes=[pltpu.VMEM((POOL_HIST + ts, mixw), F32)],
        compiler_params=_cparams(2),
        name="pool_mixer",
    )(proj, w, scale)


CONV_HIST = 32


def _conformer_kernel(u_ref, dw_ref, db_ref, lg_ref, lb_ref, o_ref, buf_ref, sh_ref, tmp_ref):
    s = pl.program_id(1)
    ts = u_ref.shape[0]
    c = o_ref.shape[1]
    width = dw_ref.shape[0]

    @pl.when(s == 0)
    def _():
        buf_ref[0:CONV_HIST, :] = jnp.zeros((CONV_HIST, c), F32)

    @pl.when(s > 0)
    def _():
        buf_ref[0:CONV_HIST, :] = buf_ref[ts:ts + CONV_HIST, :]

    a = u_ref[:, 0:c].astype(F32)
    gt = u_ref[:, c:2 * c].astype(F32)
    buf_ref[CONV_HIST:CONV_HIST + ts, :] = a * _sigmoid(gt)
    nsh = CONV_HIST + ts - SUBLANES
    for sft in range(1, SUBLANES):
        sh_ref[sft - 1, 0:nsh, :] = buf_ref[sft:sft + nsh, :]
    rc = min(ts, LANES)
    for r0 in range(0, ts, rc):
        for c0 in range(0, c, LANES):
            acc = jnp.zeros((rc, LANES), F32) + db_ref[:, c0:c0 + LANES]
            for k in range(width):
                off = CONV_HIST - (width - 1) + k + r0
                sft = off % SUBLANES
                if sft == 0:
                    tap = buf_ref[off:off + rc, c0:c0 + LANES]
                else:
                    tap = sh_ref[sft - 1, off - sft:off - sft + rc, c0:c0 + LANES]
                acc = acc + tap * dw_ref[k:k + 1, c0:c0 + LANES]
            tmp_ref[r0:r0 + rc, c0:c0 + LANES] = acc
    acc = tmp_ref[...]
    mu = jnp.mean(acc, axis=-1, keepdims=True)
    xc = acc - mu
    y = xc * lax.rsqrt(jnp.mean(xc * xc, axis=-1, keepdims=True) + NORM_EPS)
    y = y * lg_ref[...] + lb_ref[...]
    o_ref[...] = (y * _sigmoid(y)).astype(o_ref.dtype)


def _conformer(proj, col_block, dw, db, lg, lb, bsz, seq, ts):
    c = dw.shape[1]
    nblk = seq // ts
    return pl.pallas_call(
        _conformer_kernel,
        grid=(bsz, nblk),
        in_specs=[pl.BlockSpec((ts, 2 * c), lambda b, s: (b * nblk + s, col_block)),
                  pl.BlockSpec(dw.shape, lambda b, s: (0, 0)),
                  pl.BlockSpec((1, c), lambda b, s: (0, 0)),
                  pl.BlockSpec((1, c), lambda b, s: (0, 0)),
                  pl.BlockSpec((1, c), lambda b, s: (0, 0))],
        out_specs=pl.BlockSpec((ts, c), lambda b, s: (b * nblk + s, 0)),
        out_shape=jax.ShapeDtypeStruct((bsz * seq, c), BF16),
        scratch_shapes=[pltpu.VMEM((CONV_HIST + ts, c), F32),
                        pltpu.VMEM((SUBLANES - 1, CONV_HIST + ts, c), F32),
                        pltpu.VMEM((ts, c), F32)],
        compiler_params=_cparams(2),
        name="conformer_conv",
    )(proj, dw, db, lg, lb)


SSM_HIST = 8


def _ssd_kernel(z_ref, xs_ref, bc_ref, dt_ref, cw_ref, cb_ref, dtb_ref, a_ref, dsk_ref, ng_ref,
                o_ref, y_ref, xbuf_ref, bcbuf_ref, state_ref):
    c_idx = pl.program_id(1)
    L = xs_ref.shape[0]
    mixw = xs_ref.shape[1]
    bcw = bc_ref.shape[1]
    width = cw_ref.shape[0]
    hp = SSM_HEAD_DIM
    ns = SSM_STATE
    rep = SSM_HEADS // SSM_GROUPS

    @pl.when(c_idx == 0)
    def _():
        xbuf_ref[0:SSM_HIST, :] = jnp.zeros((SSM_HIST, mixw), F32)
        bcbuf_ref[0:SSM_HIST, :] = jnp.zeros((SSM_HIST, bcw), F32)
        state_ref[...] = jnp.zeros(state_ref.shape, F32)

    @pl.when(c_idx > 0)
    def _():
        xbuf_ref[0:SSM_HIST, :] = xbuf_ref[L:L + SSM_HIST, :]
        bcbuf_ref[0:SSM_HIST, :] = bcbuf_ref[L:L + SSM_HIST, :]

    xbuf_ref[SSM_HIST:SSM_HIST + L, :] = xs_ref[...].astype(F32)
    bcbuf_ref[SSM_HIST:SSM_HIST + L, :] = bc_ref[...].astype(F32)

    xc = jnp.zeros((L, mixw), F32) + cb_ref[:, 0:mixw]
    bcc = jnp.zeros((L, bcw), F32) + cb_ref[:, mixw:mixw + bcw]
    for k in range(width):
        off = SSM_HIST - (width - 1) + k
        xc = xc + xbuf_ref[off:off + L, :] * cw_ref[k:k + 1, 0:mixw]
        bcc = bcc + bcbuf_ref[off:off + L, :] * cw_ref[k:k + 1, mixw:mixw + bcw]
    xc = xc * _sigmoid(xc)
    bcc = bcc * _sigmoid(bcc)

    dtr = dt_ref[...] + dtb_ref[...]
    dt = jnp.maximum(dtr, 0.0) + jnp.log(1.0 + jnp.exp(-jnp.abs(dtr)))
    da = dt * a_ref[...]
    row_i = lax.broadcasted_iota(jnp.int32, (L, L), 0)
    col_i = lax.broadcasted_iota(jnp.int32, (L, L), 1)
    tri = col_i <= row_i
    a_cum = jnp.dot(tri.astype(F32), da, preferred_element_type=F32,
                    precision=lax.Precision.HIGHEST)
    a_cum_t = a_cum.T
    a_last = a_cum[L - 1:L, :]
    dec_last = jnp.exp(a_last)
    dec_out = jnp.exp(a_cum)
    dec_st = jnp.exp(a_last - a_cum)

    cbs = []
    for g in range(SSM_GROUPS):
        bg = bcc[:, g * ns:(g + 1) * ns].astype(BF16)
        cg = bcc[:, SSM_GROUPS * ns + g * ns:SSM_GROUPS * ns + (g + 1) * ns].astype(BF16)
        cbs.append(lax.dot_general(cg, bg, (((1,), (1,)), ((), ())), preferred_element_type=F32))

    for h in range(SSM_HEADS):
        g = h // rep
        x_h = xc[:, h * hp:(h + 1) * hp]
        b_g = bcc[:, g * ns:(g + 1) * ns]
        c_g = bcc[:, SSM_GROUPS * ns + g * ns:SSM_GROUPS * ns + (g + 1) * ns]
        xd = x_h * dt[:, h:h + 1]
        seg = a_cum[:, h:h + 1] - a_cum_t[h:h + 1, :]
        lmat = jnp.exp(jnp.where(tri, seg, -jnp.inf))
        y_diag = jnp.dot((cbs[g] * lmat).astype(BF16), xd.astype(BF16), preferred_element_type=F32)
        st = state_ref[h]
        y_off = jnp.dot(c_g.astype(BF16), st.astype(BF16), preferred_element_type=F32) * dec_out[:, h:h + 1]
        bd = (b_g * dec_st[:, h:h + 1]).astype(BF16)
        st_new = lax.dot_general(bd, xd.astype(BF16), (((0,), (0,)), ((), ())),
                                 preferred_element_type=F32)
        state_ref[h] = st * dec_last[:, h:h + 1] + st_new
        y_h = y_diag + y_off + dsk_ref[:, h:h + 1] * x_h
        zz = z_ref[:, h * hp:(h + 1) * hp].astype(F32)
        y_ref[:, h * hp:(h + 1) * hp] = y_h * (zz * _sigmoid(zz))

    o_ref[...] = _rms_rows(y_ref[...], ng_ref[...]).astype(o_ref.dtype)


def _ssd(proj, aux, z_block, xs_block, bc_block, dt_block, cw, cb, dtb, a_neg, dsk, ng, bsz, seq):
    mixw = ng.shape[-1]
    bcw = cw.shape[1] - mixw
    L = SSM_CHUNK
    nblk = seq // L
    row = lambda b, s: b * nblk + s
    return pl.pallas_call(
        _ssd_kernel,
        grid=(bsz, nblk),
        in_specs=[pl.BlockSpec((L, mixw), lambda b, s: (row(b, s), z_block)),
                  pl.BlockSpec((L, mixw), lambda b, s: (row(b, s), xs_block)),
                  pl.BlockSpec((L, bcw), lambda b, s: (row(b, s), bc_block)),
                  pl.BlockSpec((L, LANES), lambda b, s: (row(b, s), dt_block)),
                  pl.BlockSpec(cw.shape, lambda b, s: (0, 0)),
                  pl.BlockSpec(cb.shape, lambda b, s: (0, 0)),
                  pl.BlockSpec((1, LANES), lambda b, s: (0, 0)),
                  pl.BlockSpec((1, LANES), lambda b, s: (0, 0)),
                  pl.BlockSpec((1, LANES), lambda b, s: (0, 0)),
                  pl.BlockSpec((1, mixw), lambda b, s: (0, 0))],
        out_specs=pl.BlockSpec((L, mixw), lambda b, s: (row(b, s), 0)),
        out_shape=jax.ShapeDtypeStruct((bsz * seq, mixw), BF16),
        scratch_shapes=[pltpu.VMEM((L, mixw), F32),
                        pltpu.VMEM((SSM_HIST + L, mixw), F32),
                        pltpu.VMEM((SSM_HIST + L, bcw), F32),
                        pltpu.VMEM((SSM_HEADS, SSM_STATE, SSM_HEAD_DIM), F32)],
        compiler_params=_cparams(2),
        name="ssd_mixer",
    )(proj, proj, proj, aux, cw, cb, dtb, a_neg, dsk, ng)


def _rope_cols(x, c, s1, s2, half):
    return (x * c + pltpu.roll(x, LANES - half, axis=1) * s1 + pltpu.roll(x, half, axis=1) * s2)


LOG2E = 1.4426950408889634


def _rope_prep_kernel(q_ref, kv_ref, idx_ref, ta_ref, ti_ref, tt_ref,
                      qg_ref, k_ref, vt_ref, qig_ref, ki_ref, wg_ref):
    hd = HEAD_DIM
    ts = q_ref.shape[0]
    rep = ATT_HEADS // ATT_KV_HEADS
    scale = (hd ** -0.5) * LOG2E
    ca, s1a, s2a = ta_ref[0], ta_ref[1], ta_ref[2]
    heads_per_col = LANES // hd
    for c in range(q_ref.shape[1] // LANES):
        qc = _rope_cols(q_ref[:, c * LANES:(c + 1) * LANES].astype(F32), ca, s1a, s2a, ROPE_DIM // 2) * scale
        qct = qc.T.astype(qg_ref.dtype)
        for r in range(heads_per_col):
            h = c * heads_per_col + r
            g, hh = h // rep, h % rep
            for jb in range(ts // LANES):
                dst = jb * rep * LANES + hh * LANES
                qg_ref[g, :, dst:dst + LANES] = qct[r * hd:(r + 1) * hd, jb * LANES:(jb + 1) * LANES]
    kw = ATT_KV_HEADS * hd
    kc = _rope_cols(kv_ref[:, 0:kw].astype(F32), ca, s1a, s2a, ROPE_DIM // 2)
    for g in range(ATT_KV_HEADS):
        k_ref[g] = kc[:, g * hd:(g + 1) * hd].astype(k_ref.dtype)
    vt_ref[...] = kv_ref[:, kw:2 * kw].astype(F32).T.astype(vt_ref.dtype)
    ci, s1i, s2i = ti_ref[0], ti_ref[1], ti_ref[2]
    nq = IDX_HEADS * IDX_DIM
    iheads_per_col = LANES // IDX_DIM
    for c in range(nq // LANES):
        qc = _rope_cols(idx_ref[:, c * LANES:(c + 1) * LANES], ci, s1i, s2i, IDX_ROPE_DIM // 2)
        qct = qc.T.astype(qig_ref.dtype)
        for r in range(iheads_per_col):
            h = c * iheads_per_col + r
            for jb in range(ts // LANES):
                dst = jb * IDX_HEADS * LANES + h * LANES
                qig_ref[:, dst:dst + LANES] = qct[r * IDX_DIM:(r + 1) * IDX_DIM, jb * LANES:(jb + 1) * LANES]
    tail = _rope_cols(idx_ref[:, nq:nq + LANES], tt_ref[0], tt_ref[1], tt_ref[2], IDX_ROPE_DIM // 2)
    ki_ref[...] = tail[:, 0:IDX_DIM].astype(ki_ref.dtype)
    tail_t = tail.T
    for h in range(IDX_HEADS):
        for jb in range(ts // LANES):
            dst = jb * IDX_HEADS * LANES + h * LANES
            wg_ref[:, dst:dst + LANES] = tail_t[IDX_DIM + h:IDX_DIM + h + 1, jb * LANES:(jb + 1) * LANES]


def _rope_prep(proj, aux, q_block, kv_block, tab_att, tab_idx, tab_tail, bsz, seq, ts):
    nblk = seq // ts
    qw = ATT_HEADS * HEAD_DIM
    kw = ATT_KV_HEADS * HEAD_DIM
    idxw = 4 * LANES
    rep = ATT_HEADS // ATT_KV_HEADS
    row = lambda b, s: b * nblk + s
    tab_spec = pl.BlockSpec((3, ts, LANES), lambda b, s: (0, s, 0))
    return pl.pallas_call(
        _rope_prep_kernel,
        grid=(bsz, nblk),
        in_specs=[pl.BlockSpec((ts, qw), lambda b, s: (row(b, s), q_block)),
                  pl.BlockSpec((ts, 2 * kw), lambda b, s: (row(b, s), kv_block)),
                  pl.BlockSpec((ts, idxw), lambda b, s: (row(b, s), 0)),
                  tab_spec, tab_spec, tab_spec],
        out_specs=[pl.BlockSpec((None, ATT_KV_HEADS, HEAD_DIM, rep * ts), lambda b, s: (b, 0, 0, s)),
                   pl.BlockSpec((None, ATT_KV_HEADS, ts, HEAD_DIM), lambda b, s: (b, 0, s, 0)),
                   pl.BlockSpec((None, kw, ts), lambda b, s: (b, 0, s)),
                   pl.BlockSpec((None, IDX_DIM, IDX_HEADS * ts), lambda b, s: (b, 0, s)),
                   pl.BlockSpec((None, ts, IDX_DIM), lambda b, s: (b, s, 0)),
                   pl.BlockSpec((None, 1, IDX_HEADS * ts), lambda b, s: (b, 0, s))],
        out_shape=[jax.ShapeDtypeStruct((bsz, ATT_KV_HEADS, HEAD_DIM, rep * seq), BF16),
                   jax.ShapeDtypeStruct((bsz, ATT_KV_HEADS, seq, HEAD_DIM), BF16),
                   jax.ShapeDtypeStruct((bsz, kw, seq), BF16),
                   jax.ShapeDtypeStruct((bsz, IDX_DIM, IDX_HEADS * seq), BF16),
                   jax.ShapeDtypeStruct((bsz, seq, IDX_DIM), BF16),
                   jax.ShapeDtypeStruct((bsz, 1, IDX_HEADS * seq), F32)],
        compiler_params=_cparams(2),
        name="rope_prep",
    )(proj, proj, aux, tab_att, tab_idx, tab_tail)


def _rope_tables(seq, head_dim, rot_dim, n_rot_heads):
    half = rot_dim // 2
    inv_freq = jnp.power(jnp.float32(ROPE_THETA), -jnp.arange(half, dtype=F32) * (2.0 / rot_dim))
    ang = jnp.arange(seq, dtype=F32)[:, None] * inv_freq[None, :]
    cos, sin = jnp.cos(ang), jnp.sin(ang)
    lane = np.arange(LANES)
    j = lane % head_dim
    rot_head = lane < n_rot_heads * head_dim
    first = rot_head & (j < half)
    second = rot_head & (j >= half) & (j < rot_dim)
    fi = np.where(first, j, 0)
    si = np.where(second, j - half, 0)
    c = jnp.where(first[None, :], cos[:, fi], jnp.where(second[None, :], cos[:, si], 1.0))
    s1 = jnp.where(first[None, :], -sin[:, fi], 0.0)
    s2 = jnp.where(second[None, :], sin[:, si], 0.0)
    return jnp.stack([c, s1, s2]).astype(F32)


def _sortable_key(score):
    score = jnp.where(score == 0.0, 0.0, score)
    bits = pltpu.bitcast(score, jnp.int32)
    return bits ^ (lax.shift_right_arithmetic(bits, 31) & jnp.int32(0x7FFFFFFF))


KEY_CHUNK = 256
COUNT_ROWS = 4 * SUBLANES


def _dsa_kernel(qg_ref, qig_ref, wg_ref, k_ref, ki_ref, vt_ref, o_ref, keys_ref, acc_ref, m_ref, s_ref,
                sa_ref, la_ref,
                *, top_k, seq_bits):
    j = pl.program_id(1)
    qb = LANES
    kc = KEY_CHUNK
    hd = HEAD_DIM
    rep = ATT_HEADS // ATT_KV_HEADS
    n_kc = lax.div(j * qb + (qb + kc - 1), kc)
    sub_i = lax.broadcasted_iota(jnp.int32, (kc, qb), 0)
    q_pos = j * qb + lax.broadcasted_iota(jnp.int32, (kc, qb), 1)

    n_pair = lax.div(j * qb + (qb + 2 * kc - 1), 2 * kc)
    last_chunk = 2 * n_pair - 1

    def idx_matmul(c):
        start = pl.multiple_of(c * kc, kc)
        return jnp.dot(ki_ref[pl.ds(start, kc), :], qig_ref[...], preferred_element_type=F32)

    def idx_keys(c, s_all):
        start = pl.multiple_of(c * kc, kc)
        score = jnp.zeros((kc, qb), F32)
        for h in range(IDX_HEADS):
            score = score + jnp.maximum(s_all[:, h * qb:(h + 1) * qb], 0.0) * wg_ref[:, h * qb:(h + 1) * qb]
        causal = (start + sub_i) <= q_pos
        keys_ref[pl.ds(start, kc), :] = jnp.where(causal, _sortable_key(score), jnp.int32(INT_MIN))

    def qk_logits(c):
        start = pl.multiple_of(c * kc, kc)
        return [jnp.dot(k_ref[g, pl.ds(start, kc), :], qg_ref[g], preferred_element_type=F32)
                for g in range(ATT_KV_HEADS)]

    sa_ref[...] = idx_matmul(0)
    for g, lg in enumerate(qk_logits(0)):
        la_ref[g] = lg

    def score_pair(p, carry):
        s_b = idx_matmul(2 * p + 1)
        idx_keys(2 * p, sa_ref[...])
        sa_ref[...] = idx_matmul(jnp.minimum(2 * p + 2, last_chunk))
        idx_keys(2 * p + 1, s_b)
        return carry

    lax.fori_loop(0, n_pair, score_pair, 0)

    def count_where(pred_fn):
        def body(c, acc):
            start = pl.multiple_of(c * kc, kc)
            ones = jnp.where(pred_fn(keys_ref[pl.ds(start, kc), :], start), 1.0, 0.0)
            return acc + jnp.sum(ones.reshape(kc // COUNT_ROWS, COUNT_ROWS, qb), axis=0)
        acc = lax.fori_loop(0, n_kc, body, jnp.zeros((COUNT_ROWS, qb), F32))
        return jnp.sum(acc, axis=0, keepdims=True)

    def count_ge(cand):
        def body(p, acc):
            start = pl.multiple_of(p * (2 * kc), 2 * kc)
            ones = jnp.where(keys_ref[pl.ds(start, 2 * kc), :] >= cand, 1.0, 0.0)
            return acc + jnp.sum(ones.reshape(2 * kc // COUNT_ROWS, COUNT_ROWS, qb), axis=0)
        acc = lax.fori_loop(0, n_pair, body, jnp.zeros((COUNT_ROWS, qb), F32))
        return jnp.sum(acc, axis=0, keepdims=True)

    def bit_body(i, carry):
        tau, n_ge = carry
        cand = tau ^ lax.shift_left(jnp.int32(1), 31 - i)
        cnt = count_ge(cand)
        ok = cnt >= top_k
        return jnp.where(ok, cand, tau), jnp.where(ok, cnt, n_ge)

    tau, n_ge = lax.fori_loop(0, 32, bit_body,
                              (jnp.full((1, qb), INT_MIN, jnp.int32), jnp.zeros((1, qb), F32)))
    tau = jnp.maximum(tau, jnp.int32(INT_MIN + 1))

    has_excess = jnp.max(jnp.where(n_ge > top_k, 1.0, 0.0)) > 0.5

    @pl.when(has_excess)
    def _():
        n_gt = count_where(lambda kk, _: kk > tau)
        need = top_k - n_gt

        def pos_body(i, xcut):
            cand = xcut | lax.shift_left(jnp.int32(1), seq_bits - 1 - i)
            cnt = count_where(lambda kk, st: (kk == tau) & ((st + sub_i) < cand))
            return jnp.where(cnt < need, cand, xcut)

        xcut = lax.fori_loop(0, seq_bits, pos_body, jnp.zeros((1, qb), jnp.int32))
        def fix_body(c, carry):
            start = pl.multiple_of(c * kc, kc)
            kk = keys_ref[pl.ds(start, kc), :]
            drop = (kk == tau) & (((start + sub_i) > xcut) | (need < 1))
            keys_ref[pl.ds(start, kc), :] = jnp.where(drop, jnp.int32(INT_MIN), kk)
            return carry

        lax.fori_loop(0, n_kc, fix_body, 0)

    m_ref[...] = jnp.full(m_ref.shape, NEG_BIG, F32)
    s_ref[...] = jnp.zeros(s_ref.shape, F32)
    acc_ref[...] = jnp.zeros(acc_ref.shape, F32)

    def softmax_pv(c, lgs):
        start = pl.multiple_of(c * kc, kc)
        bias = jnp.where(keys_ref[pl.ds(start, kc), :] >= tau, 0.0, NEG_BIG)
        for g in range(ATT_KV_HEADS):
            ps, alphas = [], []
            for hh in range(rep):
                h = g * rep + hh
                logit = lgs[g][:, hh * qb:(hh + 1) * qb] + bias
                m_old = m_ref[h:h + 1, :]
                m_new = jnp.maximum(m_old, jnp.max(logit, axis=0, keepdims=True))
                p = jnp.exp2(logit - m_new)
                alpha = jnp.exp2(m_old - m_new)
                s_ref[h:h + 1, :] = s_ref[h:h + 1, :] * alpha + jnp.sum(p, axis=0, keepdims=True)
                m_ref[h:h + 1, :] = m_new
                ps.append(p.astype(BF16))
                alphas.append(alpha)
            pv = jnp.dot(vt_ref[g * hd:(g + 1) * hd, pl.ds(start, kc)], jnp.concatenate(ps, axis=1),
                         preferred_element_type=F32)
            for hh in range(rep):
                rows = slice((g * rep + hh) * hd, (g * rep + hh + 1) * hd)
                acc_ref[rows, :] = acc_ref[rows, :] * alphas[hh] + pv[:, hh * qb:(hh + 1) * qb]

    def att_pair(p, carry):
        l_b = qk_logits(2 * p + 1)
        softmax_pv(2 * p, [la_ref[g] for g in range(ATT_KV_HEADS)])
        for g, lg in enumerate(qk_logits(jnp.minimum(2 * p + 2, last_chunk))):
            la_ref[g] = lg
        softmax_pv(2 * p + 1, l_b)
        return carry

    lax.fori_loop(0, n_pair, att_pair, 0)
    for h in range(ATT_HEADS):
        rows = slice(h * hd, (h + 1) * hd)
        acc_ref[rows, :] = acc_ref[rows, :] / s_ref[h:h + 1, :]
    for c0 in range(0, ATT_HEADS * hd, LANES):
        o_ref[:, c0:c0 + LANES] = acc_ref[c0:c0 + LANES, :].T.astype(o_ref.dtype)


def _dsa_attention(qg, qig, wg, k, ki, vt, bsz, seq):
    qb = LANES
    nblk = seq // qb
    qw = ATT_HEADS * HEAD_DIM
    kw = ATT_KV_HEADS * HEAD_DIM
    rep = ATT_HEADS // ATT_KV_HEADS
    top_k = min(TOPK_MAX, seq // 4)
    seq_bits = int(np.log2(seq))
    assert 2 ** seq_bits == seq and seq % (2 * KEY_CHUNK) == 0
    kern = functools.partial(_dsa_kernel, top_k=top_k, seq_bits=seq_bits)
    return pl.pallas_call(
        kern,
        grid=(bsz, nblk),
        in_specs=[pl.BlockSpec((None, ATT_KV_HEADS, HEAD_DIM, rep * qb), lambda b, j: (b, 0, 0, j)),
                  pl.BlockSpec((None, IDX_DIM, IDX_HEADS * qb), lambda b, j: (b, 0, j)),
                  pl.BlockSpec((None, 1, IDX_HEADS * qb), lambda b, j: (b, 0, j)),
                  pl.BlockSpec((None, ATT_KV_HEADS, seq, HEAD_DIM), lambda b, j: (b, 0, 0, 0)),
                  pl.BlockSpec((None, seq, IDX_DIM), lambda b, j: (b, 0, 0)),
                  pl.BlockSpec((None, kw, seq), lambda b, j: (b, 0, 0))],
        out_specs=pl.BlockSpec((qb, qw), lambda b, j: (b * nblk + j, 0)),
        out_shape=jax.ShapeDtypeStruct((bsz * seq, qw), BF16),
        scratch_shapes=[pltpu.VMEM((seq, qb), jnp.int32), pltpu.VMEM((qw, qb), F32),
                        pltpu.VMEM((ATT_HEADS, qb), F32), pltpu.VMEM((ATT_HEADS, qb), F32),
                        pltpu.VMEM((KEY_CHUNK, IDX_HEADS * qb), F32),
                        pltpu.VMEM((ATT_KV_HEADS, KEY_CHUNK, rep * qb), F32)],
        compiler_params=_cparams(2),
        name="dsa_attention",
    )(qg, qig, wg, k, ki, vt)


def _merge_kernel(x_ref, ya_ref, yb_ref, yc_ref, yd_ref, gl_ref, wbr_ref, wout_ref, o_ref):
    d = x_ref.shape[1]
    merged = None
    for n, y_ref in enumerate((ya_ref, yb_ref, yc_ref, yd_ref)):
        u = jnp.dot(y_ref[...], wbr_ref[n], preferred_element_type=F32)
        term = _sigmoid(gl_ref[:, n * d:(n + 1) * d].astype(F32)) * u
        merged = term if merged is None else merged + term
    o_ref[...] = x_ref[...] + jnp.dot(merged.astype(BF16), wout_ref[...], preferred_element_type=F32)


def _merge(x, ya, yb, yc, yd, proj, gate_block, wbr, wout, tm):
    t, d = x.shape
    mixw = ya.shape[1]
    nbr = wbr.shape[0]
    y_spec = pl.BlockSpec((tm, mixw), lambda i: (i, 0))
    return pl.pallas_call(
        _merge_kernel,
        grid=(t // tm,),
        in_specs=[pl.BlockSpec((tm, d), lambda i: (i, 0)), y_spec, y_spec, y_spec, y_spec,
                  pl.BlockSpec((tm, nbr * d), lambda i: (i, gate_block)),
                  pl.BlockSpec(wbr.shape, lambda i: (0, 0, 0)),
                  pl.BlockSpec(wout.shape, lambda i: (0, 0))],
        out_specs=pl.BlockSpec((tm, d), lambda i: (i, 0)),
        out_shape=jax.ShapeDtypeStruct((t, d), F32),
        compiler_params=_cparams(1),
        name="merge_out_proj",
    )(x, ya, yb, yc, yd, proj, wbr, wout)


def _ffn_kernel(x_ref, g_ref, wg_ref, wu_ref, wd_ref, fin_ref, o_ref, h_ref, acc_ref, *, final_norm):
    f = pl.program_id(1)

    @pl.when(f == 0)
    def _():
        h_ref[...] = _rms_rows(x_ref[...], g_ref[...]).astype(h_ref.dtype)
        acc_ref[...] = jnp.zeros(acc_ref.shape, F32)

    h = h_ref[...]
    a = jnp.dot(h, wg_ref[...], preferred_element_type=F32)
    b = jnp.dot(h, wu_ref[...], preferred_element_type=F32)
    t = (a * _sigmoid(a)) * b
    acc_ref[...] += jnp.dot(t.astype(BF16), wd_ref[...], preferred_element_type=F32)

    @pl.when(f == pl.num_programs(1) - 1)
    def _():
        y = x_ref[...] + acc_ref[...]
        o_ref[...] = _rms_rows(y, fin_ref[...]) if final_norm else y


def _ffn(x, g, wg, wu, wd, fin, final_norm, tm, tf):
    t, d = x.shape
    ff = wg.shape[1]
    return pl.pallas_call(
        functools.partial(_ffn_kernel, final_norm=final_norm),
        grid=(t // tm, ff // tf),
        in_specs=[pl.BlockSpec((tm, d), lambda i, f: (i, 0)),
                  pl.BlockSpec((1, d), lambda i, f: (0, 0)),
                  pl.BlockSpec((d, tf), lambda i, f: (0, f)),
                  pl.BlockSpec((d, tf), lambda i, f: (0, f)),
                  pl.BlockSpec((tf, d), lambda i, f: (f, 0)),
                  pl.BlockSpec((1, d), lambda i, f: (0, 0))],
        out_specs=pl.BlockSpec((tm, d), lambda i, f: (i, 0)),
        out_shape=jax.ShapeDtypeStruct((t, d), F32),
        scratch_shapes=[pltpu.VMEM((tm, d), BF16), pltpu.VMEM((tm, d), F32)],
        compiler_params=_cparams(2),
        name="dense_swiglu",
    )(x, g, wg, wu, wd, fin)


def _router_kernel(x_ref, g_ref, r_ref, h_ref, mi_ref, mp_ref, cnt_ref, tri_ref, carry_ref, *, n_experts):
    i = pl.program_id(0)
    tm = x_ref.shape[0]
    lane = lax.broadcasted_iota(jnp.int32, (tm, LANES), 1)

    @pl.when(i == 0)
    def _():
        carry_ref[...] = jnp.zeros(carry_ref.shape, F32)
        r_i = lax.broadcasted_iota(jnp.int32, (tm, tm), 0)
        c_i = lax.broadcasted_iota(jnp.int32, (tm, tm), 1)
        tri_ref[...] = jnp.where(c_i < r_i, 1.0, 0.0).astype(tri_ref.dtype)

    hf = _rms_rows(x_ref[...], g_ref[...])
    h_ref[...] = hf
    logits = jnp.dot(hf, r_ref[...], preferred_element_type=F32, precision=lax.Precision.HIGHEST)
    logits = jnp.where(lane < n_experts, logits, -jnp.inf)
    m1 = jnp.max(logits, axis=-1, keepdims=True)
    i1 = jnp.min(jnp.where(logits == m1, lane, LANES), axis=-1, keepdims=True)
    rest = jnp.where(lane == i1, -jnp.inf, logits)
    m2 = jnp.max(rest, axis=-1, keepdims=True)
    i2 = jnp.min(jnp.where(rest == m2, lane, LANES), axis=-1, keepdims=True)
    e2 = jnp.exp(m2 - m1)
    p1 = 1.0 / (1.0 + e2)
    p2 = e2 / (1.0 + e2)
    oh1 = lane == i1
    oh2 = lane == i2
    ohs = jnp.where(oh1, 1.0, jnp.where(oh2, 1.0, 0.0))
    pref = jnp.dot(tri_ref[...], ohs.astype(tri_ref.dtype), preferred_element_type=F32) + carry_ref[...]
    rank1 = jnp.sum(jnp.where(oh1, pref, 0.0), axis=-1, keepdims=True).astype(jnp.int32)
    rank2 = jnp.sum(jnp.where(oh2, pref, 0.0), axis=-1, keepdims=True).astype(jnp.int32)
    carry_ref[...] = carry_ref[...] + jnp.sum(ohs, axis=0, keepdims=True)
    cnt_ref[...] = carry_ref[...]
    mi_ref[...] = jnp.where(lane == 0, i1, jnp.where(lane == 1, i2, jnp.where(lane == 2, rank1,
                                                                          jnp.where(lane == 3, rank2, 0))))
    mp_ref[...] = jnp.where(lane == 0, p1, jnp.where(lane == 1, p2, 0.0))


def _router(x, g, router, tm):
    t, d = x.shape
    n_experts = router.shape[1]
    rpad = jnp.zeros((d, LANES), F32).at[:, :n_experts].set(router.astype(F32))
    return pl.pallas_call(
        functools.partial(_router_kernel, n_experts=n_experts),
        grid=(t // tm,),
        in_specs=[pl.BlockSpec((tm, d), lambda i: (i, 0)),
                  pl.BlockSpec((1, d), lambda i: (0, 0)),
                  pl.BlockSpec((d, LANES), lambda i: (0, 0))],
        out_specs=[pl.BlockSpec((tm, d), lambda i: (i, 0)),
                   pl.BlockSpec((tm, LANES), lambda i: (i, 0)),
                   pl.BlockSpec((tm, LANES), lambda i: (i, 0)),
                   pl.BlockSpec((1, LANES), lambda i: (0, 0))],
        out_shape=[jax.ShapeDtypeStruct((t, d), F32),
                   jax.ShapeDtypeStruct((t, LANES), jnp.int32),
                   jax.ShapeDtypeStruct((t, LANES), F32),
                   jax.ShapeDtypeStruct((1, LANES), F32)],
        scratch_shapes=[pltpu.VMEM((tm, tm), BF16), pltpu.VMEM((1, LANES), F32)],
        compiler_params=_cparams(1),
        name="moe_router",
    )(x, g, rpad)


DMA_UNROLL = 8


def _row_copy(src, dst, sem):
    return pltpu.make_async_copy(src, dst, sem)


def _dispatch_kernel(dest_ref, h_ref, xs_in_ref, xs_ref, sem, *, n_tokens):
    del xs_in_ref
    td = h_ref.shape[0]
    base = pl.program_id(0) * td

    def issue(r, carry):
        for k in range(TOP_K):
            d = dest_ref[k * n_tokens + base + r]
            _row_copy(h_ref.at[pl.ds(r, 1), :], xs_ref.at[pl.ds(d, 1), :], sem).start(priority=k % 2)
        return carry

    def drain(r, carry):
        for k in range(TOP_K):
            _row_copy(h_ref.at[pl.ds(0, 1), :], xs_ref.at[pl.ds(0, 1), :], sem).wait()
        return carry

    lax.fori_loop(0, td, issue, 0, unroll=DMA_UNROLL)
    lax.fori_loop(0, td, drain, 0, unroll=DMA_UNROLL)


def _dispatch(dest, h, n_rows, td):
    t, d = h.shape
    xs0 = jnp.zeros((n_rows, d), F32)
    return pl.pallas_call(
        functools.partial(_dispatch_kernel, n_tokens=t),
        grid_spec=pltpu.PrefetchScalarGridSpec(
            num_scalar_prefetch=1,
            grid=(t // td,),
            in_specs=[pl.BlockSpec((td, d), lambda i, dest: (i, 0)),
                      pl.BlockSpec(memory_space=pl.ANY)],
            out_specs=pl.BlockSpec(memory_space=pl.ANY),
            scratch_shapes=[pltpu.SemaphoreType.DMA(())]),
        out_shape=jax.ShapeDtypeStruct((n_rows, d), F32),
        input_output_aliases={2: 0},
        compiler_params=_cparams(1),
        name="moe_dispatch",
    )(dest, h, xs0)


def _expert_kernel(te_ref, nu_ref, xs_ref, wg_ref, wu_ref, wd_ref, ys_ref, h_ref, acc_ref):
    i = pl.program_id(0)
    f = pl.program_id(1)
    last_f = pl.num_programs(1) - 1
    valid = i < nu_ref[0]

    @pl.when(valid & (f == 0))
    def _():
        h_ref[...] = xs_ref[...].astype(h_ref.dtype)
        acc_ref[...] = jnp.zeros(acc_ref.shape, F32)

    @pl.when(valid)
    def _():
        h = h_ref[...]
        a = jnp.dot(h, wg_ref[...], preferred_element_type=F32)
        b = jnp.dot(h, wu_ref[...], preferred_element_type=F32)
        t = (a * _sigmoid(a)) * b
        acc_ref[...] += jnp.dot(t.astype(BF16), wd_ref[...], preferred_element_type=F32)

    @pl.when(valid & (f == last_f))
    def _():
        ys_ref[...] = acc_ref[...]

    @pl.when(jnp.logical_not(valid) & (f == last_f))
    def _():
        ys_ref[...] = jnp.zeros(ys_ref.shape, F32)


def _expert_mlp(tile_expert, n_used, xs, wg, wu, wd, tg, tf):
    n_rows, d = xs.shape
    ff = wg.shape[-1]
    nf = ff // tf
    fblk = lambda i, f, te, nu: jnp.where(i < nu[0], f, nf - 1)
    return pl.pallas_call(
        _expert_kernel,
        grid_spec=pltpu.PrefetchScalarGridSpec(
            num_scalar_prefetch=2,
            grid=(n_rows // tg, nf),
            in_specs=[pl.BlockSpec((tg, d), lambda i, f, te, nu: (i, 0)),
                      pl.BlockSpec((None, d, tf), lambda i, f, te, nu: (te[i], 0, fblk(i, f, te, nu))),
                      pl.BlockSpec((None, d, tf), lambda i, f, te, nu: (te[i], 0, fblk(i, f, te, nu))),
                      pl.BlockSpec((None, tf, d), lambda i, f, te, nu: (te[i], fblk(i, f, te, nu), 0))],
            out_specs=pl.BlockSpec((tg, d), lambda i, f, te, nu: (i, 0)),
            scratch_shapes=[pltpu.VMEM((tg, d), BF16), pltpu.VMEM((tg, d), F32)]),
        out_shape=jax.ShapeDtypeStruct((n_rows, d), F32),
        compiler_params=_cparams(2),
        name="moe_experts",
    )(tile_expert, n_used, xs, wg, wu, wd)


def _combine_kernel(dest_ref, x_ref, mp_ref, fin_ref, ys_ref, o_ref, buf_ref, sem, *, n_tokens, final_norm):
    tc = x_ref.shape[0]
    base = pl.program_id(0) * tc

    def issue(r, carry):
        for k in range(TOP_K):
            d = dest_ref[k * n_tokens + base + r]
            _row_copy(ys_ref.at[pl.ds(d, 1), :], buf_ref.at[k, pl.ds(r, 1), :], sem).start(priority=k % 2)
        return carry

    def drain(r, carry):
        for k in range(TOP_K):
            _row_copy(ys_ref.at[pl.ds(0, 1), :], buf_ref.at[k, pl.ds(0, 1), :], sem).wait()
        return carry

    lax.fori_loop(0, tc, issue, 0, unroll=DMA_UNROLL)
    lax.fori_loop(0, tc, drain, 0, unroll=DMA_UNROLL)
    y = x_ref[...]
    for k in range(TOP_K):
        y = y + mp_ref[:, k:k + 1] * buf_ref[k]
    o_ref[...] = _rms_rows(y, fin_ref[...]) if final_norm else y


def _combine(dest, x, mp, fin, ys, final_norm, tc):
    t, d = x.shape
    return pl.pallas_call(
        functools.partial(_combine_kernel, n_tokens=t, final_norm=final_norm),
        grid_spec=pltpu.PrefetchScalarGridSpec(
            num_scalar_prefetch=1,
            grid=(t // tc,),
            in_specs=[pl.BlockSpec((tc, d), lambda i, dest: (i, 0)),
                      pl.BlockSpec((tc, LANES), lambda i, dest: (i, 0)),
                      pl.BlockSpec((1, d), lambda i, dest: (0, 0)),
                      pl.BlockSpec(memory_space=pl.ANY)],
            out_specs=pl.BlockSpec((tc, d), lambda i, dest: (i, 0)),
            scratch_shapes=[pltpu.VMEM((TOP_K, tc, d), F32), pltpu.SemaphoreType.DMA(())]),
        out_shape=jax.ShapeDtypeStruct((t, d), F32),
        compiler_params=_cparams(1),
        name="moe_combine",
    )(dest, x, mp, fin, ys)


def _moe(x, g, router, wg, wu, wd, fin, final_norm, tm, tg, tf):
    t, d = x.shape
    n_experts = router.shape[1]
    h, mi, mp, cnt = _router(x, g, router, tm)
    counts = cnt[0, :n_experts].astype(jnp.int32)
    padded = ((counts + tg - 1) // tg) * tg
    ends = jnp.cumsum(padded)
    offs = ends - padded
    dest = jnp.concatenate([offs[mi[:, k]] + mi[:, TOP_K + k] for k in range(TOP_K)]).astype(jnp.int32)
    n_tiles = (TOP_K * t) // tg + n_experts
    n_used = (ends[-1] // tg).astype(jnp.int32)
    tidx = jnp.minimum(jnp.arange(n_tiles, dtype=jnp.int32), n_used - 1)
    tile_expert = jnp.sum((tidx[:, None] * tg >= ends[None, :]).astype(jnp.int32), axis=1)
    xs = _dispatch(dest, h, n_tiles * tg, _pick_tile(t, 512))
    ys = _expert_mlp(tile_expert, n_used.reshape(1), xs, wg, wu, wd, tg, tf)
    return _combine(dest, x, mp, fin, ys, final_norm, _pick_tile(t, 512))


def _pack_w_in(w_in, d_model, mixw):
    qw = ATT_HEADS * HEAD_DIM
    kvw = 2 * ATT_KV_HEADS * HEAD_DIM
    qiw = IDX_HEADS * IDX_DIM
    bcw = 2 * SSM_GROUPS * SSM_STATE
    sizes = (mixw, qw, kvw, qiw, IDX_DIM, IDX_HEADS, 2 * mixw, mixw, mixw + bcw, SSM_HEADS, 4 * d_model)
    offs = np.concatenate([[0], np.cumsum(sizes)])
    assert offs[-1] == w_in.shape[1]
    seg = lambda i: w_in[:, offs[i]:offs[i + 1]]
    u_pool, q, kv, qi, ki, wi, u_conv, z, xbc, dt, gates = (seg(i) for i in range(len(sizes)))
    d = w_in.shape[0]
    zeros = lambda n: jnp.zeros((d, n), w_in.dtype)
    idx_blk = jnp.concatenate([qi, ki, wi, zeros(3 * LANES - qiw - IDX_DIM - IDX_HEADS),
                               dt, zeros(LANES - SSM_HEADS)], axis=1)
    packed = jnp.concatenate([gates, u_conv, u_pool, q, z, xbc[:, :mixw], idx_blk, kv, xbc[:, mixw:]], axis=1)
    widths = dict(gates=4 * d_model, conv=2 * mixw, pool=mixw, q=qw, z=mixw, xs=mixw, idx=4 * LANES,
                  kv=kvw, bc=bcw)
    blocks, off = {}, 0
    for name in ("gates", "conv", "pool", "q", "z", "xs", "idx", "kv", "bc"):
        assert off % widths[name] == 0
        blocks[name] = off // widths[name]
        off += widths[name]
    blocks["dt"] = 3
    return packed.astype(BF16), blocks


def _pad_lanes(v):
    return jnp.zeros((1, LANES), F32).at[0, :v.shape[0]].set(v.astype(F32))


def _pick_tile(n, target):
    t = min(n, target)
    while n % t:
        t //= 2
    return t


def _pick_lane_tile(n, target):
    best = None
    for m in range(LANES, min(n, target) + 1, LANES):
        if n % m == 0:
            best = m
    return n if best is None else best


def kernel(x, norm_mix, w_in, pool_w, pool_scale, conv_dw, conv_b, conv_ln_g, conv_ln_b, ssm_conv_w, ssm_conv_b, ssm_dt_bias, ssm_a_log, ssm_d, ssm_norm, w_br, w_out, norm_ffn, ffn_w_gate, ffn_w_up, ffn_w_down, moe_router, moe_w_gate, moe_w_up, moe_w_down, final_norm):
    bsz, seq, d = x.shape
    depth = norm_mix.shape[0]
    mixw = pool_scale.shape[-1]
    t = bsz * seq
    row = lambda v: v.reshape(1, -1).astype(F32)

    tab_att = _rope_tables(seq, HEAD_DIM, ROPE_DIM, LANES // HEAD_DIM)
    tab_idx = _rope_tables(seq, IDX_DIM, IDX_ROPE_DIM, LANES // IDX_DIM)
    tab_tail = _rope_tables(seq, IDX_DIM, IDX_ROPE_DIM, 1)

    tm = _pick_tile(t, 1024)
    ts_seq = _pick_tile(seq, 512)
    fin = row(final_norm)

    xf = x.reshape(t, d)
    for layer in range(depth):
        w_packed, blk = _pack_w_in(w_in[layer], d, mixw)
        idx_w = 4 * LANES
        proj, aux = _norm_matmul(xf, row(norm_mix[layer]), w_packed, _pick_tile(t, 2048),
                                 _pick_tile(w_packed.shape[1], 1024), blk["idx"] * idx_w, idx_w)

        y_a = _pool_mixer(proj, blk["pool"], pool_w[layer].astype(BF16), row(pool_scale[layer]), bsz, seq, ts_seq)
        qt, k_r, vt, qit, ki_r, wt = _rope_prep(proj, aux, blk["q"], blk["kv"], tab_att, tab_idx,
                                                tab_tail, bsz, seq, ts_seq)
        y_b = _dsa_attention(qt, qit, wt, k_r, ki_r, vt, bsz, seq)
        y_c = _conformer(proj, blk["conv"], conv_dw[layer], row(conv_b[layer]), row(conv_ln_g[layer]),
                         row(conv_ln_b[layer]), bsz, seq, _pick_tile(seq, 512))
        y_d = _ssd(proj, aux, blk["z"], blk["xs"], blk["bc"], blk["dt"], ssm_conv_w[layer], row(ssm_conv_b[layer]),
                   _pad_lanes(ssm_dt_bias[layer]), _pad_lanes(-jnp.exp(ssm_a_log[layer].astype(F32))),
                   _pad_lanes(ssm_d[layer]), row(ssm_norm[layer]), bsz, seq)
        xf = _merge(xf, y_a, y_b, y_c, y_d, proj, blk["gates"], w_br[layer].astype(BF16),
                    w_out[layer].astype(BF16), _pick_tile(t, 512))

        last = layer == depth - 1
        jj = layer // 2
        if layer % 2 == 0:
            ff = ffn_w_gate.shape[-1]
            xf = _ffn(xf, row(norm_ffn[layer]), ffn_w_gate[jj].astype(BF16), ffn_w_up[jj].astype(BF16),
                      ffn_w_down[jj].astype(BF16), fin, last, _pick_tile(t, 512), _pick_lane_tile(ff, 1408))
        else:
            ff = moe_w_gate.shape[-1]
            xf = _moe(xf, row(norm_ffn[layer]), moe_router[jj], moe_w_gate[jj].astype(BF16),
                      moe_w_up[jj].astype(BF16), moe_w_down[jj].astype(BF16), fin, last, tm,
                      _pick_tile(t, 1024), _pick_lane_tile(ff, 896))
    if depth == 0:
        raise ValueError("depth must be positive")
    return xf.reshape(bsz, seq, d)
```

```python
import functools

import numpy as np
import jax
import jax.numpy as jnp
from jax import lax
from jax.experimental import pallas as pl
from jax.experimental.pallas import tpu as pltpu

POOL_WINDOWS = (2, 4, 8, 16)
ATT_HEADS = 8
ATT_KV_HEADS = 2
HEAD_DIM = 64
ROPE_DIM = HEAD_DIM // 4
ROPE_THETA = 500000.0
IDX_HEADS = 8
IDX_DIM = 32
IDX_ROPE_DIM = IDX_DIM // 4
TOPK_MAX = 256
SSM_HEADS = 8
SSM_HEAD_DIM = 64
SSM_GROUPS = 2
SSM_STATE = 64
SSM_CHUNK = 128
TOP_K = 2
NORM_EPS = 1e-6

LANES = 128
SUBLANES = 8
VMEM_LIMIT_BYTES = 56 * 1024 * 1024

F32 = jnp.float32
BF16 = jnp.bfloat16
INT_MIN = -(2 ** 31)
NEG_BIG = -1e30


def _cparams(n_axes):
    return pltpu.CompilerParams(dimension_semantics=("arbitrary",) * n_axes,
                                vmem_limit_bytes=VMEM_LIMIT_BYTES)


def _sigmoid(x):
    return 1.0 / (1.0 + jnp.exp(-x))


def _rms_rows(x, g):
    return x * lax.rsqrt(jnp.mean(x * x, axis=-1, keepdims=True) + NORM_EPS) * g


def _norm_matmul_kernel(x_ref, g_ref, w_ref, o_ref, aux_ref, h_ref, *, aux_j, aux_off):
    @pl.when(pl.program_id(1) == 0)
    def _():
        h_ref[...] = _rms_rows(x_ref[...], g_ref[...]).astype(h_ref.dtype)

    res = jnp.dot(h_ref[...], w_ref[...], preferred_element_type=F32)
    o_ref[...] = res.astype(o_ref.dtype)

    @pl.when(pl.program_id(1) == aux_j)
    def _():
        aux_ref[...] = res[:, aux_off:aux_off + aux_ref.shape[1]]


def _norm_matmul(x, g, w, tm, tn, aux_col, aux_w):
    t, d = x.shape
    n = w.shape[1]
    assert aux_col // tn == (aux_col + aux_w - 1) // tn
    kern = functools.partial(_norm_matmul_kernel, aux_j=aux_col // tn, aux_off=aux_col % tn)
    return pl.pallas_call(
        kern,
        grid=(t // tm, n // tn),
        in_specs=[pl.BlockSpec((tm, d), lambda i, j: (i, 0)),
                  pl.BlockSpec((1, d), lambda i, j: (0, 0)),
                  pl.BlockSpec((d, tn), lambda i, j: (0, j))],
        out_specs=[pl.BlockSpec((tm, tn), lambda i, j: (i, j)),
                   pl.BlockSpec((tm, aux_w), lambda i, j: (i, 0))],
        out_shape=[jax.ShapeDtypeStruct((t, n), BF16), jax.ShapeDtypeStruct((t, aux_w), F32)],
        scratch_shapes=[pltpu.VMEM((tm, d), BF16)],
        compiler_params=_cparams(2),
        name="norm_in_proj",
    )(x, g, w)


POOL_HIST = 16


def _pool_kernel(u_ref, w_ref, sc_ref, o_ref, buf_ref):
    s = pl.program_id(1)
    ts = u_ref.shape[0]
    gw = w_ref.shape[1]

    @pl.when(s == 0)
    def _():
        buf_ref[0:POOL_HIST, :] = jnp.zeros((POOL_HIST, buf_ref.shape[1]), F32)

    @pl.when(s > 0)
    def _():
        buf_ref[0:POOL_HIST, :] = buf_ref[ts:ts + POOL_HIST, :]

    buf_ref[POOL_HIST:POOL_HIST + ts, :] = u_ref[...].astype(F32)
    pos = s * ts + lax.broadcasted_iota(jnp.int32, (ts, 1), 0)
    for g, win in enumerate(POOL_WINDOWS):
        cols = slice(g * gw, (g + 1) * gw)
        acc = buf_ref[POOL_HIST:POOL_HIST + ts, cols]
        cur = acc
        for k in range(1, win):
            acc = acc + buf_ref[POOL_HIST - k:POOL_HIST - k + ts, cols]
        cnt = jnp.minimum(pos + 1, win).astype(F32)
        p = acc / cnt - cur
        y = jnp.dot(p.astype(BF16), w_ref[g], preferred_element_type=F32)
        o_ref[:, cols] = (y * sc_ref[:, cols]).astype(o_ref.dtype)


def _pool_mixer(proj, col_block, w, scale, bsz, seq, ts):
    mixw = scale.shape[-1]
    nblk = seq // ts
    return pl.pallas_call(
        _pool_kernel,
        grid=(bsz, nblk),
        in_specs=[pl.BlockSpec((ts, mixw), lambda b, s: (b * nblk + s, col_block)),
                  pl.BlockSpec(w.shape, lambda b, s: (0, 0, 0)),
                  pl.BlockSpec((1, mixw), lambda b, s: (0, 0))],
        out_specs=pl.BlockSpec((ts, mixw), lambda b, s: (b * nblk + s, 0)),
        out_shape=jax.ShapeDtypeStruct((bsz * seq, mixw), BF16),
        scratch_shapes=[pltpu.VMEM((POOL_HIST + ts, mixw), F32)],
        compiler_params=_cparams(2),
        name="pool_mixer",
    )(proj, w, scale)


CONV_HIST = 32


def _conformer_kernel(u_ref, dw_ref, db_ref, lg_ref, lb_ref, o_ref, buf_ref, sh_ref, tmp_ref):
    s = pl.program_id(1)
    ts = u_ref.shape[0]
    c = o_ref.shape[1]
    width = dw_ref.shape[0]

    @pl.when(s == 0)
    def _():
        buf_ref[0:CONV_HIST, :] = jnp.zeros((CONV_HIST, c), F32)

    @pl.when(s > 0)
    def _():
        buf_ref[0:CONV_HIST, :] = buf_ref[ts:ts + CONV_HIST, :]

    a = u_ref[:, 0:c].astype(F32)
    gt = u_ref[:, c:2 * c].astype(F32)
    buf_ref[CONV_HIST:CONV_HIST + ts, :] = a * _sigmoid(gt)
    nsh = CONV_HIST + ts - SUBLANES
    for sft in range(1, SUBLANES):
        sh_ref[sft - 1, 0:nsh, :] = buf_ref[sft:sft + nsh, :]
    rc = min(ts, LANES)
    for r0 in range(0, ts, rc):
        for c0 in range(0, c, LANES):
            acc = jnp.zeros((rc, LANES), F32) + db_ref[:, c0:c0 + LANES]
            for k in range(width):
                off = CONV_HIST - (width - 1) + k + r0
                sft = off % SUBLANES
                if sft == 0:
                    tap = buf_ref[off:off + rc, c0:c0 + LANES]
                else:
                    tap = sh_ref[sft - 1, off - sft:off - sft + rc, c0:c0 + LANES]
                acc = acc + tap * dw_ref[k:k + 1, c0:c0 + LANES]
            tmp_ref[r0:r0 + rc, c0:c0 + LANES] = acc
    acc = tmp_ref[...]
    mu = jnp.mean(acc, axis=-1, keepdims=True)
    xc = acc - mu
    y = xc * lax.rsqrt(jnp.mean(xc * xc, axis=-1, keepdims=True) + NORM_EPS)
    y = y * lg_ref[...] + lb_ref[...]
    o_ref[...] = (y * _sigmoid(y)).astype(o_ref.dtype)


def _conformer(proj, col_block, dw, db, lg, lb, bsz, seq, ts):
    c = dw.shape[1]
    nblk = seq // ts
    return pl.pallas_call(
        _conformer_kernel,
        grid=(bsz, nblk),
        in_specs=[pl.BlockSpec((ts, 2 * c), lambda b, s: (b * nblk + s, col_block)),
                  pl.BlockSpec(dw.shape, lambda b, s: (0, 0)),
                  pl.BlockSpec((1, c), lambda b, s: (0, 0)),
                  pl.BlockSpec((1, c), lambda b, s: (0, 0)),
                  pl.BlockSpec((1, c), lambda b, s: (0, 0))],
        out_specs=pl.BlockSpec((ts, c), lambda b, s: (b * nblk + s, 0)),
        out_shape=jax.ShapeDtypeStruct((bsz * seq, c), BF16),
        scratch_shapes=[pltpu.VMEM((CONV_HIST + ts, c), F32),
                        pltpu.VMEM((SUBLANES - 1, CONV_HIST + ts, c), F32),
                        pltpu.VMEM((ts, c), F32)],
        compiler_params=_cparams(2),
        name="conformer_conv",
    )(proj, dw, db, lg, lb)


SSM_HIST = 8


def _ssd_kernel(z_ref, xs_ref, bc_ref, dt_ref, cw_ref, cb_ref, dtb_ref, a_ref, dsk_ref, ng_ref,
                o_ref, y_ref, xbuf_ref, bcbuf_ref, state_ref):
    c_idx = pl.program_id(1)
    L = xs_ref.shape[0]
    mixw = xs_ref.shape[1]
    bcw = bc_ref.shape[1]
    width = cw_ref.shape[0]
    hp = SSM_HEAD_DIM
    ns = SSM_STATE
    rep = SSM_HEADS // SSM_GROUPS

    @pl.when(c_idx == 0)
    def _():
        xbuf_ref[0:SSM_HIST, :] = jnp.zeros((SSM_HIST, mixw), F32)
        bcbuf_ref[0:SSM_HIST, :] = jnp.zeros((SSM_HIST, bcw), F32)
        state_ref[...] = jnp.zeros(state_ref.shape, F32)

    @pl.when(c_idx > 0)
    def _():
        xbuf_ref[0:SSM_HIST, :] = xbuf_ref[L:L + SSM_HIST, :]
        bcbuf_ref[0:SSM_HIST, :] = bcbuf_ref[L:L + SSM_HIST, :]

    xbuf_ref[SSM_HIST:SSM_HIST + L, :] = xs_ref[...].astype(F32)
    bcbuf_ref[SSM_HIST:SSM_HIST + L, :] = bc_ref[...].astype(F32)

    xc = jnp.zeros((L, mixw), F32) + cb_ref[:, 0:mixw]
    bcc = jnp.zeros((L, bcw), F32) + cb_ref[:, mixw:mixw + bcw]
    for k in range(width):
        off = SSM_HIST - (width - 1) + k
        xc = xc + xbuf_ref[off:off + L, :] * cw_ref[k:k + 1, 0:mixw]
        bcc = bcc + bcbuf_ref[off:off + L, :] * cw_ref[k:k + 1, mixw:mixw + bcw]
    xc = xc * _sigmoid(xc)
    bcc = bcc * _sigmoid(bcc)

    dtr = dt_ref[...] + dtb_ref[...]
    dt = jnp.maximum(dtr, 0.0) + jnp.log(1.0 + jnp.exp(-jnp.abs(dtr)))
    da = dt * a_ref[...]
    row_i = lax.broadcasted_iota(jnp.int32, (L, L), 0)
    col_i = lax.broadcasted_iota(jnp.int32, (L, L), 1)
    tri = col_i <= row_i
    a_cum = jnp.dot(tri.astype(F32), da, preferred_element_type=F32,
                    precision=lax.Precision.HIGHEST)
    a_cum_t = a_cum.T
    a_last = a_cum[L - 1:L, :]
    dec_last = jnp.exp(a_last)
    dec_out = jnp.exp(a_cum)
    dec_st = jnp.exp(a_last - a_cum)

    cbs = []
    for g in range(SSM_GROUPS):
        bg = bcc[:, g * ns:(g + 1) * ns].astype(BF16)
        cg = bcc[:, SSM_GROUPS * ns + g * ns:SSM_GROUPS * ns + (g + 1) * ns].astype(BF16)
        cbs.append(lax.dot_general(cg, bg, (((1,), (1,)), ((), ())), preferred_element_type=F32))

    for h in range(SSM_HEADS):
        g = h // rep
        x_h = xc[:, h * hp:(h + 1) * hp]
        b_g = bcc[:, g * ns:(g + 1) * ns]
        c_g = bcc[:, SSM_GROUPS * ns + g * ns:SSM_GROUPS * ns + (g + 1) * ns]
        xd = x_h * dt[:, h:h + 1]
        seg = a_cum[:, h:h + 1] - a_cum_t[h:h + 1, :]
        lmat = jnp.exp(jnp.where(tri, seg, -jnp.inf))
        y_diag = jnp.dot((cbs[g] * lmat).astype(BF16), xd.astype(BF16), preferred_element_type=F32)
        st = state_ref[h]
        y_off = jnp.dot(c_g.astype(BF16), st.astype(BF16), preferred_element_type=F32) * dec_out[:, h:h + 1]
        bd = (b_g * dec_st[:, h:h + 1]).astype(BF16)
        st_new = lax.dot_general(bd, xd.astype(BF16), (((0,), (0,)), ((), ())),
                                 preferred_element_type=F32)
        state_ref[h] = st * dec_last[:, h:h + 1] + st_new
        y_h = y_diag + y_off + dsk_ref[:, h:h + 1] * x_h
        zz = z_ref[:, h * hp:(h + 1) * hp].astype(F32)
        y_ref[:, h * hp:(h + 1) * hp] = y_h * (zz * _sigmoid(zz))

    o_ref[...] = _rms_rows(y_ref[...], ng_ref[...]).astype(o_ref.dtype)


def _ssd(proj, aux, z_block, xs_block, bc_block, dt_block, cw, cb, dtb, a_neg, dsk, ng, bsz, seq):
    mixw = ng.shape[-1]
    bcw = cw.shape[1] - mixw
    L = SSM_CHUNK
    nblk = seq // L
    row = lambda b, s: b * nblk + s
    return pl.pallas_call(
        _ssd_kernel,
        grid=(bsz, nblk),
        in_specs=[pl.BlockSpec((L, mixw), lambda b, s: (row(b, s), z_block)),
                  pl.BlockSpec((L, mixw), lambda b, s: (row(b, s), xs_block)),
                  pl.BlockSpec((L, bcw), lambda b, s: (row(b, s), bc_block)),
                  pl.BlockSpec((L, LANES), lambda b, s: (row(b, s), dt_block)),
                  pl.BlockSpec(cw.shape, lambda b, s: (0, 0)),
                  pl.BlockSpec(cb.shape, lambda b, s: (0, 0)),
                  pl.BlockSpec((1, LANES), lambda b, s: (0, 0)),
                  pl.BlockSpec((1, LANES), lambda b, s: (0, 0)),
                  pl.BlockSpec((1, LANES), lambda b, s: (0, 0)),
                  pl.BlockSpec((1, mixw), lambda b, s: (0, 0))],
        out_specs=pl.BlockSpec((L, mixw), lambda b, s: (row(b, s), 0)),
        out_shape=jax.ShapeDtypeStruct((bsz * seq, mixw), BF16),
        scratch_shapes=[pltpu.VMEM((L, mixw), F32),
                        pltpu.VMEM((SSM_HIST + L, mixw), F32),
                        pltpu.VMEM((SSM_HIST + L, bcw), F32),
                        pltpu.VMEM((SSM_HEADS, SSM_STATE, SSM_HEAD_DIM), F32)],
        compiler_params=_cparams(2),
        name="ssd_mixer",
    )(proj, proj, proj, aux, cw, cb, dtb, a_neg, dsk, ng)


def _rope_cols(x, c, s1, s2, half):
    return (x * c + pltpu.roll(x, LANES - half, axis=1) * s1 + pltpu.roll(x, half, axis=1) * s2)


LOG2E = 1.4426950408889634


def _rope_prep_kernel(q_ref, kv_ref, idx_ref, ta_ref, ti_ref, tt_ref,
                      qg_ref, k_ref, vt_ref, qig_ref, ki_ref, wg_ref):
    hd = HEAD_DIM
    ts = q_ref.shape[0]
    rep = ATT_HEADS // ATT_KV_HEADS
    scale = (hd ** -0.5) * LOG2E
    ca, s1a, s2a = ta_ref[0], ta_ref[1], ta_ref[2]
    heads_per_col = LANES // hd
    for c in range(q_ref.shape[1] // LANES):
        qc = _rope_cols(q_ref[:, c * LANES:(c + 1) * LANES].astype(F32), ca, s1a, s2a, ROPE_DIM // 2) * scale
        qct = qc.T.astype(qg_ref.dtype)
        for r in range(heads_per_col):
            h = c * heads_per_col + r
            g, hh = h // rep, h % rep
            for jb in range(ts // LANES):
                dst = jb * rep * LANES + hh * LANES
                qg_ref[g, :, dst:dst + LANES] = qct[r * hd:(r + 1) * hd, jb * LANES:(jb + 1) * LANES]
    kw = ATT_KV_HEADS * hd
    kc = _rope_cols(kv_ref[:, 0:kw].astype(F32), ca, s1a, s2a, ROPE_DIM // 2)
    for g in range(ATT_KV_HEADS):
        k_ref[g] = kc[:, g * hd:(g + 1) * hd].astype(k_ref.dtype)
    vt_ref[...] = kv_ref[:, kw:2 * kw].astype(F32).T.astype(vt_ref.dtype)
    ci, s1i, s2i = ti_ref[0], ti_ref[1], ti_ref[2]
    nq = IDX_HEADS * IDX_DIM
    iheads_per_col = LANES // IDX_DIM
    for c in range(nq // LANES):
        qc = _rope_cols(idx_ref[:, c * LANES:(c + 1) * LANES], ci, s1i, s2i, IDX_ROPE_DIM // 2)
        qct = qc.T.astype(qig_ref.dtype)
        for r in range(iheads_per_col):
            h = c * iheads_per_col + r
            for jb in range(ts // LANES):
                dst = jb * IDX_HEADS * LANES + h * LANES
                qig_ref[:, dst:dst + LANES] = qct[r * IDX_DIM:(r + 1) * IDX_DIM, jb * LANES:(jb + 1) * LANES]
    tail = _rope_cols(idx_ref[:, nq:nq + LANES], tt_ref[0], tt_ref[1], tt_ref[2], IDX_ROPE_DIM // 2)
    ki_ref[...] = tail[:, 0:IDX_DIM].astype(ki_ref.dtype)
    tail_t = tail.T
    for h in range(IDX_HEADS):
        for jb in range(ts // LANES):
            dst = jb * IDX_HEADS * LANES + h * LANES
            wg_ref[:, dst:dst + LANES] = tail_t[IDX_DIM + h:IDX_DIM + h + 1, jb * LANES:(jb + 1) * LANES]


def _rope_prep(proj, aux, q_block, kv_block, tab_att, tab_idx, tab_tail, bsz, seq, ts):
    nblk = seq // ts
    qw = ATT_HEADS * HEAD_DIM
    kw = ATT_KV_HEADS * HEAD_DIM
    idxw = 4 * LANES
    rep = ATT_HEADS // ATT_KV_HEADS
    row = lambda b, s: b * nblk + s
    tab_spec = pl.BlockSpec((3, ts, LANES), lambda b, s: (0, s, 0))
    return pl.pallas_call(
        _rope_prep_kernel,
        grid=(bsz, nblk),
        in_specs=[pl.BlockSpec((ts, qw), lambda b, s: (row(b, s), q_block)),
                  pl.BlockSpec((ts, 2 * kw), lambda b, s: (row(b, s), kv_block)),
                  pl.BlockSpec((ts, idxw), lambda b, s: (row(b, s), 0)),
                  tab_spec, tab_spec, tab_spec],
        out_specs=[pl.BlockSpec((None, ATT_KV_HEADS, HEAD_DIM, rep * ts), lambda b, s: (b, 0, 0, s)),
                   pl.BlockSpec((None, ATT_KV_HEADS, ts, HEAD_DIM), lambda b, s: (b, 0, s, 0)),
                   pl.BlockSpec((None, kw, ts), lambda b, s: (b, 0, s)),
                   pl.BlockSpec((None, IDX_DIM, IDX_HEADS * ts), lambda b, s: (b, 0, s)),
                   pl.BlockSpec((None, ts, IDX_DIM), lambda b, s: (b, s, 0)),
                   pl.BlockSpec((None, 1, IDX_HEADS * ts), lambda b, s: (b, 0, s))],
        out_shape=[jax.ShapeDtypeStruct((bsz, ATT_KV_HEADS, HEAD_DIM, rep * seq), BF16),
                   jax.ShapeDtypeStruct((bsz, ATT_KV_HEADS, seq, HEAD_DIM), BF16),
                   jax.ShapeDtypeStruct((bsz, kw, seq), BF16),
                   jax.ShapeDtypeStruct((bsz, IDX_DIM, IDX_HEADS * seq), BF16),
                   jax.ShapeDtypeStruct((bsz, seq, IDX_DIM), BF16),
                   jax.ShapeDtypeStruct((bsz, 1, IDX_HEADS * seq), F32)],
        compiler_params=_cparams(2),
        name="rope_prep",
    )(proj, proj, aux, tab_att, tab_idx, tab_tail)


def _rope_tables(seq, head_dim, rot_dim, n_rot_heads):
    half = rot_dim // 2
    inv_freq = jnp.power(jnp.float32(ROPE_THETA), -jnp.arange(half, dtype=F32) * (2.0 / rot_dim))
    ang = jnp.arange(seq, dtype=F32)[:, None] * inv_freq[None, :]
    cos, sin = jnp.cos(ang), jnp.sin(ang)
    lane = np.arange(LANES)
    j = lane % head_dim
    rot_head = lane < n_rot_heads * head_dim
    first = rot_head & (j < half)
    second = rot_head & (j >= half) & (j < rot_dim)
    fi = np.where(first, j, 0)
    si = np.where(second, j - half, 0)
    c = jnp.where(first[None, :], cos[:, fi], jnp.where(second[None, :], cos[:, si], 1.0))
    s1 = jnp.where(first[None, :], -sin[:, fi], 0.0)
    s2 = jnp.where(second[None, :], sin[:, si], 0.0)
    return jnp.stack([c, s1, s2]).astype(F32)


def _sortable_key(score):
    score = jnp.where(score == 0.0, 0.0, score)
    bits = pltpu.bitcast(score, jnp.int32)
    return bits ^ (lax.shift_right_arithmetic(bits, 31) & jnp.int32(0x7FFFFFFF))


KEY_CHUNK = 256
COUNT_ROWS = 4 * SUBLANES


def _dsa_kernel(qg_ref, qig_ref, wg_ref, k_ref, ki_ref, vt_ref, o_ref, keys_ref, acc_ref, m_ref, s_ref,
                sa_ref, la_ref,
                *, top_k, seq_bits):
    j = pl.program_id(1)
    qb = LANES
    kc = KEY_CHUNK
    hd = HEAD_DIM
    rep = ATT_HEADS // ATT_KV_HEADS
    n_kc = lax.div(j * qb + (qb + kc - 1), kc)
    sub_i = lax.broadcasted_iota(jnp.int32, (kc, qb), 0)
    q_pos = j * qb + lax.broadcasted_iota(jnp.int32, (kc, qb), 1)

    n_pair = lax.div(j * qb + (qb + 2 * kc - 1), 2 * kc)
    last_chunk = 2 * n_pair - 1

    def idx_matmul(c):
        start = pl.multiple_of(c * kc, kc)
        return jnp.dot(ki_ref[pl.ds(start, kc), :], qig_ref[...], preferred_element_type=F32)

    def idx_keys(c, s_all):
        start = pl.multiple_of(c * kc, kc)
        score = jnp.zeros((kc, qb), F32)
        for h in range(IDX_HEADS):
            score = score + jnp.maximum(s_all[:, h * qb:(h + 1) * qb], 0.0) * wg_ref[:, h * qb:(h + 1) * qb]
        causal = (start + sub_i) <= q_pos
        keys_ref[pl.ds(start, kc), :] = jnp.where(causal, _sortable_key(score), jnp.int32(INT_MIN))

    def qk_logits(c):
        start = pl.multiple_of(c * kc, kc)
        return [jnp.dot(k_ref[g, pl.ds(start, kc), :], qg_ref[g], preferred_element_type=F32)
                for g in range(ATT_KV_HEADS)]

    sa_ref[...] = idx_matmul(0)
    for g, lg in enumerate(qk_logits(0)):
        la_ref[g] = lg

    def score_pair(p, carry):
        s_b = idx_matmul(2 * p + 1)
        idx_keys(2 * p, sa_ref[...])
        sa_ref[...] = idx_matmul(jnp.minimum(2 * p + 2, last_chunk))
        idx_keys(2 * p + 1, s_b)
        return carry

    lax.fori_loop(0, n_pair, score_pair, 0)

    def count_where(pred_fn):
        def body(c, acc):
            start = pl.multiple_of(c * kc, kc)
            ones = jnp.where(pred_fn(keys_ref[pl.ds(start, kc), :], start), 1.0, 0.0)
            return acc + jnp.sum(ones.reshape(kc // COUNT_ROWS, COUNT_ROWS, qb), axis=0)
        acc = lax.fori_loop(0, n_kc, body, jnp.zeros((COUNT_ROWS, qb), F32))
        return jnp.sum(acc, axis=0, keepdims=True)

    def count_ge(cand):
        def body(p, acc):
            start = pl.multiple_of(p * (2 * kc), 2 * kc)
            ones = jnp.where(keys_ref[pl.ds(start, 2 * kc), :] >= cand, 1.0, 0.0)
            return acc + jnp.sum(ones.reshape(2 * kc // COUNT_ROWS, COUNT_ROWS, qb), axis=0)
        acc = lax.fori_loop(0, n_pair, body, jnp.zeros((COUNT_ROWS, qb), F32))
        return jnp.sum(acc, axis=0, keepdims=True)

    def bit_body(i, carry):
        tau, n_ge = carry
        cand = tau ^ lax.shift_left(jnp.int32(1), 31 - i)
        cnt = count_ge(cand)
        ok = cnt >= top_k
        return jnp.where(ok, cand, tau), jnp.where(ok, cnt, n_ge)

    tau, n_ge = lax.fori_loop(0, 32, bit_body,
                              (jnp.full((1, qb), INT_MIN, jnp.int32), jnp.zeros((1, qb), F32)))
    tau = jnp.maximum(tau, jnp.int32(INT_MIN + 1))

    has_excess = jnp.max(jnp.where(n_ge > top_k, 1.0, 0.0)) > 0.5

    @pl.when(has_excess)
    def _():
        n_gt = count_where(lambda kk, _: kk > tau)
        need = top_k - n_gt

        def pos_body(i, xcut):
            cand = xcut | lax.shift_left(jnp.int32(1), seq_bits - 1 - i)
            cnt = count_where(lambda kk, st: (kk == tau) & ((st + sub_i) < cand))
            return jnp.where(cnt < need, cand, xcut)

        xcut = lax.fori_loop(0, seq_bits, pos_body, jnp.zeros((1, qb), jnp.int32))
        def fix_body(c, carry):
            start = pl.multiple_of(c * kc, kc)
            kk = keys_ref[pl.ds(start, kc), :]
            drop = (kk == tau) & (((start + sub_i) > xcut) | (need < 1))
            keys_ref[pl.ds(start, kc), :] = jnp.where(drop, jnp.int32(INT_MIN), kk)
            return carry

        lax.fori_loop(0, n_kc, fix_body, 0)

    m_ref[...] = jnp.full(m_ref.shape, NEG_BIG, F32)
    s_ref[...] = jnp.zeros(s_ref.shape, F32)
    acc_ref[...] = jnp.zeros(acc_ref.shape, F32)

    def softmax_pv(c, lgs):
        start = pl.multiple_of(c * kc, kc)
        bias = jnp.where(keys_ref[pl.ds(start, kc), :] >= tau, 0.0, NEG_BIG)
        for g in range(ATT_KV_HEADS):
            ps, alphas = [], []
            for hh in range(rep):
                h = g * rep + hh
                logit = lgs[g][:, hh * qb:(hh + 1) * qb] + bias
                m_old = m_ref[h:h + 1, :]
                m_new = jnp.maximum(m_old, jnp.max(logit, axis=0, keepdims=True))
                p = jnp.exp2(logit - m_new)
                alpha = jnp.exp2(m_old - m_new)
                s_ref[h:h + 1, :] = s_ref[h:h + 1, :] * alpha + jnp.sum(p, axis=0, keepdims=True)
                m_ref[h:h + 1, :] = m_new
                ps.append(p.astype(BF16))
                alphas.append(alpha)
            pv = jnp.dot(vt_ref[g * hd:(g + 1) * hd, pl.ds(start, kc)], jnp.concatenate(ps, axis=1),
                         preferred_element_type=F32)
            for hh in range(rep):
                rows = slice((g * rep + hh) * hd, (g * rep + hh + 1) * hd)
                acc_ref[rows, :] = acc_ref[rows, :] * alphas[hh] + pv[:, hh * qb:(hh + 1) * qb]

    def att_pair(p, carry):
        l_b = qk_logits(2 * p + 1)
        softmax_pv(2 * p, [la_ref[g] for g in range(ATT_KV_HEADS)])
        for g, lg in enumerate(qk_logits(jnp.minimum(2 * p + 2, last_chunk))):
            la_ref[g] = lg
        softmax_pv(2 * p + 1, l_b)
        return carry

    lax.fori_loop(0, n_pair, att_pair, 0)
    for h in range(ATT_HEADS):
        rows = slice(h * hd, (h + 1) * hd)
        acc_ref[rows, :] = acc_ref[rows, :] / s_ref[h:h + 1, :]
    for c0 in range(0, ATT_HEADS * hd, LANES):
        o_ref[:, c0:c0 + LANES] = acc_ref[c0:c0 + LANES, :].T.astype(o_ref.dtype)


def _dsa_attention(qg, qig, wg, k, ki, vt, bsz, seq):
    qb = LANES
    nblk = seq // qb
    qw = ATT_HEADS * HEAD_DIM
    kw = ATT_KV_HEADS * HEAD_DIM
    rep = ATT_HEADS // ATT_KV_HEADS
    top_k = min(TOPK_MAX, seq // 4)
    seq_bits = int(np.log2(seq))
    assert 2 ** seq_bits == seq and seq % (2 * KEY_CHUNK) == 0
    kern = functools.partial(_dsa_kernel, top_k=top_k, seq_bits=seq_bits)
    return pl.pallas_call(
        kern,
        grid=(bsz, nblk),
        in_specs=[pl.BlockSpec((None, ATT_KV_HEADS, HEAD_DIM, rep * qb), lambda b, j: (b, 0, 0, j)),
                  pl.BlockSpec((None, IDX_DIM, IDX_HEADS * qb), lambda b, j: (b, 0, j)),
                  pl.BlockSpec((None, 1, IDX_HEADS * qb), lambda b, j: (b, 0, j)),
                  pl.BlockSpec((None, ATT_KV_HEADS, seq, HEAD_DIM), lambda b, j: (b, 0, 0, 0)),
                  pl.BlockSpec((None, seq, IDX_DIM), lambda b, j: (b, 0, 0)),
                  pl.BlockSpec((None, kw, seq), lambda b, j: (b, 0, 0))],
        out_specs=pl.BlockSpec((qb, qw), lambda b, j: (b * nblk + j, 0)),
        out_shape=jax.ShapeDtypeStruct((bsz * seq, qw), BF16),
        scratch_shapes=[pltpu.VMEM((seq, qb), jnp.int32), pltpu.VMEM((qw, qb), F32),
                        pltpu.VMEM((ATT_HEADS, qb), F32), pltpu.VMEM((ATT_HEADS, qb), F32),
                        pltpu.VMEM((KEY_CHUNK, IDX_HEADS * qb), F32),
                        pltpu.VMEM((ATT_KV_HEADS, KEY_CHUNK, rep * qb), F32)],
        compiler_params=_cparams(2),
        name="dsa_attention",
    )(qg, qig, wg, k, ki, vt)


def _merge_kernel(x_ref, ya_ref, yb_ref, yc_ref, yd_ref, gl_ref, wbr_ref, wout_ref, o_ref):
    d = x_ref.shape[1]
    merged = None
    for n, y_ref in enumerate((ya_ref, yb_ref, yc_ref, yd_ref)):
        u = jnp.dot(y_ref[...], wbr_ref[n], preferred_element_type=F32)
        term = _sigmoid(gl_ref[:, n * d:(n + 1) * d].astype(F32)) * u
        merged = term if merged is None else merged + term
    o_ref[...] = x_ref[...] + jnp.dot(merged.astype(BF16), wout_ref[...], preferred_element_type=F32)


def _merge(x, ya, yb, yc, yd, proj, gate_block, wbr, wout, tm):
    t, d = x.shape
    mixw = ya.shape[1]
    nbr = wbr.shape[0]
    y_spec = pl.BlockSpec((tm, mixw), lambda i: (i, 0))
    return pl.pallas_call(
        _merge_kernel,
        grid=(t // tm,),
        in_specs=[pl.BlockSpec((tm, d), lambda i: (i, 0)), y_spec, y_spec, y_spec, y_spec,
                  pl.BlockSpec((tm, nbr * d), lambda i: (i, gate_block)),
                  pl.BlockSpec(wbr.shape, lambda i: (0, 0, 0)),
                  pl.BlockSpec(wout.shape, lambda i: (0, 0))],
        out_specs=pl.BlockSpec((tm, d), lambda i: (i, 0)),
        out_shape=jax.ShapeDtypeStruct((t, d), F32),
        compiler_params=_cparams(1),
        name="merge_out_proj",
    )(x, ya, yb, yc, yd, proj, wbr, wout)


def _ffn_kernel(x_ref, g_ref, wg_ref, wu_ref, wd_ref, fin_ref, o_ref, h_ref, acc_ref, *, final_norm):
    f = pl.program_id(1)

    @pl.when(f == 0)
    def _():
        h_ref[...] = _rms_rows(x_ref[...], g_ref[...]).astype(h_ref.dtype)
        acc_ref[...] = jnp.zeros(acc_ref.shape, F32)

    h = h_ref[...]
    a = jnp.dot(h, wg_ref[...], preferred_element_type=F32)
    b = jnp.dot(h, wu_ref[...], preferred_element_type=F32)
    t = (a * _sigmoid(a)) * b
    acc_ref[...] += jnp.dot(t.astype(BF16), wd_ref[...], preferred_element_type=F32)

    @pl.when(f == pl.num_programs(1) - 1)
    def _():
        y = x_ref[...] + acc_ref[...]
        o_ref[...] = _rms_rows(y, fin_ref[...]) if final_norm else y


def _ffn(x, g, wg, wu, wd, fin, final_norm, tm, tf):
    t, d = x.shape
    ff = wg.shape[1]
    return pl.pallas_call(
        functools.partial(_ffn_kernel, final_norm=final_norm),
        grid=(t // tm, ff // tf),
        in_specs=[pl.BlockSpec((tm, d), lambda i, f: (i, 0)),
                  pl.BlockSpec((1, d), lambda i, f: (0, 0)),
                  pl.BlockSpec((d, tf), lambda i, f: (0, f)),
                  pl.BlockSpec((d, tf), lambda i, f: (0, f)),
                  pl.BlockSpec((tf, d), lambda i, f: (f, 0)),
                  pl.BlockSpec((1, d), lambda i, f: (0, 0))],
        out_specs=pl.BlockSpec((tm, d), lambda i, f: (i, 0)),
        out_shape=jax.ShapeDtypeStruct((t, d), F32),
        scratch_shapes=[pltpu.VMEM((tm, d), BF16), pltpu.VMEM((tm, d), F32)],
        compiler_params=_cparams(2),
        name="dense_swiglu",
    )(x, g, wg, wu, wd, fin)


def _router_kernel(x_ref, g_ref, r_ref, h_ref, mi_ref, mp_ref, cnt_ref, tri_ref, carry_ref, *, n_experts):
    i = pl.program_id(0)
    tm = x_ref.shape[0]
    lane = lax.broadcasted_iota(jnp.int32, (tm, LANES), 1)

    @pl.when(i == 0)
    def _():
        carry_ref[...] = jnp.zeros(carry_ref.shape, F32)
        r_i = lax.broadcasted_iota(jnp.int32, (tm, tm), 0)
        c_i = lax.broadcasted_iota(jnp.int32, (tm, tm), 1)
        tri_ref[...] = jnp.where(c_i < r_i, 1.0, 0.0).astype(tri_ref.dtype)

    hf = _rms_rows(x_ref[...], g_ref[...])
    h_ref[...] = hf
    logits = jnp.dot(hf, r_ref[...], preferred_element_type=F32, precision=lax.Precision.HIGHEST)
    logits = jnp.where(lane < n_experts, logits, -jnp.inf)
    m1 = jnp.max(logits, axis=-1, keepdims=True)
    i1 = jnp.min(jnp.where(logits == m1, lane, LANES), axis=-1, keepdims=True)
    rest = jnp.where(lane == i1, -jnp.inf, logits)
    m2 = jnp.max(rest, axis=-1, keepdims=True)
    i2 = jnp.min(jnp.where(rest == m2, lane, LANES), axis=-1, keepdims=True)
    e2 = jnp.exp(m2 - m1)
    p1 = 1.0 / (1.0 + e2)
    p2 = e2 / (1.0 + e2)
    oh1 = lane == i1
    oh2 = lane == i2
    ohs = jnp.where(oh1, 1.0, jnp.where(oh2, 1.0, 0.0))
    pref = jnp.dot(tri_ref[...], ohs.astype(tri_ref.dtype), preferred_element_type=F32) + carry_ref[...]
    rank1 = jnp.sum(jnp.where(oh1, pref, 0.0), axis=-1, keepdims=True).astype(jnp.int32)
    rank2 = jnp.sum(jnp.where(oh2, pref, 0.0), axis=-1, keepdims=True).astype(jnp.int32)
    carry_ref[...] = carry_ref[...] + jnp.sum(ohs, axis=0, keepdims=True)
    cnt_ref[...] = carry_ref[...]
    mi_ref[...] = jnp.where(lane == 0, i1, jnp.where(lane == 1, i2, jnp.where(lane == 2, rank1,
                                                                          jnp.where(lane == 3, rank2, 0))))
    mp_ref[...] = jnp.where(lane == 0, p1, jnp.where(lane == 1, p2, 0.0))


def _router(x, g, router, tm):
    t, d = x.shape
    n_experts = router.shape[1]
    rpad = jnp.zeros((d, LANES), F32).at[:, :n_experts].set(router.astype(F32))
    return pl.pallas_call(
        functools.partial(_router_kernel, n_experts=n_experts),
        grid=(t // tm,),
        in_specs=[pl.BlockSpec((tm, d), lambda i: (i, 0)),
                  pl.BlockSpec((1, d), lambda i: (0, 0)),
                  pl.BlockSpec((d, LANES), lambda i: (0, 0))],
        out_specs=[pl.BlockSpec((tm, d), lambda i: (i, 0)),
                   pl.BlockSpec((tm, LANES), lambda i: (i, 0)),
                   pl.BlockSpec((tm, LANES), lambda i: (i, 0)),
                   pl.BlockSpec((1, LANES), lambda i: (0, 0))],
        out_shape=[jax.ShapeDtypeStruct((t, d), F32),
                   jax.ShapeDtypeStruct((t, LANES), jnp.int32),
                   jax.ShapeDtypeStruct((t, LANES), F32),
                   jax.ShapeDtypeStruct((1, LANES), F32)],
        scratch_shapes=[pltpu.VMEM((tm, tm), BF16), pltpu.VMEM((1, LANES), F32)],
        compiler_params=_cparams(1),
        name="moe_router",
    )(x, g, rpad)


DMA_UNROLL = 8


def _row_copy(src, dst, sem):
    return pltpu.make_async_copy(src, dst, sem)


def _dispatch_kernel(dest_ref, h_ref, xs_in_ref, xs_ref, sem, *, n_tokens):
    del xs_in_ref
    td = h_ref.shape[0]
    base = pl.program_id(0) * td

    def issue(r, carry):
        for k in range(TOP_K):
            d = dest_ref[k * n_tokens + base + r]
            _row_copy(h_ref.at[pl.ds(r, 1), :], xs_ref.at[pl.ds(d, 1), :], sem).start(priority=k % 2)
        return carry

    def drain(r, carry):
        for k in range(TOP_K):
            _row_copy(h_ref.at[pl.ds(0, 1), :], xs_ref.at[pl.ds(0, 1), :], sem).wait()
        return carry

    lax.fori_loop(0, td, issue, 0, unroll=DMA_UNROLL)
    lax.fori_loop(0, td, drain, 0, unroll=DMA_UNROLL)


def _dispatch(dest, h, n_rows, td):
    t, d = h.shape
    xs0 = jnp.zeros((n_rows, d), F32)
    return pl.pallas_call(
        functools.partial(_dispatch_kernel, n_tokens=t),
        grid_spec=pltpu.PrefetchScalarGridSpec(
            num_scalar_prefetch=1,
            grid=(t // td,),
            in_specs=[pl.BlockSpec((td, d), lambda i, dest: (i, 0)),
                      pl.BlockSpec(memory_space=pl.ANY)],
            out_specs=pl.BlockSpec(memory_space=pl.ANY),
            scratch_shapes=[pltpu.SemaphoreType.DMA(())]),
        out_shape=jax.ShapeDtypeStruct((n_rows, d), F32),
        input_output_aliases={2: 0},
        compiler_params=_cparams(1),
        name="moe_dispatch",
    )(dest, h, xs0)


def _expert_kernel(te_ref, nu_ref, xs_ref, wg_ref, wu_ref, wd_ref, ys_ref, h_ref, acc_ref):
    i = pl.program_id(0)
    f = pl.program_id(1)
    last_f = pl.num_programs(1) - 1
    valid = i < nu_ref[0]

    @pl.when(valid & (f == 0))
    def _():
        h_ref[...] = xs_ref[...].astype(h_ref.dtype)
        acc_ref[...] = jnp.zeros(acc_ref.shape, F32)

    @pl.when(valid)
    def _():
        h = h_ref[...]
        a = jnp.dot(h, wg_ref[...], preferred_element_type=F32)
        b = jnp.dot(h, wu_ref[...], preferred_element_type=F32)
        t = (a * _sigmoid(a)) * b
        acc_ref[...] += jnp.dot(t.astype(BF16), wd_ref[...], preferred_element_type=F32)

    @pl.when(valid & (f == last_f))
    def _():
        ys_ref[...] = acc_ref[...]

    @pl.when(jnp.logical_not(valid) & (f == last_f))
    def _():
        ys_ref[...] = jnp.zeros(ys_ref.shape, F32)


def _expert_mlp(tile_expert, n_used, xs, wg, wu, wd, tg, tf):
    n_rows, d = xs.shape
    ff = wg.shape[-1]
    nf = ff // tf
    fblk = lambda i, f, te, nu: jnp.where(i < nu[0], f, nf - 1)
    return pl.pallas_call(
        _expert_kernel,
        grid_spec=pltpu.PrefetchScalarGridSpec(
            num_scalar_prefetch=2,
            grid=(n_rows // tg, nf),
            in_specs=[pl.BlockSpec((tg, d), lambda i, f, te, nu: (i, 0)),
                      pl.BlockSpec((None, d, tf), lambda i, f, te, nu: (te[i], 0, fblk(i, f, te, nu))),
                      pl.BlockSpec((None, d, tf), lambda i, f, te, nu: (te[i], 0, fblk(i, f, te, nu))),
                      pl.BlockSpec((None, tf, d), lambda i, f, te, nu: (te[i], fblk(i, f, te, nu), 0))],
            out_specs=pl.BlockSpec((tg, d), lambda i, f, te, nu: (i, 0)),
            scratch_shapes=[pltpu.VMEM((tg, d), BF16), pltpu.VMEM((tg, d), F32)]),
        out_shape=jax.ShapeDtypeStruct((n_rows, d), F32),
        compiler_params=_cparams(2),
        name="moe_experts",
    )(tile_expert, n_used, xs, wg, wu, wd)


def _combine_kernel(dest_ref, x_ref, mp_ref, fin_ref, ys_ref, o_ref, buf_ref, sem, *, n_tokens, final_norm):
    tc = x_ref.shape[0]
    base = pl.program_id(0) * tc

    def issue(r, carry):
        for k in range(TOP_K):
            d = dest_ref[k * n_tokens + base + r]
            _row_copy(ys_ref.at[pl.ds(d, 1), :], buf_ref.at[k, pl.ds(r, 1), :], sem).start(priority=k % 2)
        return carry

    def drain(r, carry):
        for k in range(TOP_K):
            _row_copy(ys_ref.at[pl.ds(0, 1), :], buf_ref.at[k, pl.ds(0, 1), :], sem).wait()
        return carry

    lax.fori_loop(0, tc, issue, 0, unroll=DMA_UNROLL)
    lax.fori_loop(0, tc, drain, 0, unroll=DMA_UNROLL)
    y = x_ref[...]
    for k in range(TOP_K):
        y = y + mp_ref[:, k:k + 1] * buf_ref[k]
    o_ref[...] = _rms_rows(y, fin_ref[...]) if final_norm else y


def _combine(dest, x, mp, fin, ys, final_norm, tc):
    t, d = x.shape
    return pl.pallas_call(
        functools.partial(_combine_kernel, n_tokens=t, final_norm=final_norm),
        grid_spec=pltpu.PrefetchScalarGridSpec(
            num_scalar_prefetch=1,
            grid=(t // tc,),
            in_specs=[pl.BlockSpec((tc, d), lambda i, dest: (i, 0)),
                      pl.BlockSpec((tc, LANES), lambda i, dest: (i, 0)),
                      pl.BlockSpec((1, d), lambda i, dest: (0, 0)),
                      pl.BlockSpec(memory_space=pl.ANY)],
            out_specs=pl.BlockSpec((tc, d), lambda i, dest: (i, 0)),
            scratch_shapes=[pltpu.VMEM((TOP_K, tc, d), F32), pltpu.SemaphoreType.DMA(())]),
        out_shape=jax.ShapeDtypeStruct((t, d), F32),
        compiler_params=_cparams(1),
        name="moe_combine",
    )(dest, x, mp, fin, ys)


def _moe(x, g, router, wg, wu, wd, fin, final_norm, tm, tg, tf):
    t, d = x.shape
    n_experts = router.shape[1]
    h, mi, mp, cnt = _router(x, g, router, tm)
    counts = cnt[0, :n_experts].astype(jnp.int32)
    padded = ((counts + tg - 1) // tg) * tg
    ends = jnp.cumsum(padded)
    offs = ends - padded
    dest = jnp.concatenate([offs[mi[:, k]] + mi[:, TOP_K + k] for k in range(TOP_K)]).astype(jnp.int32)
    n_tiles = (TOP_K * t) // tg + n_experts
    n_used = (ends[-1] // tg).astype(jnp.int32)
    tidx = jnp.minimum(jnp.arange(n_tiles, dtype=jnp.int32), n_used - 1)
    tile_expert = jnp.sum((tidx[:, None] * tg >= ends[None, :]).astype(jnp.int32), axis=1)
    xs = _dispatch(dest, h, n_tiles * tg, _pick_tile(t, 512))
    ys = _expert_mlp(tile_expert, n_used.reshape(1), xs, wg, wu, wd, tg, tf)
    return _combine(dest, x, mp, fin, ys, final_norm, _pick_tile(t, 512))


def _pack_w_in(w_in, d_model, mixw):
    qw = ATT_HEADS * HEAD_DIM
    kvw = 2 * ATT_KV_HEADS * HEAD_DIM
    qiw = IDX_HEADS * IDX_DIM
    bcw = 2 * SSM_GROUPS * SSM_STATE
    sizes = (mixw, qw, kvw, qiw, IDX_DIM, IDX_HEADS, 2 * mixw, mixw, mixw + bcw, SSM_HEADS, 4 * d_model)
    offs = np.concatenate([[0], np.cumsum(sizes)])
    assert offs[-1] == w_in.shape[1]
    seg = lambda i: w_in[:, offs[i]:offs[i + 1]]
    u_pool, q, kv, qi, ki, wi, u_conv, z, xbc, dt, gates = (seg(i) for i in range(len(sizes)))
    d = w_in.shape[0]
    zeros = lambda n: jnp.zeros((d, n), w_in.dtype)
    idx_blk = jnp.concatenate([qi, ki, wi, zeros(3 * LANES - qiw - IDX_DIM - IDX_HEADS),
                               dt, zeros(LANES - SSM_HEADS)], axis=1)
    packed = jnp.concatenate([gates, u_conv, u_pool, q, z, xbc[:, :mixw], idx_blk, kv, xbc[:, mixw:]], axis=1)
    widths = dict(gates=4 * d_model, conv=2 * mixw, pool=mixw, q=qw, z=mixw, xs=mixw, idx=4 * LANES,
                  kv=kvw, bc=bcw)
    blocks, off = {}, 0
    for name in ("gates", "conv", "pool", "q", "z", "xs", "idx", "kv", "bc"):
        assert off % widths[name] == 0
        blocks[name] = off // widths[name]
        off += widths[name]
    blocks["dt"] = 3
    return packed.astype(BF16), blocks


def _pad_lanes(v):
    return jnp.zeros((1, LANES), F32).at[0, :v.shape[0]].set(v.astype(F32))


def _pick_tile(n, target):
    t = min(n, target)
    while n % t:
        t //= 2
    return t


def _pick_lane_tile(n, target):
    best = None
    for m in range(LANES, min(n, target) + 1, LANES):
        if n % m == 0:
            best = m
    return n if best is None else best


def kernel(x, norm_mix, w_in, pool_w, pool_scale, conv_dw, conv_b, conv_ln_g, conv_ln_b, ssm_conv_w, ssm_conv_b, ssm_dt_bias, ssm_a_log, ssm_d, ssm_norm, w_br, w_out, norm_ffn, ffn_w_gate, ffn_w_up, ffn_w_down, moe_router, moe_w_gate, moe_w_up, moe_w_down, final_norm):
    bsz, seq, d = x.shape
    depth = norm_mix.shape[0]
    mixw = pool_scale.shape[-1]
    t = bsz * seq
    row = lambda v: v.reshape(1, -1).astype(F32)

    tab_att = _rope_tables(seq, HEAD_DIM, ROPE_DIM, LANES // HEAD_DIM)
    tab_idx = _rope_tables(seq, IDX_DIM, IDX_ROPE_DIM, LANES // IDX_DIM)
    tab_tail = _rope_tables(seq, IDX_DIM, IDX_ROPE_DIM, 1)

    tm = _pick_tile(t, 1024)
    ts_seq = _pick_tile(seq, 1024)
    fin = row(final_norm)

    xf = x.reshape(t, d)
    for layer in range(depth):
        w_packed, blk = _pack_w_in(w_in[layer], d, mixw)
        idx_w = 4 * LANES
        proj, aux = _norm_matmul(xf, row(norm_mix[layer]), w_packed, _pick_tile(t, 2048),
                                 _pick_tile(w_packed.shape[1], 1024), blk["idx"] * idx_w, idx_w)

        y_a = _pool_mixer(proj, blk["pool"], pool_w[layer].astype(BF16), row(pool_scale[layer]), bsz, seq, ts_seq)
        qt, k_r, vt, qit, ki_r, wt = _rope_prep(proj, aux, blk["q"], blk["kv"], tab_att, tab_idx,
                                                tab_tail, bsz, seq, ts_seq)
        y_b = _dsa_attention(qt, qit, wt, k_r, ki_r, vt, bsz, seq)
        y_c = _conformer(proj, blk["conv"], conv_dw[layer], row(conv_b[layer]), row(conv_ln_g[layer]),
                         row(conv_ln_b[layer]), bsz, seq, _pick_tile(seq, 512))
        y_d = _ssd(proj, aux, blk["z"], blk["xs"], blk["bc"], blk["dt"], ssm_conv_w[layer], row(ssm_conv_b[layer]),
                   _pad_lanes(ssm_dt_bias[layer]), _pad_lanes(-jnp.exp(ssm_a_log[layer].astype(F32))),
                   _pad_lanes(ssm_d[layer]), row(ssm_norm[layer]), bsz, seq)
        xf = _merge(xf, y_a, y_b, y_c, y_d, proj, blk["gates"], w_br[layer].astype(BF16),
                    w_out[layer].astype(BF16), _pick_tile(t, 512))

        last = layer == depth - 1
        jj = layer // 2
        if layer % 2 == 0:
            ff = ffn_w_gate.shape[-1]
            xf = _ffn(xf, row(norm_ffn[layer]), ffn_w_gate[jj].astype(BF16), ffn_w_up[jj].astype(BF16),
                      ffn_w_down[jj].astype(BF16), fin, last, _pick_tile(t, 512), _pick_lane_tile(ff, 1408))
        else:
            ff = moe_w_gate.shape[-1]
            xf = _moe(xf, row(norm_ffn[layer]), moe_router[jj], moe_w_gate[jj].astype(BF16),
                      moe_w_up[jj].astype(BF16), moe_w_down[jj].astype(BF16), fin, last, tm,
                      _pick_tile(t, 1024), _pick_lane_tile(ff, 896))
    if depth == 0:
        raise ValueError("depth must be positive")
    return xf.reshape(bsz, seq, d)
```
